```python
import jax, jax.numpy as jnp
from jax import lax
import numpy as np

D_MODEL = 1024
BATCH = 8
SEQ = 8192
DEPTH = 4

GRID_W = 64
CTX_LEN = 256
RMS_EPS = 1e-6
N_DIR = 2
REC_CHUNK = 64
A_HEADS = 4
A_HEAD_DIM = 3 * D_MODEL // 32
A_WIDTH = A_HEADS * A_HEAD_DIM
CONV_W = 3
B_GROUPS = 4
B_WIDTH = D_MODEL // 4
B_GROUP_DIM = B_WIDTH // B_GROUPS
B_CHUNK = 128
C_HEADS = 4
C_WIDTH = D_MODEL - A_WIDTH - B_WIDTH
C_HEAD_DV = C_WIDTH // C_HEADS
C_HEAD_DK = C_HEAD_DV // 2
C_KEY_WIDTH = C_HEADS * C_HEAD_DK
C_GATE_RANK = 16
C_GATE_NORM = 16.0
D_FF = ((8 * D_MODEL // 3 + 255) // 256) * 256
IN_SIZES = (A_WIDTH, A_WIDTH, A_WIDTH, A_WIDTH, N_DIR * 2 * A_HEADS,
            B_WIDTH, B_WIDTH,
            C_KEY_WIDTH, C_KEY_WIDTH, C_WIDTH, C_WIDTH, N_DIR * C_GATE_RANK)
N_IN = sum(IN_SIZES)

kernel_name = 'hybrid_mlstm_sgu_gla_dit'


def rmsnorm(x, g):
    xf = x.astype(jnp.float32)
    y = xf * lax.rsqrt(jnp.mean(xf * xf, axis=-1, keepdims=True) + RMS_EPS)
    return (y * g.astype(jnp.float32)).astype(x.dtype)


def modulate(h, shift, scale):
    return h * (1 + scale) + shift


def centred_dwconv(x, w):
    pad = CONV_W // 2
    t_len = x.shape[1]
    xp = jnp.pad(x, ((0, 0), (pad, pad), (0, 0)))
    return sum(w[j] * xp[:, j:j + t_len] for j in range(CONV_W))


def split_heads(a, n_heads):
    bsz, t_len, _ = a.shape
    return a.reshape(bsz, t_len, n_heads, -1).transpose(0, 2, 1, 3)


def to_chunks(a):
    t_len = a.shape[2]
    a = a.reshape(a.shape[:2] + (t_len // REC_CHUNK, REC_CHUNK) + a.shape[3:])
    return jnp.moveaxis(a, 2, 0)


def from_chunks(a):
    a = jnp.moveaxis(a, 0, 2)
    return a.reshape(a.shape[:2] + (a.shape[2] * a.shape[3],) + a.shape[4:])


def raster_to_column(h, rows):
    bsz, t_len, d = h.shape
    return h.reshape(bsz, rows, GRID_W, d).transpose(0, 2, 1, 3).reshape(bsz, t_len, d)


def column_to_raster(h, rows):
    bsz, t_len, d = h.shape
    return h.reshape(bsz, GRID_W, rows, d).transpose(0, 2, 1, 3).reshape(bsz, t_len, d)


def mlstm_scan(q, k, v, log_i, log_f, state):
    tril = jnp.tril(jnp.ones((REC_CHUNK, REC_CHUNK), dtype=bool))

    def step(carry, xs):
        C, n, m = carry
        qc, kc, vc, ic, fc = xs
        b = jnp.cumsum(fc, axis=-1)
        d_log = jnp.where(tril, b[..., :, None] - b[..., None, :] + ic[..., None, :], -jnp.inf)
        inter_log = b + m[..., None]
        m_t = jnp.maximum(inter_log, jnp.max(d_log, axis=-1))
        s = jnp.einsum('bhtd,bhsd->bhts', qc, kc) * jnp.exp(d_log - m_t[..., None])
        inter = jnp.exp(inter_log - m_t)
        num = jnp.einsum('bhts,bhse->bhte', s, vc) + inter[..., None] * jnp.einsum('bhed,bhtd->bhte', C, qc)
        den = jnp.sum(s, axis=-1) + inter * jnp.einsum('bhd,bhtd->bht', n, qc)
        h = num / jnp.maximum(jnp.abs(den), jnp.exp(-m_t))[..., None]
        b_last = b[..., -1]
        w_log = b_last[..., None] - b + ic
        m_new = jnp.maximum(b_last + m, jnp.max(w_log, axis=-1))
        decay = jnp.exp(b_last + m - m_new)
        w = jnp.exp(w_log - m_new[..., None])
        C_new = decay[..., None, None] * C + jnp.einsum('bhs,bhse,bhsd->bhed', w, vc, kc)
        n_new = decay[..., None] * n + jnp.einsum('bhs,bhsd->bhd', w, kc)
        return (C_new, n_new, m_new), h

    state, h = lax.scan(step, state, tuple(to_chunks(a) for a in (q, k, v, log_i, log_f)))
    return state, from_chunks(h)


def gla_scan(q, k, v, log_a, S):
    tril = jnp.tril(jnp.ones((REC_CHUNK, REC_CHUNK), dtype=bool))[:, :, None]

    def step(S, xs):
        qc, kc, vc, ac = xs
        b = jnp.cumsum(ac, axis=2)
        rel = jnp.where(tril, b[:, :, :, None, :] - b[:, :, None, :, :], -jnp.inf)
        att = jnp.einsum('bhtd,bhsd,bhtsd->bhts', qc, kc, jnp.exp(rel))
        o = jnp.einsum('bhts,bhse->bhte', att, vc) + jnp.einsum('bhtd,bhde->bhte', qc * jnp.exp(b), S)
        b_last = b[:, :, -1:, :]
        S_new = jnp.exp(b_last[:, :, 0, :])[..., None] * S + \
            jnp.einsum('bhsd,bhse->bhde', kc * jnp.exp(b_last - b), vc)
        return S_new, o

    S, o = lax.scan(step, S, tuple(to_chunks(a) for a in (q, k, v, log_a)))
    return S, from_chunks(o)


def bidirectional(scan_fn, ctx_shared, lat_shared, ctx_gates, lat_gates, init):
    outs_ctx, outs_lat = [], []
    for d in range(N_DIR):
        flip = (lambda a: jnp.flip(a, axis=2)) if d == 1 else (lambda a: a)
        state, h_ctx = scan_fn(*[flip(a) for a in ctx_shared + ctx_gates[d]], init)
        _, h_lat = scan_fn(*[flip(a) for a in lat_shared + lat_gates[d]], state)
        outs_ctx.append(flip(h_ctx))
        outs_lat.append(flip(h_lat))
    return outs_ctx[0] + outs_ctx[1], outs_lat[0] + outs_lat[1]


def mixer_streams(h, w_in, mlstm_conv, mlstm_gate_b, gla_w_gk2, gla_b_gk):
    bsz, t_len, _ = h.shape
    points = [int(p) for p in np.cumsum(IN_SIZES)[:-1]]
    qa, ka, va, oa, ga, ub, vb, qc, kc, vc, gc, gkc = jnp.split(h @ w_in, points, axis=-1)
    f32 = lambda a: a.astype(jnp.float32)
    qk = jax.nn.silu(centred_dwconv(jnp.concatenate([qa, ka], axis=-1), mlstm_conv))
    qa, ka = jnp.split(qk, 2, axis=-1)
    a_shared = (f32(split_heads(qa, A_HEADS)) * A_HEAD_DIM ** -0.5,
                f32(split_heads(ka, A_HEADS)), f32(split_heads(va, A_HEADS)))
    g = f32(ga).reshape(bsz, t_len, N_DIR, 2, A_HEADS) + f32(mlstm_gate_b)
    g = g.transpose(2, 3, 0, 4, 1)
    a_gates = tuple((g[d, 0], jax.nn.log_sigmoid(g[d, 1])) for d in range(N_DIR))
    c_shared = (f32(split_heads(qc, C_HEADS)) * C_HEAD_DK ** -0.5,
                f32(split_heads(kc, C_HEADS)), f32(split_heads(vc, C_HEADS)))
    gk = jnp.einsum('btdr,drk->dbtk', f32(gkc).reshape(bsz, t_len, N_DIR, C_GATE_RANK), f32(gla_w_gk2)) \
        + f32(gla_b_gk)[:, None, None, :]
    log_a = jax.nn.log_sigmoid(gk) / C_GATE_NORM
    c_gates = tuple((split_heads(log_a[d], C_HEADS),) for d in range(N_DIR))
    return a_shared, a_gates, c_shared, c_gates, (oa, gc, ub, vb)


def spatial_gating(ub, vb, norm_g, w_s, b_s):
    bsz, t_len, _ = vb.shape
    u = jax.nn.gelu(ub)
    v = rmsnorm(jax.nn.gelu(vb), norm_g).reshape(bsz, t_len // B_CHUNK, B_CHUNK, B_GROUPS, B_GROUP_DIM)
    mixed = jnp.einsum('gts,bnsgc->bntgc', w_s, v) + b_s.T[:, :, None]
    return u * mixed.reshape(bsz, t_len, B_WIDTH)


def mixer_output(a_h, c_h, oa, gc, ub, vb, mlstm_norm_g, gla_norm_g, sgu_norm_g, sgu_w, sgu_b, w_out):
    bsz, t_len, _ = oa.shape
    dt = oa.dtype
    a = rmsnorm(a_h.transpose(0, 2, 1, 3), mlstm_norm_g.reshape(A_HEADS, A_HEAD_DIM))
    a = a.reshape(bsz, t_len, A_WIDTH).astype(dt) * jax.nn.sigmoid(oa)
    cc = rmsnorm(c_h.transpose(0, 2, 1, 3), gla_norm_g).reshape(bsz, t_len, C_WIDTH).astype(dt) * jax.nn.silu(gc)
    s = spatial_gating(ub, vb, sgu_norm_g, sgu_w, sgu_b)
    return jnp.concatenate([a, s, cc], axis=-1) @ w_out


def mixer(h_ctx, h_lat, w_in, mlstm_conv, mlstm_gate_b, mlstm_norm_g, gla_w_gk2, gla_b_gk, gla_norm_g,
          sgu_norm_g, sgu_w, sgu_b, w_out, need_ctx):
    sc = mixer_streams(h_ctx, w_in, mlstm_conv, mlstm_gate_b, gla_w_gk2, gla_b_gk)
    sl = mixer_streams(h_lat, w_in, mlstm_conv, mlstm_gate_b, gla_w_gk2, gla_b_gk)
    bsz = h_lat.shape[0]
    a_init = (jnp.zeros((bsz, A_HEADS, A_HEAD_DIM, A_HEAD_DIM), jnp.float32),
              jnp.zeros((bsz, A_HEADS, A_HEAD_DIM), jnp.float32),
              jnp.zeros((bsz, A_HEADS), jnp.float32))
    c_init = jnp.zeros((bsz, C_HEADS, C_HEAD_DK, C_HEAD_DV), jnp.float32)
    a_ctx, a_lat = bidirectional(mlstm_scan, sc[0], sl[0], sc[1], sl[1], a_init)
    c_ctx, c_lat = bidirectional(gla_scan, sc[2], sl[2], sc[3], sl[3], c_init)
    o_lat = mixer_output(a_lat, c_lat, *sl[4], mlstm_norm_g, gla_norm_g, sgu_norm_g, sgu_w, sgu_b, w_out)
    o_ctx = mixer_output(a_ctx, c_ctx, *sc[4], mlstm_norm_g, gla_norm_g, sgu_norm_g, sgu_w, sgu_b, w_out) \
        if need_ctx else None
    return o_ctx, o_lat


def swiglu(h, w_in, w_out):
    gate, up = jnp.split(h @ w_in, 2, axis=-1)
    return (jax.nn.silu(gate) * up) @ w_out


def setup_inputs(seed: int = 0) -> dict:
    key = jax.random.key(seed)
    ks = jax.random.split(key, 24)
    nrm = lambda k, shape, scale: scale * jax.random.normal(k, shape, jnp.float32)
    gate_base = jnp.array([0.0, 3.0], jnp.float32)[None, :, None]
    return {
        'x': nrm(ks[0], (BATCH, SEQ, D_MODEL), 1.0),
        'c': nrm(ks[1], (BATCH, D_MODEL), 1.0),
        'ctx': nrm(ks[2], (BATCH, CTX_LEN, D_MODEL), 1.0),
        'c_ctx': nrm(ks[3], (D_MODEL,), 1.0),
        'norm1_g': 1.0 + nrm(ks[4], (DEPTH, D_MODEL), 0.02),
        'norm2_g': 1.0 + nrm(ks[5], (DEPTH, D_MODEL), 0.02),
        'w_ada': nrm(ks[6], (DEPTH, D_MODEL, 6 * D_MODEL), 0.5 * D_MODEL ** -0.5),
        'b_ada': nrm(ks[7], (DEPTH, 6 * D_MODEL), 0.02),
        'w_in': nrm(ks[8], (DEPTH, D_MODEL, N_IN), D_MODEL ** -0.5),
        'mlstm_conv': nrm(ks[9], (DEPTH, CONV_W, 2 * A_WIDTH), CONV_W ** -0.5),
        'mlstm_gate_b': gate_base + nrm(ks[10], (DEPTH, N_DIR, 2, A_HEADS), 0.3),
        'mlstm_norm_g': 1.0 + nrm(ks[11], (DEPTH, A_WIDTH), 0.02),
        'gla_w_gk2': nrm(ks[12], (DEPTH, N_DIR, C_GATE_RANK, C_KEY_WIDTH), C_GATE_RANK ** -0.5),
        'gla_b_gk': nrm(ks[13], (DEPTH, N_DIR, C_KEY_WIDTH), 0.1),
        'gla_norm_g': 1.0 + nrm(ks[14], (DEPTH, C_HEAD_DV), 0.02),
        'sgu_norm_g': 1.0 + nrm(ks[15], (DEPTH, B_WIDTH), 0.02),
        'sgu_w': nrm(ks[16], (DEPTH, B_GROUPS, B_CHUNK, B_CHUNK), B_CHUNK ** -0.5),
        'sgu_b': 1.0 + nrm(ks[17], (DEPTH, B_GROUPS, B_CHUNK), 0.02),
        'w_out': nrm(ks[18], (DEPTH, D_MODEL, D_MODEL), D_MODEL ** -0.5),
        'w_ffn_in': nrm(ks[19], (DEPTH, D_MODEL, 2 * D_FF), D_MODEL ** -0.5),
        'w_ffn_out': nrm(ks[20], (DEPTH, D_FF, D_MODEL), D_FF ** -0.5),
        'final_g': 1.0 + nrm(ks[21], (D_MODEL,), 0.02),
    }


def reference(x, c, ctx, c_ctx, norm1_g, norm2_g, w_ada, b_ada, w_in, mlstm_conv, mlstm_gate_b,
              mlstm_norm_g, gla_w_gk2, gla_b_gk, gla_norm_g, sgu_norm_g, sgu_w, sgu_b, w_out,
              w_ffn_in, w_ffn_out, final_g):
    rows = x.shape[1] // GRID_W
    x_lat, x_ctx = x, ctx
    s_lat = jax.nn.silu(c)
    s_ctx = jax.nn.silu(c_ctx)
    for l in range(DEPTH):
        need_ctx = l < DEPTH - 1
        mod_lat = (s_lat @ w_ada[l] + b_ada[l])[:, None, :]
        mod_ctx = (s_ctx @ w_ada[l] + b_ada[l])[None, None, :]
        sh1, sc1, g1, sh2, sc2, g2 = jnp.split(mod_lat, 6, axis=-1)
        csh1, csc1, cg1, csh2, csc2, cg2 = jnp.split(mod_ctx, 6, axis=-1)
        h_lat = modulate(rmsnorm(x_lat, norm1_g[l]), sh1, sc1)
        h_ctx = modulate(rmsnorm(x_ctx, norm1_g[l]), csh1, csc1)
        column_major = l % 2 == 1
        if column_major:
            h_lat = raster_to_column(h_lat, rows)
        o_ctx, o_lat = mixer(h_ctx, h_lat, w_in[l], mlstm_conv[l], mlstm_gate_b[l], mlstm_norm_g[l],
                             gla_w_gk2[l], gla_b_gk[l], gla_norm_g[l], sgu_norm_g[l], sgu_w[l], sgu_b[l],
                             w_out[l], need_ctx)
        if column_major:
            o_lat = column_to_raster(o_lat, rows)
        x_lat = x_lat + g1 * o_lat
        x_lat = x_lat + g2 * swiglu(modulate(rmsnorm(x_lat, norm2_g[l]), sh2, sc2), w_ffn_in[l], w_ffn_out[l])
        if need_ctx:
            x_ctx = x_ctx + cg1 * o_ctx
            x_ctx = x_ctx + cg2 * swiglu(modulate(rmsnorm(x_ctx, norm2_g[l]), csh2, csc2),
                                         w_ffn_in[l], w_ffn_out[l])
    return rmsnorm(x_lat, final_g)
```

```python
import functools
import math

import jax
import jax.numpy as jnp
from jax import lax
from jax.experimental import pallas as pl
from jax.experimental.pallas import tpu as pltpu

F32 = jnp.float32
BF16 = jnp.bfloat16
HIGHEST = lax.Precision.HIGHEST

LANES = 128
SUBLANES = 8
VMEM_LIMIT_BYTES = 56 * 1024 * 1024

GRID_W = 64
RMS_EPS = 1e-6
A_HEADS = 4
A_HEAD_DIM = 96
B_GROUPS = 4
B_GROUP_DIM = 64
B_WIDTH = B_GROUPS * B_GROUP_DIM
C_HEADS = 4
C_HEAD_DK = 48
C_HEAD_DV = 96
C_GATE_RANK = 16
C_GATE_NORM = 16.0
N_DIR = 2

HEAD_PAD = LANES
A_PAD = A_HEADS * HEAD_PAD
CK_PAD = 64
CK_W = C_HEADS * CK_PAD
CV_W = C_HEADS * HEAD_PAD
ONES_LANE = A_HEAD_DIM

SEQ_BLOCK = 128
SUB = 16
N_SUB = SEQ_BLOCK // SUB
HALO = SUBLANES

_IN_LAYOUT = (("qa", A_PAD, 1), ("ka", A_PAD, 1), ("va", A_PAD, 1), ("oa", A_PAD, 1),
              ("ga", LANES, N_DIR),
              ("ub", B_WIDTH, 1), ("vb", B_WIDTH, 1),
              ("qc", CK_W, 1), ("kc", CK_W, 1), ("vc", CV_W, 1), ("gc", CV_W, 1),
              ("gk", LANES, N_DIR))
N_IN_PAD = sum(w * n for _, w, n in _IN_LAYOUT)


def _cparams(n_axes):
    return pltpu.CompilerParams(dimension_semantics=("arbitrary",) * n_axes,
                                vmem_limit_bytes=VMEM_LIMIT_BYTES)


def _log_sigmoid(x):
    return jnp.minimum(x, 0.0) - jnp.log1p(jnp.exp(-jnp.abs(x)))


def _silu(x):
    return x * jax.nn.sigmoid(x)


def _gelu_tanh(x):
    c = math.sqrt(2.0 / math.pi)
    return 0.5 * x * (1.0 + jnp.tanh(c * (x + 0.044715 * (x * x * x))))


def _rms_scale(x, n):
    return lax.rsqrt(jnp.sum(x * x, axis=-1, keepdims=True) * (1.0 / n) + RMS_EPS)


def _ada_kernel(s_ref, w_ref, b_ref, o_ref):
    s = _silu(s_ref[...])
    o_ref[0] = jnp.dot(s, w_ref[0], precision=HIGHEST, preferred_element_type=F32) + b_ref[0]


def _ada_call(cond, w_ada, b_ada):
    depth, d, n6 = w_ada.shape
    nb = cond.shape[0]
    tn = 1536
    return pl.pallas_call(
        _ada_kernel,
        grid=(depth, n6 // tn),
        in_specs=[pl.BlockSpec((nb, d), lambda l, j: (0, 0)),
                  pl.BlockSpec((1, d, tn), lambda l, j: (l, 0, j)),
                  pl.BlockSpec((1, 1, tn), lambda l, j: (l, 0, j))],
        out_specs=pl.BlockSpec((1, nb, tn), lambda l, j: (l, 0, j)),
        out_shape=jax.ShapeDtypeStruct((depth, nb, n6), F32),
        compiler_params=_cparams(2),
        name="ada_mod",
    )(cond, w_ada, b_ada.reshape(depth, 1, n6))


def _proj_in_kernel(x_ref, mod_ref, g_ref, w_ref, *out_refs):
    x = x_ref[0]
    y = x * _rms_scale(x, x.shape[-1]) * g_ref[...]
    h = (y * (1.0 + mod_ref[0, 1:2, :]) + mod_ref[0, 0:1, :]).astype(BF16)
    off = 0
    for (_, width, n_dir), o_ref in zip(_IN_LAYOUT, out_refs):
        for di in range(n_dir):
            res = jnp.dot(h, w_ref[:, off:off + width], preferred_element_type=F32)
            if n_dir == 1:
                o_ref[0] = res
            else:
                o_ref[di, 0] = res
            off += width


def _proj_in_call(x, mod, g, w, tm):
    bsz, t, d = x.shape
    mod_b = mod.shape[0]
    mod_map = (lambda b, i: (b, 0, 0)) if mod_b > 1 else (lambda b, i: (0, 0, 0))
    out_shapes, out_specs = [], []
    for _, wd, n_dir in _IN_LAYOUT:
        if n_dir == 1:
            out_shapes.append(jax.ShapeDtypeStruct((bsz, t, wd), F32))
            out_specs.append(pl.BlockSpec((1, tm, wd), lambda b, i: (b, i, 0)))
        else:
            out_shapes.append(jax.ShapeDtypeStruct((n_dir, bsz, t, wd), F32))
            out_specs.append(pl.BlockSpec((n_dir, 1, tm, wd), lambda b, i: (0, b, i, 0)))
    outs = pl.pallas_call(
        _proj_in_kernel,
        grid=(bsz, t // tm),
        in_specs=[pl.BlockSpec((1, tm, d), lambda b, i: (b, i, 0)),
                  pl.BlockSpec((1, 6, d), mod_map),
                  pl.BlockSpec((1, d), lambda b, i: (0, 0)),
                  pl.BlockSpec((d, N_IN_PAD), lambda b, i: (0, 0))],
        out_specs=out_specs,
        out_shape=out_shapes,
        compiler_params=_cparams(2),
        name="proj_in",
    )(x, mod, g, w)
    return dict(zip([n for n, _, _ in _IN_LAYOUT], outs))


class _SeqView:
    def __init__(self, t_len, column_major):
        self.column_major = column_major
        self.nblk = t_len // SEQ_BLOCK
        self.t_len = t_len
        if column_major:
            assert t_len // GRID_W == SEQ_BLOCK

    def view(self, a):
        if not self.column_major:
            return a
        c = a.shape[-1]
        return a.reshape(a.shape[:-2] + (SEQ_BLOCK, GRID_W * c))

    def unview(self, a, c):
        if not self.column_major:
            return a
        return a.reshape(a.shape[:-2] + (self.t_len, c))

    def main(self, j):
        return (0, j) if self.column_major else (j, 0)

    def prev_halo(self, j):
        jp = jnp.maximum(j - 1, 0)
        if self.column_major:
            return (SEQ_BLOCK // HALO - 1, jp)
        return (jnp.maximum(j * (SEQ_BLOCK // HALO) - 1, 0), 0)

    def next_halo(self, j):
        jn = jnp.minimum(j + 1, self.nblk - 1)
        if self.column_major:
            return (0, jn)
        return (jn * (SEQ_BLOCK // HALO), 0)


def _dir_block(d, j, nblk):
    return jnp.where(d == 0, j, nblk - 1 - j)


def _mlstm_kernel(nblk, q_ref, qp_ref, qn_ref, k_ref, kp_ref, kn_ref, v_ref, g_ref,
                  cw_ref, gb_ref, cinit_ref, minit_ref,
                  h_ref, cfin_ref, mfin_ref, c_sc, m_sc):
    d = pl.program_id(1)
    j = pl.program_id(2)
    blk = _dir_block(d, j, nblk)
    fwd = d == 0
    sign = 1 - 2 * d
    L = SEQ_BLOCK

    @pl.when(j == 0)
    def _():
        c_sc[...] = cinit_ref[0, 0]
        m_sc[...] = minit_ref[0, 0]

    row = lax.broadcasted_iota(jnp.int32, (L, 1), 0)
    has_prev = (blk > 0).astype(F32)
    has_next = (blk < nblk - 1).astype(F32)

    def conv_silu(x_ref, xp_ref, xn_ref, w):
        x = x_ref[0]
        prev_row = xp_ref[0][HALO - 1:HALO, :] * has_prev
        next_row = xn_ref[0][0:1, :] * has_next
        x_dn = jnp.where(row == 0, prev_row, pltpu.roll(x, 1, 0))
        x_up = jnp.where(row == L - 1, next_row, pltpu.roll(x, L - 1, 0))
        return _silu(w[0:1, :] * x_dn + w[1:2, :] * x + w[2:3, :] * x_up)

    q = (conv_silu(q_ref, qp_ref, qn_ref, cw_ref[0]) * (A_HEAD_DIM ** -0.5)).astype(BF16)
    k = conv_silu(k_ref, kp_ref, kn_ref, cw_ref[1]).astype(BF16)
    lane = lax.broadcasted_iota(jnp.int32, (1, A_PAD), 1) % HEAD_PAD
    v = jnp.where(lane == ONES_LANE, 1.0, v_ref[0])

    g = g_ref[0, 0] + gb_ref[0]
    logf = _log_sigmoid(g)
    ri = lax.broadcasted_iota(jnp.int32, (L, L), 0)
    ci = lax.broadcasted_iota(jnp.int32, (L, L), 1)
    past = (ri - ci) * sign >= 0
    bc = jnp.dot(past.astype(F32), logf, precision=HIGHEST, preferred_element_type=F32)
    g_t = g.T
    bc_t = bc.T

    out_lane = lax.broadcasted_iota(jnp.int32, (1, HEAD_PAD), 1)
    for h in range(A_HEADS):
        sl = slice(h * HEAD_PAD, (h + 1) * HEAD_PAD)
        fcol = A_HEADS + h
        qh, kh, vh = q[:, sl], k[:, sl], v[:, sl]
        b_col = bc[:, fcol:fcol + 1]
        b_row = bc_t[fcol:fcol + 1, :]
        i_col = g[:, h:h + 1]
        i_row = g_t[h:h + 1, :]
        m_prev = m_sc[h:h + 1, 0:1]
        b_last = jnp.where(fwd, bc[L - 1:L, fcol:fcol + 1], bc[0:1, fcol:fcol + 1])
        ct = c_sc[h]

        d_log = jnp.where(past, b_col - b_row + i_row, -jnp.inf)
        inter_log = b_col + m_prev
        m_t = jnp.maximum(inter_log, jnp.max(d_log, axis=1, keepdims=True))
        qk = lax.dot_general(qh, kh, (((1,), (1,)), ((), ())), preferred_element_type=F32)
        s = qk * jnp.exp(d_log - m_t)
        inter = jnp.exp(inter_log - m_t)
        num = jnp.dot(s.astype(BF16), vh.astype(BF16), preferred_element_type=F32) \
            + inter * jnp.dot(qh, ct.astype(BF16), preferred_element_type=F32)
        den = num[:, ONES_LANE:ONES_LANE + 1]
        inv = 1.0 / jnp.maximum(jnp.abs(den), jnp.exp(-m_t))
        h_ref[0, 0, :, sl] = jnp.where(out_lane < A_HEAD_DIM, num * inv, 0.0)

        w_log = b_last - b_col + i_col
        m_new = jnp.maximum(b_last + m_prev, jnp.max(w_log, axis=0, keepdims=True))
        decay = jnp.exp(b_last + m_prev - m_new)
        wv = (jnp.exp(w_log - m_new) * vh).astype(BF16)
        upd = lax.dot_general(kh, wv, (((0,), (0,)), ((), ())), preferred_element_type=F32)
        c_sc[h] = decay * ct + upd
        m_sc[h:h + 1, :] = jnp.broadcast_to(m_new, (1, LANES))

    @pl.when(j == nblk - 1)
    def _():
        cfin_ref[0, 0] = c_sc[...]
        mfin_ref[0, 0] = m_sc[...]


def _mlstm_call(st, conv_w, gate_b, cinit, minit, sv):
    bsz = st["qa"].shape[0]
    nblk = sv.nblk
    t_len = sv.t_len

    def main_map(b, d, j):
        return (b,) + sv.main(_dir_block(d, j, nblk))

    def prev_map(b, d, j):
        return (b,) + sv.prev_halo(_dir_block(d, j, nblk))

    def next_map(b, d, j):
        return (b,) + sv.next_halo(_dir_block(d, j, nblk))

    def dir_map(b, d, j):
        return (d, b) + sv.main(_dir_block(d, j, nblk))

    def state_map(b, d, j):
        return (b, d, 0, 0, 0)

    def mstate_map(b, d, j):
        return (b, d, 0, 0)

    blk = lambda c: pl.BlockSpec((1, SEQ_BLOCK, c), main_map)
    halo_p = pl.BlockSpec((1, HALO, A_PAD), prev_map)
    halo_n = pl.BlockSpec((1, HALO, A_PAD), next_map)
    qa, ka, va = sv.view(st["qa"]), sv.view(st["ka"]), sv.view(st["va"])
    ga = sv.view(st["ga"])
    h, cfin, mfin = pl.pallas_call(
        functools.partial(_mlstm_kernel, nblk),
        grid=(bsz, N_DIR, nblk),
        in_specs=[blk(A_PAD), halo_p, halo_n, blk(A_PAD), halo_p, halo_n, blk(A_PAD),
                  pl.BlockSpec((1, 1, SEQ_BLOCK, LANES), dir_map),
                  pl.BlockSpec((2, 3, A_PAD), lambda b, d, j: (0, 0, 0)),
                  pl.BlockSpec((1, 1, LANES), lambda b, d, j: (d, 0, 0)),
                  pl.BlockSpec((1, 1, A_HEADS, HEAD_PAD, HEAD_PAD), state_map),
                  pl.BlockSpec((1, 1, SUBLANES, LANES), mstate_map)],
        out_specs=[pl.BlockSpec((1, 1, SEQ_BLOCK, A_PAD), dir_map),
                   pl.BlockSpec((1, 1, A_HEADS, HEAD_PAD, HEAD_PAD), state_map),
                   pl.BlockSpec((1, 1, SUBLANES, LANES), mstate_map)],
        out_shape=[jax.ShapeDtypeStruct((N_DIR,) + qa.shape, F32),
                   jax.ShapeDtypeStruct(cinit.shape, F32),
                   jax.ShapeDtypeStruct(minit.shape, F32)],
        scratch_shapes=[pltpu.VMEM((A_HEADS, HEAD_PAD, HEAD_PAD), F32),
                        pltpu.VMEM((SUBLANES, LANES), F32)],
        compiler_params=_cparams(3),
        name="mlstm_scan",
    )(qa, qa, qa, ka, ka, ka, va, ga, conv_w, gate_b, cinit, minit)
    return sv.unview(h, A_PAD), cfin, mfin


def _gla_kernel(nblk, q_ref, k_ref, v_ref, gk_ref, w2_ref, bgk_ref, hexp_ref, sinit_ref,
                o_ref, sfin_ref, s_sc, qs_sc, qe_sc, ke_sc, b_sc, dec_sc):
    d = pl.program_id(1)
    j = pl.program_id(2)
    fwd = d == 0
    sign = 1 - 2 * d
    L = SEQ_BLOCK

    @pl.when(j == 0)
    def _():
        s_sc[...] = sinit_ref[0, 0]

    gk = jnp.dot(gk_ref[0, 0], w2_ref[0], precision=HIGHEST, preferred_element_type=F32) + bgk_ref[0]
    la = _log_sigmoid(gk) * (1.0 / C_GATE_NORM)
    ri = lax.broadcasted_iota(jnp.int32, (L, L), 0)
    ci = lax.broadcasted_iota(jnp.int32, (L, L), 1)
    same = (ri // SUB) == (ci // SUB)
    past = (ri - ci) * sign >= 0
    b = jnp.dot((same & past).astype(F32), la, precision=HIGHEST, preferred_element_type=F32)
    btot = jnp.dot(same.astype(F32), la, precision=HIGHEST, preferred_element_type=F32)
    qs = q_ref[0] * (C_HEAD_DK ** -0.5)
    qs_sc[...] = qs
    qe_sc[...] = qs * jnp.exp(b)
    ke_sc[...] = k_ref[0] * jnp.exp(btot - b)
    b_sc[...] = b
    dec_sc[...] = jnp.exp(btot)

    er = lax.broadcasted_iota(jnp.int32, (CV_W, CK_W), 0) // HEAD_PAD
    dc = lax.broadcasted_iota(jnp.int32, (CV_W, CK_W), 1) // CK_PAD
    head_mask = er == dc
    tr = lax.broadcasted_iota(jnp.int32, (SUB, 1), 0)

    def sub_step(i, carry):
        sub = jnp.where(fwd, i, N_SUB - 1 - i)
        r0 = pl.multiple_of(sub * SUB, SUB)
        rows = pl.ds(r0, SUB)
        k_s = k_ref[0, rows, :]
        v_s = v_ref[0, rows, :]
        b_s = b_sc[rows, :]
        q_s = qs_sc[rows, :]
        st = s_sc[...]
        o = lax.dot_general(qe_sc[rows, :].astype(BF16), st.astype(BF16),
                            (((1,), (1,)), ((), ())), preferred_element_type=F32)
        parts = []
        for s in range(SUB):
            ok = (tr - s) * sign >= 0
            rel = jnp.where(ok, b_s - b_s[s:s + 1, :], -jnp.inf)
            parts.append(q_s * jnp.exp(rel) * k_s[s:s + 1, :])
        p = jnp.concatenate(parts, axis=0).astype(BF16)
        r = jnp.dot(p, hexp_ref[...], preferred_element_type=F32)
        for s in range(SUB):
            o = o + r[s * SUB:(s + 1) * SUB, :] * v_s[s:s + 1, :]
        o_ref[0, 0, rows, :] = o
        upd = lax.dot_general(v_s.astype(BF16), ke_sc[rows, :].astype(BF16),
                              (((0,), (0,)), ((), ())), preferred_element_type=F32)
        s_sc[...] = dec_sc[pl.ds(r0, 1), :] * st + jnp.where(head_mask, upd, 0.0)
        return carry

    lax.fori_loop(0, N_SUB, sub_step, 0)

    @pl.when(j == nblk - 1)
    def _():
        sfin_ref[0, 0] = s_sc[...]


def _gla_call(st, w_gk2, b_gk, hexp, sinit, sv):
    bsz = st["qc"].shape[0]
    nblk = sv.nblk

    def main_map(b, d, j):
        return (b,) + sv.main(_dir_block(d, j, nblk))

    def dir_map(b, d, j):
        return (d, b) + sv.main(_dir_block(d, j, nblk))

    def state_map(b, d, j):
        return (b, d, 0, 0)

    blk = lambda c: pl.BlockSpec((1, SEQ_BLOCK, c), main_map)
    qc, kc, vc, gk = sv.view(st["qc"]), sv.view(st["kc"]), sv.view(st["vc"]), sv.view(st["gk"])
    o, sfin = pl.pallas_call(
        functools.partial(_gla_kernel, nblk),
        grid=(bsz, N_DIR, nblk),
        in_specs=[blk(CK_W), blk(CK_W), blk(CV_W),
                  pl.BlockSpec((1, 1, SEQ_BLOCK, LANES), dir_map),
                  pl.BlockSpec((1, LANES, CK_W), lambda b, d, j: (d, 0, 0)),
                  pl.BlockSpec((1, 1, CK_W), lambda b, d, j: (d, 0, 0)),
                  pl.BlockSpec((CK_W, CV_W), lambda b, d, j: (0, 0)),
                  pl.BlockSpec((1, 1, CV_W, CK_W), state_map)],
        out_specs=[pl.BlockSpec((1, 1, SEQ_BLOCK, CV_W), dir_map),
                   pl.BlockSpec((1, 1, CV_W, CK_W), state_map)],
        out_shape=[jax.ShapeDtypeStruct((N_DIR,) + vc.shape, F32),
                   jax.ShapeDtypeStruct(sinit.shape, F32)],
        scratch_shapes=[pltpu.VMEM((CV_W, CK_W), F32),
                        pltpu.VMEM((SEQ_BLOCK, CK_W), F32),
                        pltpu.VMEM((SEQ_BLOCK, CK_W), F32),
                        pltpu.VMEM((SEQ_BLOCK, CK_W), F32),
                        pltpu.VMEM((SEQ_BLOCK, CK_W), F32),
                        pltpu.VMEM((SEQ_BLOCK, CK_W), F32)],
        compiler_params=_cparams(3),
        name="gla_scan",
    )(qc, kc, vc, gk, w_gk2, b_gk, hexp, sinit)
    return sv.unview(o, CV_W), sfin


def _sgu_kernel(u_ref, v_ref, g_ref, w_ref, bias_ref, o_ref):
    u = _gelu_tanh(u_ref[0])
    v = _gelu_tanh(v_ref[0])
    v = v * _rms_scale(v, B_WIDTH) * g_ref[...]
    grp = lax.broadcasted_iota(jnp.int32, (1, B_WIDTH), 1) // B_GROUP_DIM
    mixed = bias_ref[...]
    for gi in range(B_GROUPS):
        vg = jnp.where(grp == gi, v, 0.0).astype(BF16)
        mixed = mixed + jnp.dot(w_ref[gi], vg, preferred_element_type=F32)
    o_ref[0] = u * mixed


def _sgu_call(st, norm_g, w_s, bias, sv):
    bsz = st["ub"].shape[0]

    def main_map(b, j):
        return (b,) + sv.main(j)

    blk = pl.BlockSpec((1, SEQ_BLOCK, B_WIDTH), main_map)
    ub, vb = sv.view(st["ub"]), sv.view(st["vb"])
    o = pl.pallas_call(
        _sgu_kernel,
        grid=(bsz, sv.nblk),
        in_specs=[blk, blk,
                  pl.BlockSpec((1, B_WIDTH), lambda b, j: (0, 0)),
                  pl.BlockSpec((B_GROUPS, SEQ_BLOCK, SEQ_BLOCK), lambda b, j: (0, 0, 0)),
                  pl.BlockSpec((SEQ_BLOCK, B_WIDTH), lambda b, j: (0, 0))],
        out_specs=blk,
        out_shape=jax.ShapeDtypeStruct(ub.shape, F32),
        compiler_params=_cparams(2),
        name="sgu",
    )(ub, vb, norm_g, w_s, bias)
    return sv.unview(o, B_WIDTH)


def _mix_out_kernel(x_ref, mod_ref, ha_ref, hc_ref, oa_ref, gc_ref, s_ref, na_ref, nc_ref, w_ref, o_ref):
    def head_norm(hsum, gain):
        outs = []
        for h in range(A_HEADS):
            sl = slice(h * HEAD_PAD, (h + 1) * HEAD_PAD)
            xh = hsum[:, sl]
            outs.append(xh * _rms_scale(xh, A_HEAD_DIM) * gain[:, sl])
        return outs

    a = head_norm(ha_ref[0, 0] + ha_ref[1, 0], na_ref[...])
    c = head_norm(hc_ref[0, 0] + hc_ref[1, 0], nc_ref[...])
    acc = jnp.dot(s_ref[0].astype(BF16), w_ref[A_PAD:A_PAD + B_WIDTH, :], preferred_element_type=F32)
    for h in range(A_HEADS):
        sl = slice(h * HEAD_PAD, (h + 1) * HEAD_PAD)
        ah = (a[h] * jax.nn.sigmoid(oa_ref[0, :, sl])).astype(BF16)
        ch = (c[h] * _silu(gc_ref[0, :, sl])).astype(BF16)
        acc = acc + jnp.dot(ah, w_ref[h * HEAD_PAD:(h + 1) * HEAD_PAD, :], preferred_element_type=F32)
        c0 = A_PAD + B_WIDTH + h * HEAD_PAD
        acc = acc + jnp.dot(ch, w_ref[c0:c0 + HEAD_PAD, :], preferred_element_type=F32)
    o_ref[0] = x_ref[0] + mod_ref[0, 2:3, :] * acc


def _mix_out_call(x, mod, ha, hc, oa, gc, s, na, nc, w, tm):
    bsz, t, d = x.shape
    mod_map = (lambda b, i: (b, 0, 0)) if mod.shape[0] > 1 else (lambda b, i: (0, 0, 0))
    row = lambda c: pl.BlockSpec((1, tm, c), lambda b, i: (b, i, 0))
    both = lambda c: pl.BlockSpec((N_DIR, 1, tm, c), lambda b, i: (0, b, i, 0))
    full = lambda shp: pl.BlockSpec(shp, lambda b, i: (0,) * len(shp))
    return pl.pallas_call(
        _mix_out_kernel,
        grid=(bsz, t // tm),
        in_specs=[row(d), pl.BlockSpec((1, 6, d), mod_map), both(A_PAD), both(CV_W),
                  row(A_PAD), row(CV_W), row(B_WIDTH),
                  full((1, A_PAD)), full((1, CV_W)), full(w.shape)],
        out_specs=row(d),
        out_shape=jax.ShapeDtypeStruct(x.shape, F32),
        compiler_params=_cparams(2),
        name="mix_out",
    )(x, mod, ha, hc, oa, gc, s, na, nc, w)


def _ffn_kernel(nk, final_norm, x_ref, mod_ref, g_ref, wg_ref, wu_ref, wo_ref, fg_ref, o_ref, h_sc, acc_sc):
    kk = pl.program_id(2)

    @pl.when(kk == 0)
    def _():
        x = x_ref[0]
        y = x * _rms_scale(x, x.shape[-1]) * g_ref[...]
        h_sc[...] = (y * (1.0 + mod_ref[0, 4:5, :]) + mod_ref[0, 3:4, :]).astype(BF16)
        acc_sc[...] = jnp.zeros_like(acc_sc)

    h = h_sc[...]
    gate = jnp.dot(h, wg_ref[...], preferred_element_type=F32)
    up = jnp.dot(h, wu_ref[...], preferred_element_type=F32)
    act = (_silu(gate) * up).astype(BF16)
    acc_sc[...] += jnp.dot(act, wo_ref[...], preferred_element_type=F32)

    @pl.when(kk == nk - 1)
    def _():
        y = x_ref[0] + mod_ref[0, 5:6, :] * acc_sc[...]
        if final_norm:
            y = y * _rms_scale(y, y.shape[-1]) * fg_ref[...]
        o_ref[0] = y


def _ffn_call(x, mod, g, w_in, w_out, final_g, final_norm, tm, nk):
    bsz, t, d = x.shape
    d_ff = w_out.shape[0]
    tk = d_ff // nk
    mod_map = (lambda b, i, k: (b, 0, 0)) if mod.shape[0] > 1 else (lambda b, i, k: (0, 0, 0))
    return pl.pallas_call(
        functools.partial(_ffn_kernel, nk, final_norm),
        grid=(bsz, t // tm, nk),
        in_specs=[pl.BlockSpec((1, tm, d), lambda b, i, k: (b, i, 0)),
                  pl.BlockSpec((1, 6, d), mod_map),
                  pl.BlockSpec((1, d), lambda b, i, k: (0, 0)),
                  pl.BlockSpec((d, tk), lambda b, i, k: (0, k)),
                  pl.BlockSpec((d, tk), lambda b, i, k: (0, nk + k)),
                  pl.BlockSpec((tk, d), lambda b, i, k: (k, 0)),
                  pl.BlockSpec((1, d), lambda b, i, k: (0, 0))],
        out_specs=pl.BlockSpec((1, tm, d), lambda b, i, k: (b, i, 0)),
        out_shape=jax.ShapeDtypeStruct(x.shape, F32),
        scratch_shapes=[pltpu.VMEM((tm, d), BF16), pltpu.VMEM((tm, d), F32)],
        compiler_params=_cparams(3),
        name="ffn",
    )(x, mod, g, w_in, w_in, w_out, final_g)


def _pad_heads(w, n_heads, dim, pad, axis=-1):
    w = jnp.moveaxis(w, axis, -1)
    lead = w.shape[:-1]
    w = w.reshape(lead + (n_heads, dim))
    w = jnp.pad(w, [(0, 0)] * len(lead) + [(0, 0), (0, pad - dim)])
    return jnp.moveaxis(w.reshape(lead + (n_heads * pad,)), -1, axis)


def _pad_to(w, width):
    return jnp.pad(w, [(0, 0)] * (w.ndim - 1) + [(0, width - w.shape[-1])])


def _prep_w_in(w_in):
    a_w = A_HEADS * A_HEAD_DIM
    ck = C_HEADS * C_HEAD_DK
    cv = C_HEADS * C_HEAD_DV
    sizes = (a_w, a_w, a_w, a_w, N_DIR * 2 * A_HEADS, B_WIDTH, B_WIDTH, ck, ck, cv, cv, N_DIR * C_GATE_RANK)
    pts = [sum(sizes[:i + 1]) for i in range(len(sizes) - 1)]
    qa, ka, va, oa, ga, ub, vb, qc, kc, vc, gc, gkc = jnp.split(w_in, pts, axis=-1)
    pa = lambda w: _pad_heads(w, A_HEADS, A_HEAD_DIM, HEAD_PAD)
    pk = lambda w: _pad_heads(w, C_HEADS, C_HEAD_DK, CK_PAD)
    pv = lambda w: _pad_heads(w, C_HEADS, C_HEAD_DV, HEAD_PAD)
    ng = 2 * A_HEADS
    cols = [pa(qa), pa(ka), pa(va), pa(oa),
            _pad_to(ga[:, :ng], LANES), _pad_to(ga[:, ng:], LANES),
            ub, vb, pk(qc), pk(kc), pv(vc), pv(gc),
            _pad_to(gkc[:, :C_GATE_RANK], LANES), _pad_to(gkc[:, C_GATE_RANK:], LANES)]
    return jnp.concatenate(cols, axis=-1).astype(BF16)


def _prep_w_out(w_out):
    a_w = A_HEADS * A_HEAD_DIM
    wa, wb, wc = w_out[:a_w], w_out[a_w:a_w + B_WIDTH], w_out[a_w + B_WIDTH:]
    wa = _pad_heads(wa, A_HEADS, A_HEAD_DIM, HEAD_PAD, axis=0)
    wc = _pad_heads(wc, C_HEADS, C_HEAD_DV, HEAD_PAD, axis=0)
    return jnp.concatenate([wa, wb, wc], axis=0).astype(BF16)


def _head_expand():
    r = jnp.arange(CK_W)[:, None] // CK_PAD
    c = jnp.arange(CV_W)[None, :] // HEAD_PAD
    return (r == c).astype(BF16)


def _mixer_scans(st, lw, a_state, c_state, sv):
    ha, cfin, mfin = _mlstm_call(st, lw["conv"], lw["gate_b"], a_state[0], a_state[1], sv)
    hc, sfin = _gla_call(st, lw["w_gk2"], lw["b_gk"], lw["hexp"], c_state, sv)
    s = _sgu_call(st, lw["sgu_g"], lw["sgu_w"], lw["sgu_bias"], sv)
    return ha, hc, s, (cfin, mfin), sfin


def kernel(x, c, ctx, c_ctx, norm1_g, norm2_g, w_ada, b_ada, w_in, mlstm_conv, mlstm_gate_b,
           mlstm_norm_g, gla_w_gk2, gla_b_gk, gla_norm_g, sgu_norm_g, sgu_w, sgu_b, w_out,
           w_ffn_in, w_ffn_out, final_g):
    bsz, seq, d = x.shape
    ctx_len = ctx.shape[1]
    depth = w_in.shape[0]
    n_cond = 2 * SUBLANES
    cond = jnp.zeros((n_cond, d), F32).at[:bsz].set(c).at[bsz].set(c_ctx)
    mods = _ada_call(cond, w_ada, b_ada).reshape(depth, n_cond, 6, d)

    hexp = _head_expand()
    fg = final_g.reshape(1, d)
    x_lat, x_ctx = x, ctx
    for l in range(depth):
        need_ctx = l < depth - 1
        mod_lat = mods[l, :bsz]
        mod_ctx = mods[l, bsz:bsz + 1]
        gb = mlstm_gate_b[l].reshape(N_DIR, 1, 2 * A_HEADS)
        lw = {
            "conv": jnp.stack([_pad_heads(mlstm_conv[l][:, :A_HEADS * A_HEAD_DIM], A_HEADS, A_HEAD_DIM, HEAD_PAD),
                               _pad_heads(mlstm_conv[l][:, A_HEADS * A_HEAD_DIM:], A_HEADS, A_HEAD_DIM, HEAD_PAD)]),
            "gate_b": _pad_to(gb, LANES),
            "w_gk2": jnp.pad(_pad_heads(gla_w_gk2[l], C_HEADS, C_HEAD_DK, CK_PAD),
                             ((0, 0), (0, LANES - C_GATE_RANK), (0, 0))),
            "b_gk": _pad_heads(gla_b_gk[l], C_HEADS, C_HEAD_DK, CK_PAD).reshape(N_DIR, 1, CK_W),
            "hexp": hexp,
            "sgu_g": sgu_norm_g[l].reshape(1, B_WIDTH),
            "sgu_w": sgu_w[l].astype(BF16),
            "sgu_bias": jnp.repeat(sgu_b[l].T, B_GROUP_DIM, axis=1),
        }
        w_in_l = _prep_w_in(w_in[l])
        w_out_l = _prep_w_out(w_out[l])
        na = _pad_heads(mlstm_norm_g[l], A_HEADS, A_HEAD_DIM, HEAD_PAD).reshape(1, A_PAD)
        nc = _pad_heads(jnp.tile(gla_norm_g[l], C_HEADS), C_HEADS, C_HEAD_DV, HEAD_PAD).reshape(1, CV_W)
        g1 = norm1_g[l].reshape(1, d)
        g2 = norm2_g[l].reshape(1, d)
        w_ffn_in_l = w_ffn_in[l].astype(BF16)
        w_ffn_out_l = w_ffn_out[l].astype(BF16)

        st_ctx = _proj_in_call(x_ctx, mod_ctx, g1, w_in_l, tm=256)
        st_lat = _proj_in_call(x_lat, mod_lat, g1, w_in_l, tm=256)

        a0 = (jnp.zeros((bsz, N_DIR, A_HEADS, HEAD_PAD, HEAD_PAD), F32),
              jnp.zeros((bsz, N_DIR, SUBLANES, LANES), F32))
        c0 = jnp.zeros((bsz, N_DIR, CV_W, CK_W), F32)
        sv_ctx = _SeqView(ctx_len, False)
        sv_lat = _SeqView(seq, l % 2 == 1)
        ha_c, hc_c, s_c, a_state, c_state = _mixer_scans(st_ctx, lw, a0, c0, sv_ctx)
        ha_l, hc_l, s_l, _, _ = _mixer_scans(st_lat, lw, a_state, c_state, sv_lat)

        x_lat = _mix_out_call(x_lat, mod_lat, ha_l, hc_l, st_lat["oa"], st_lat["gc"], s_l, na, nc, w_out_l, tm=256)
        x_lat = _ffn_call(x_lat, mod_lat, g2, w_ffn_in_l, w_ffn_out_l, fg, not need_ctx, tm=512, nk=2)
        if need_ctx:
            x_ctx = _mix_out_call(x_ctx, mod_ctx, ha_c, hc_c, st_ctx["oa"], st_ctx["gc"], s_c, na, nc, w_out_l, tm=256)
            x_ctx = _ffn_call(x_ctx, mod_ctx, g2, w_ffn_in_l, w_ffn_out_l, fg, False, tm=256, nk=2)
    return x_lat
```

```python
import functools
import math

import jax
import jax.numpy as jnp
from jax import lax
from jax.experimental import pallas as pl
from jax.experimental.pallas import tpu as pltpu

F32 = jnp.float32
BF16 = jnp.bfloat16
HIGHEST = lax.Precision.HIGHEST

LANES = 128
SUBLANES = 8
VMEM_LIMIT_BYTES = 56 * 1024 * 1024

GRID_W = 64
RMS_EPS = 1e-6
A_HEADS = 4
A_HEAD_DIM = 96
B_GROUPS = 4
B_GROUP_DIM = 64
B_WIDTH = B_GROUPS * B_GROUP_DIM
C_HEADS = 4
C_HEAD_DK = 48
C_HEAD_DV = 96
C_GATE_RANK = 16
C_GATE_NORM = 16.0
N_DIR = 2

HEAD_PAD = LANES
A_PAD = A_HEADS * HEAD_PAD
CK_PAD = 64
CK_W = C_HEADS * CK_PAD
CV_W = C_HEADS * HEAD_PAD
ONES_LANE = A_HEAD_DIM

SEQ_BLOCK = 128
HALO = SUBLANES
GK_COPIES = 3

_IN_LAYOUT = (("qa", A_PAD, 1), ("ka", A_PAD, 1), ("va", A_PAD, 1), ("oa", A_PAD, 1),
              ("ga", LANES, N_DIR),
              ("ub", B_WIDTH, 1), ("vb", B_WIDTH, 1),
              ("qc", CK_W, 1), ("kc", CK_W, 1), ("vc", CV_W, 1), ("gc", CV_W, 1),
              ("gk", LANES, N_DIR))
N_IN_PAD = sum(w * n for _, w, n in _IN_LAYOUT)


def _cparams(n_axes):
    return pltpu.CompilerParams(dimension_semantics=("arbitrary",) * n_axes,
                                vmem_limit_bytes=VMEM_LIMIT_BYTES)


def _log_sigmoid(x):
    return jnp.minimum(x, 0.0) - jnp.log(1.0 + jnp.exp(-jnp.abs(x)))


def _silu(x):
    return x * jax.nn.sigmoid(x)


def _gelu_tanh(x):
    c = math.sqrt(2.0 / math.pi)
    return 0.5 * x * (1.0 + jnp.tanh(c * (x + 0.044715 * (x * x * x))))


def _rms_scale(x, n):
    return lax.rsqrt(jnp.sum(x * x, axis=-1, keepdims=True) * (1.0 / n) + RMS_EPS)


def _split_hi_lo(x):
    hi = x.astype(BF16)
    lo = (x - hi.astype(F32)).astype(BF16)
    return hi, lo


def _ada_kernel(s_ref, w_ref, b_ref, o_ref):
    s = _silu(s_ref[...])
    o_ref[0] = jnp.dot(s, w_ref[0], precision=HIGHEST, preferred_element_type=F32) + b_ref[0]


def _ada_call(cond, w_ada, b_ada):
    depth, d, n6 = w_ada.shape
    nb = cond.shape[0]
    tn = 1536
    return pl.pallas_call(
        _ada_kernel,
        grid=(depth, n6 // tn),
        in_specs=[pl.BlockSpec((nb, d), lambda l, j: (0, 0)),
                  pl.BlockSpec((1, d, tn), lambda l, j: (l, 0, j)),
                  pl.BlockSpec((1, 1, tn), lambda l, j: (l, 0, j))],
        out_specs=pl.BlockSpec((1, nb, tn), lambda l, j: (l, 0, j)),
        out_shape=jax.ShapeDtypeStruct((depth, nb, n6), F32),
        compiler_params=_cparams(2),
        name="ada_mod",
    )(cond, w_ada, b_ada.reshape(depth, 1, n6))


def _proj_in_kernel(x_ref, mod_ref, g_ref, w_ref, *out_refs):
    x = x_ref[0]
    y = x * _rms_scale(x, x.shape[-1]) * g_ref[...]
    h = (y * (1.0 + mod_ref[0, 1:2, :]) + mod_ref[0, 0:1, :]).astype(BF16)
    off = 0
    for (_, width, n_dir), o_ref in zip(_IN_LAYOUT, out_refs):
        for di in range(n_dir):
            res = jnp.dot(h, w_ref[:, off:off + width], preferred_element_type=F32)
            if n_dir == 1:
                o_ref[0] = res
            else:
                o_ref[di, 0] = res
            off += width


def _proj_in_call(x, mod, g, w, tm):
    bsz, t, d = x.shape
    mod_b = mod.shape[0]
    mod_map = (lambda b, i: (b, 0, 0)) if mod_b > 1 else (lambda b, i: (0, 0, 0))
    out_shapes, out_specs = [], []
    for _, wd, n_dir in _IN_LAYOUT:
        if n_dir == 1:
            out_shapes.append(jax.ShapeDtypeStruct((bsz, t, wd), F32))
            out_specs.append(pl.BlockSpec((1, tm, wd), lambda b, i: (b, i, 0)))
        else:
            out_shapes.append(jax.ShapeDtypeStruct((n_dir, bsz, t, wd), F32))
            out_specs.append(pl.BlockSpec((n_dir, 1, tm, wd), lambda b, i: (0, b, i, 0)))
    outs = pl.pallas_call(
        _proj_in_kernel,
        grid=(bsz, t // tm),
        in_specs=[pl.BlockSpec((1, tm, d), lambda b, i: (b, i, 0)),
                  pl.BlockSpec((1, 6, d), mod_map),
                  pl.BlockSpec((1, d), lambda b, i: (0, 0)),
                  pl.BlockSpec((d, N_IN_PAD), lambda b, i: (0, 0))],
        out_specs=out_specs,
        out_shape=out_shapes,
        compiler_params=_cparams(2),
        name="proj_in",
    )(x, mod, g, w)
    return dict(zip([n for n, _, _ in _IN_LAYOUT], outs))


class _SeqView:
    def __init__(self, t_len, column_major):
        self.column_major = column_major
        self.nblk = t_len // SEQ_BLOCK
        self.t_len = t_len
        if column_major:
            assert t_len // GRID_W == SEQ_BLOCK

    def view(self, a):
        if not self.column_major:
            return a
        c = a.shape[-1]
        return a.reshape(a.shape[:-2] + (SEQ_BLOCK, GRID_W * c))

    def unview(self, a, c):
        if not self.column_major:
            return a
        return a.reshape(a.shape[:-2] + (self.t_len, c))

    def main(self, j):
        return (0, j) if self.column_major else (j, 0)

    def prev_halo(self, j):
        jp = jnp.maximum(j - 1, 0)
        if self.column_major:
            return (SEQ_BLOCK // HALO - 1, jp)
        return (jnp.maximum(j * (SEQ_BLOCK // HALO) - 1, 0), 0)

    def next_halo(self, j):
        jn = jnp.minimum(j + 1, self.nblk - 1)
        if self.column_major:
            return (0, jn)
        return (jn * (SEQ_BLOCK // HALO), 0)


def _dir_block(d, j, nblk):
    return jnp.where(d == 0, j, nblk - 1 - j)


def _mlstm_kernel(nblk, reverse, q_ref, qp_ref, qn_ref, k_ref, kp_ref, kn_ref, v_ref, g_ref,
                  cw_ref, gb_ref, cmask_ref, cinit_ref, minit_ref,
                  h_ref, cfin_ref, mfin_ref, c_sc, m_sc):
    j = pl.program_id(1)
    blk = nblk - 1 - j if reverse else j
    L = SEQ_BLOCK

    @pl.when(j == 0)
    def _():
        c_sc[...] = cinit_ref[0]
        m_sc[...] = minit_ref[0]

    row = lax.broadcasted_iota(jnp.int32, (L, 1), 0)
    has_prev = (blk > 0).astype(F32)
    has_next = (blk < nblk - 1).astype(F32)

    def conv_silu(x_ref, xp_ref, xn_ref, w):
        x = x_ref[0]
        prev_row = xp_ref[0][HALO - 1:HALO, :] * has_prev
        next_row = xn_ref[0][0:1, :] * has_next
        x_dn = jnp.where(row == 0, prev_row, pltpu.roll(x, 1, 0))
        x_up = jnp.where(row == L - 1, next_row, pltpu.roll(x, L - 1, 0))
        return _silu(w[0:1, :] * x_dn + w[1:2, :] * x + w[2:3, :] * x_up)

    q = (conv_silu(q_ref, qp_ref, qn_ref, cw_ref[0]) * (A_HEAD_DIM ** -0.5)).astype(BF16)
    k = conv_silu(k_ref, kp_ref, kn_ref, cw_ref[1]).astype(BF16)
    lane = lax.broadcasted_iota(jnp.int32, (1, A_PAD), 1) % HEAD_PAD
    v = jnp.where(lane == ONES_LANE, 1.0, v_ref[0])

    g = g_ref[0, 0] + gb_ref[0]
    logf = _log_sigmoid(g)
    lf_hi, lf_lo = _split_hi_lo(logf)
    cs = jnp.dot(cmask_ref[...], jnp.concatenate([lf_hi, lf_lo], axis=1), preferred_element_type=F32)
    bc = cs[:, 0:LANES] + cs[:, LANES:2 * LANES]
    ri = lax.broadcasted_iota(jnp.int32, (L, L), 0)
    ci = lax.broadcasted_iota(jnp.int32, (L, L), 1)
    past = (ci >= ri) if reverse else (ci <= ri)
    g_t = g.T
    bc_t = bc.T
    last = 0 if reverse else L - 1

    out_lane = lax.broadcasted_iota(jnp.int32, (1, HEAD_PAD), 1)
    for h in range(A_HEADS):
        sl = slice(h * HEAD_PAD, (h + 1) * HEAD_PAD)
        fcol = A_HEADS + h
        qh, kh, vh = q[:, sl], k[:, sl], v[:, sl]
        b_col = bc[:, fcol:fcol + 1]
        b_row = bc_t[fcol:fcol + 1, :]
        i_col = g[:, h:h + 1]
        i_row = g_t[h:h + 1, :]
        m_prev = m_sc[h:h + 1, 0:1]
        b_last = bc[last:last + 1, fcol:fcol + 1]
        ct = c_sc[h]

        d_log = jnp.where(past, b_col - b_row + i_row, -jnp.inf)
        inter_log = b_col + m_prev
        m_t = jnp.maximum(inter_log, jnp.max(d_log, axis=1, keepdims=True))
        qk = lax.dot_general(qh, kh, (((1,), (1,)), ((), ())), preferred_element_type=F32)
        s = qk * jnp.exp(d_log - m_t)
        inter = jnp.exp(inter_log - m_t)
        num = jnp.dot(s.astype(BF16), vh.astype(BF16), preferred_element_type=F32) \
            + inter * jnp.dot(qh, ct.astype(BF16), preferred_element_type=F32)
        den = num[:, ONES_LANE:ONES_LANE + 1]
        inv = 1.0 / jnp.maximum(jnp.abs(den), jnp.exp(-m_t))
        h_ref[0, :, sl] = jnp.where(out_lane < A_HEAD_DIM, num * inv, 0.0)

        w_log = b_last - b_col + i_col
        m_new = jnp.maximum(b_last + m_prev, jnp.max(w_log, axis=0, keepdims=True))
        decay = jnp.exp(b_last + m_prev - m_new)
        wv = (jnp.exp(w_log - m_new) * vh).astype(BF16)
        upd = lax.dot_general(kh, wv, (((0,), (0,)), ((), ())), preferred_element_type=F32)
        c_sc[h] = decay * ct + upd
        m_sc[h:h + 1, :] = jnp.broadcast_to(m_new, (1, LANES))

    @pl.when(j == nblk - 1)
    def _():
        cfin_ref[0] = c_sc[...]
        mfin_ref[0] = m_sc[...]


def _mlstm_call(st, conv_w, gate_b, cinit, minit, sv, reverse):
    bsz = st["qa"].shape[0]
    nblk = sv.nblk
    di = 1 if reverse else 0

    def blk_of(j):
        return nblk - 1 - j if reverse else j

    def main_map(b, j):
        return (b,) + sv.main(blk_of(j))

    def prev_map(b, j):
        return (b,) + sv.prev_halo(blk_of(j))

    def next_map(b, j):
        return (b,) + sv.next_halo(blk_of(j))

    def dir_map(b, j):
        return (di, b) + sv.main(blk_of(j))

    blk = lambda c: pl.BlockSpec((1, SEQ_BLOCK, c), main_map)
    halo_p = pl.BlockSpec((1, HALO, A_PAD), prev_map)
    halo_n = pl.BlockSpec((1, HALO, A_PAD), next_map)
    c_spec = pl.BlockSpec((1, A_HEADS, HEAD_PAD, HEAD_PAD), lambda b, j: (b, 0, 0, 0))
    m_spec = pl.BlockSpec((1, SUBLANES, LANES), lambda b, j: (b, 0, 0))
    qa, ka, va = sv.view(st["qa"]), sv.view(st["ka"]), sv.view(st["va"])
    ga = sv.view(st["ga"])
    cmask, _ = _gla_masks(reverse)
    h, cfin, mfin = pl.pallas_call(
        functools.partial(_mlstm_kernel, nblk, reverse),
        grid=(bsz, nblk),
        in_specs=[blk(A_PAD), halo_p, halo_n, blk(A_PAD), halo_p, halo_n, blk(A_PAD),
                  pl.BlockSpec((1, 1, SEQ_BLOCK, LANES), dir_map),
                  pl.BlockSpec((2, 3, A_PAD), lambda b, j: (0, 0, 0)),
                  pl.BlockSpec((1, 1, LANES), lambda b, j: (di, 0, 0)),
                  pl.BlockSpec(cmask.shape, lambda b, j: (0, 0)),
                  c_spec, m_spec],
        out_specs=[blk(A_PAD), c_spec, m_spec],
        out_shape=[jax.ShapeDtypeStruct(qa.shape, F32),
                   jax.ShapeDtypeStruct(cinit.shape, F32),
                   jax.ShapeDtypeStruct(minit.shape, F32)],
        scratch_shapes=[pltpu.VMEM((A_HEADS, HEAD_PAD, HEAD_PAD), F32),
                        pltpu.VMEM((SUBLANES, LANES), F32)],
        compiler_params=_cparams(2),
        name="mlstm_scan_bwd" if reverse else "mlstm_scan_fwd",
    )(qa, qa, qa, ka, ka, ka, va, ga, conv_w, gate_b, cmask, cinit, minit)
    return sv.unview(h, A_PAD), cfin, mfin


def _gla_kernel(nblk, reverse, q_ref, k_ref, v_ref, gk_ref, w2_ref, bgk_ref, cmask_ref, lmask_ref, sinit_ref,
                o_ref, sfin_ref, s_sc, sbd_sc):
    j = pl.program_id(1)
    L = SEQ_BLOCK

    @pl.when(j == 0)
    def _():
        s_sc[...] = sinit_ref[0]
        sbd_sc[...] = jnp.zeros_like(sbd_sc)
        for h in range(C_HEADS):
            sbd_sc[h * CK_PAD:(h + 1) * CK_PAD, h * HEAD_PAD:(h + 1) * HEAD_PAD] = sinit_ref[0, h].astype(BF16)

    g = gk_ref[0, 0]
    g_hi = g.astype(BF16).astype(F32)
    glane = lax.broadcasted_iota(jnp.int32, (1, LANES), 1)
    g_split = jnp.where(glane < 2 * C_GATE_RANK, g_hi, g - g_hi).astype(BF16)
    gk = jnp.dot(g_split, w2_ref[0], preferred_element_type=F32) + bgk_ref[0]
    la = _log_sigmoid(gk) * (1.0 / C_GATE_NORM)
    la_hi, la_lo = _split_hi_lo(la)
    cs = jnp.dot(cmask_ref[...], jnp.concatenate([la_hi, la_lo], axis=1), preferred_element_type=F32)
    bs = cs[:, 0:CK_W] + cs[:, CK_W:2 * CK_W]
    last = 0 if reverse else L - 1
    btot = bs[last:last + 1]

    qs = q_ref[0] * (C_HEAD_DK ** -0.5)
    kk = k_ref[0]
    vb = v_ref[0].astype(BF16)
    row = lax.broadcasted_iota(jnp.int32, (L, 1), 0)
    hlane = lax.broadcasted_iota(jnp.int32, (1, LANES), 1) // CK_PAD
    col = lambda x, h: x[:, (h // 2) * LANES:(h // 2 + 1) * LANES]
    k_head = [jnp.where(hlane == h % 2, col(kk, h), 0.0) for h in range(C_HEADS)]

    def rows_from(x, idx_of_row_block, rows_per_block):
        n = L // rows_per_block
        return jnp.concatenate([jnp.broadcast_to(x[idx_of_row_block(i):idx_of_row_block(i) + 1],
                                                 (rows_per_block, x.shape[1])) for i in range(n)], axis=0)

    att = [None] * C_HEADS

    def add_level(q_t, k_mul, level, k_rows=None):
        q_b = q_t.astype(BF16)
        for h in range(C_HEADS):
            k_t = k_head[h] if k_mul is None else k_head[h] * col(k_mul, h)
            if k_rows is not None:
                k_t = jnp.where(k_rows, k_t, 0.0)
            a = lax.dot_general(col(q_b, h), k_t.astype(BF16), (((1,), (1,)), ((), ())),
                                preferred_element_type=F32)
            if level is not None:
                a = a * lmask_ref[level]
            att[h] = a if att[h] is None else att[h] + a

    level = 0
    m = L // 2
    while m >= 1:
        pos = row % (2 * m)
        is_q = (pos < m) if reverse else (pos >= m)
        k_rows = None
        if m == 1:
            q_t = jnp.where(is_q, qs * jnp.exp(la), 0.0)
            k_mul = None
            k_rows = jnp.logical_not(is_q)
        else:
            bnd = m if reverse else m - 1
            if 2 * m >= SUBLANES:
                pref = rows_from(bs, lambda i: i * 2 * m + bnd, 2 * m)
            else:
                p0 = rows_from(bs, lambda i: i * SUBLANES + bnd, SUBLANES)
                p1 = rows_from(bs, lambda i: i * SUBLANES + 2 * m + bnd, SUBLANES)
                pref = jnp.where(row % SUBLANES < 2 * m, p0, p1)
            q_t = qs * jnp.exp(jnp.where(is_q, bs - pref, -jnp.inf))
            k_mul = jnp.exp(jnp.where(is_q, -jnp.inf, pref - bs))
        add_level(q_t, k_mul, None if m == L // 2 else level, k_rows)
        level += 1
        m //= 2
    add_level(qs, None, level)

    o_inter = jnp.dot((qs * jnp.exp(bs)).astype(BF16), sbd_sc[...], preferred_element_type=F32)
    for h in range(C_HEADS):
        cols = slice(h * HEAD_PAD, (h + 1) * HEAD_PAD)
        o_ref[0, :, cols] = o_inter[:, cols] + jnp.dot(att[h].astype(BF16), vb[:, cols],
                                                       preferred_element_type=F32)

    ke_t = (kk * jnp.exp(btot - bs)).T.astype(BF16)
    ones = jnp.ones((L, HEAD_PAD), BF16)
    tdot = lambda a: lax.dot_general(a, ones, (((0,), (0,)), ((), ())), preferred_element_type=F32)
    dec_col = jnp.exp(tdot(la_hi) + tdot(la_lo))
    for h in range(C_HEADS):
        rows = slice(h * CK_PAD, (h + 1) * CK_PAD)
        cols = slice(h * HEAD_PAD, (h + 1) * HEAD_PAD)
        upd = jnp.dot(ke_t[rows], vb[:, cols], preferred_element_type=F32)
        s_new = dec_col[rows] * s_sc[h] + upd
        s_sc[h] = s_new
        sbd_sc[rows, cols] = s_new.astype(BF16)

    @pl.when(j == nblk - 1)
    def _():
        sfin_ref[0] = s_sc[...]


def _gla_masks(reverse):
    t = jnp.arange(SEQ_BLOCK)[:, None]
    u = jnp.arange(SEQ_BLOCK)[None, :]
    cmask = ((u >= t) if reverse else (u <= t)).astype(BF16)
    sizes = []
    m = SEQ_BLOCK
    while m >= 1:
        sizes.append(m)
        m //= 2
    lmask = jnp.stack([(t // sz) == (u // sz) for sz in sizes]).astype(F32)
    return cmask, lmask


def _gla_call(st, w2s, b_gk, sinit, sv, reverse):
    bsz = st["qc"].shape[0]
    nblk = sv.nblk
    di = 1 if reverse else 0

    def blk_of(j):
        return nblk - 1 - j if reverse else j

    def main_map(b, j):
        return (b,) + sv.main(blk_of(j))

    def dir_map(b, j):
        return (di, b) + sv.main(blk_of(j))

    blk = lambda c: pl.BlockSpec((1, SEQ_BLOCK, c), main_map)
    full = lambda shp: pl.BlockSpec(shp, lambda b, j: (0,) * len(shp))
    state_spec = pl.BlockSpec((1, C_HEADS, CK_PAD, HEAD_PAD), lambda b, j: (b, 0, 0, 0))
    qc, kc, vc, gk = sv.view(st["qc"]), sv.view(st["kc"]), sv.view(st["vc"]), sv.view(st["gk"])
    cmask, lmask = _gla_masks(reverse)
    o, sfin = pl.pallas_call(
        functools.partial(_gla_kernel, nblk, reverse),
        grid=(bsz, nblk),
        in_specs=[blk(CK_W), blk(CK_W), blk(CV_W),
                  pl.BlockSpec((1, 1, SEQ_BLOCK, LANES), dir_map),
                  pl.BlockSpec((1, LANES, CK_W), lambda b, j: (di, 0, 0)),
                  pl.BlockSpec((1, 1, CK_W), lambda b, j: (di, 0, 0)),
                  full(cmask.shape), full(lmask.shape), state_spec],
        out_specs=[blk(CV_W), state_spec],
        out_shape=[jax.ShapeDtypeStruct(vc.shape, F32),
                   jax.ShapeDtypeStruct(sinit.shape, F32)],
        scratch_shapes=[pltpu.VMEM((C_HEADS, CK_PAD, HEAD_PAD), F32),
                        pltpu.VMEM((CK_W, CV_W), BF16)],
        compiler_params=_cparams(2),
        name="gla_scan_bwd" if reverse else "gla_scan_fwd",
    )(qc, kc, vc, gk, w2s, b_gk, cmask, lmask, sinit)
    return sv.unview(o, CV_W), sfin


def _sgu_kernel(u_ref, v_ref, g_ref, w_ref, bias_ref, o_ref):
    u = _gelu_tanh(u_ref[0])
    v = _gelu_tanh(v_ref[0])
    v = v * _rms_scale(v, B_WIDTH) * g_ref[...]
    grp = lax.broadcasted_iota(jnp.int32, (1, B_WIDTH), 1) // B_GROUP_DIM
    mixed = bias_ref[...]
    for gi in range(B_GROUPS):
        vg = jnp.where(grp == gi, v, 0.0).astype(BF16)
        mixed = mixed + jnp.dot(w_ref[gi], vg, preferred_element_type=F32)
    o_ref[0] = u * mixed


def _sgu_call(st, norm_g, w_s, bias, sv):
    bsz = st["ub"].shape[0]

    def main_map(b, j):
        return (b,) + sv.main(j)

    blk = pl.BlockSpec((1, SEQ_BLOCK, B_WIDTH), main_map)
    ub, vb = sv.view(st["ub"]), sv.view(st["vb"])
    o = pl.pallas_call(
        _sgu_kernel,
        grid=(bsz, sv.nblk),
        in_specs=[blk, blk,
                  pl.BlockSpec((1, B_WIDTH), lambda b, j: (0, 0)),
                  pl.BlockSpec((B_GROUPS, SEQ_BLOCK, SEQ_BLOCK), lambda b, j: (0, 0, 0)),
                  pl.BlockSpec((SEQ_BLOCK, B_WIDTH), lambda b, j: (0, 0))],
        out_specs=blk,
        out_shape=jax.ShapeDtypeStruct(ub.shape, F32),
        compiler_params=_cparams(2),
        name="sgu",
    )(ub, vb, norm_g, w_s, bias)
    return sv.unview(o, B_WIDTH)


def _mix_out_kernel(x_ref, mod_ref, haf_ref, hab_ref, hcf_ref, hcb_ref, oa_ref, gc_ref, s_ref, na_ref, nc_ref,
                    w_ref, o_ref):
    def head_norm(hsum, gain):
        outs = []
        for h in range(A_HEADS):
            sl = slice(h * HEAD_PAD, (h + 1) * HEAD_PAD)
            xh = hsum[:, sl]
            outs.append(xh * _rms_scale(xh, A_HEAD_DIM) * gain[:, sl])
        return outs

    a = head_norm(haf_ref[0] + hab_ref[0], na_ref[...])
    c = head_norm(hcf_ref[0] + hcb_ref[0], nc_ref[...])
    acc = jnp.dot(s_ref[0].astype(BF16), w_ref[A_PAD:A_PAD + B_WIDTH, :], preferred_element_type=F32)
    for h in range(A_HEADS):
        sl = slice(h * HEAD_PAD, (h + 1) * HEAD_PAD)
        ah = (a[h] * jax.nn.sigmoid(oa_ref[0, :, sl])).astype(BF16)
        ch = (c[h] * _silu(gc_ref[0, :, sl])).astype(BF16)
        acc = acc + jnp.dot(ah, w_ref[h * HEAD_PAD:(h + 1) * HEAD_PAD, :], preferred_element_type=F32)
        c0 = A_PAD + B_WIDTH + h * HEAD_PAD
        acc = acc + jnp.dot(ch, w_ref[c0:c0 + HEAD_PAD, :], preferred_element_type=F32)
    o_ref[0] = x_ref[0] + mod_ref[0, 2:3, :] * acc


def _mix_out_call(x, mod, haf, hab, hcf, hcb, oa, gc, s, na, nc, w, tm):
    bsz, t, d = x.shape
    mod_map = (lambda b, i: (b, 0, 0)) if mod.shape[0] > 1 else (lambda b, i: (0, 0, 0))
    row = lambda c: pl.BlockSpec((1, tm, c), lambda b, i: (b, i, 0))
    full = lambda shp: pl.BlockSpec(shp, lambda b, i: (0,) * len(shp))
    return pl.pallas_call(
        _mix_out_kernel,
        grid=(bsz, t // tm),
        in_specs=[row(d), pl.BlockSpec((1, 6, d), mod_map), row(A_PAD), row(A_PAD), row(CV_W), row(CV_W),
                  row(A_PAD), row(CV_W), row(B_WIDTH),
                  full((1, A_PAD)), full((1, CV_W)), full(w.shape)],
        out_specs=row(d),
        out_shape=jax.ShapeDtypeStruct(x.shape, F32),
        compiler_params=_cparams(2),
        name="mix_out",
    )(x, mod, haf, hab, hcf, hcb, oa, gc, s, na, nc, w)


def _ffn_kernel(nk, final_norm, x_ref, mod_ref, g_ref, wg_ref, wu_ref, wo_ref, fg_ref, o_ref, h_sc, acc_sc):
    kk = pl.program_id(2)

    @pl.when(kk == 0)
    def _():
        x = x_ref[0]
        y = x * _rms_scale(x, x.shape[-1]) * g_ref[...]
        h_sc[...] = (y * (1.0 + mod_ref[0, 4:5, :]) + mod_ref[0, 3:4, :]).astype(BF16)
        acc_sc[...] = jnp.zeros_like(acc_sc)

    h = h_sc[...]
    gate = jnp.dot(h, wg_ref[...], preferred_element_type=F32)
    up = jnp.dot(h, wu_ref[...], preferred_element_type=F32)
    act = (_silu(gate) * up).astype(BF16)
    acc_sc[...] += jnp.dot(act, wo_ref[...], preferred_element_type=F32)

    @pl.when(kk == nk - 1)
    def _():
        y = x_ref[0] + mod_ref[0, 5:6, :] * acc_sc[...]
        if final_norm:
            y = y * _rms_scale(y, y.shape[-1]) * fg_ref[...]
        o_ref[0] = y


def _ffn_call(x, mod, g, w_in, w_out, final_g, final_norm, tm, nk):
    bsz, t, d = x.shape
    d_ff = w_out.shape[0]
    tk = d_ff // nk
    mod_map = (lambda b, i, k: (b, 0, 0)) if mod.shape[0] > 1 else (lambda b, i, k: (0, 0, 0))
    return pl.pallas_call(
        functools.partial(_ffn_kernel, nk, final_norm),
        grid=(bsz, t // tm, nk),
        in_specs=[pl.BlockSpec((1, tm, d), lambda b, i, k: (b, i, 0)),
                  pl.BlockSpec((1, 6, d), mod_map),
                  pl.BlockSpec((1, d), lambda b, i, k: (0, 0)),
                  pl.BlockSpec((d, tk), lambda b, i, k: (0, k)),
                  pl.BlockSpec((d, tk), lambda b, i, k: (0, nk + k)),
                  pl.BlockSpec((tk, d), lambda b, i, k: (k, 0)),
                  pl.BlockSpec((1, d), lambda b, i, k: (0, 0))],
        out_specs=pl.BlockSpec((1, tm, d), lambda b, i, k: (b, i, 0)),
        out_shape=jax.ShapeDtypeStruct(x.shape, F32),
        scratch_shapes=[pltpu.VMEM((tm, d), BF16), pltpu.VMEM((tm, d), F32)],
        compiler_params=_cparams(3),
        name="ffn",
    )(x, mod, g, w_in, w_in, w_out, final_g)


def _pad_heads(w, n_heads, dim, pad, axis=-1):
    w = jnp.moveaxis(w, axis, -1)
    lead = w.shape[:-1]
    w = w.reshape(lead + (n_heads, dim))
    w = jnp.pad(w, [(0, 0)] * len(lead) + [(0, 0), (0, pad - dim)])
    return jnp.moveaxis(w.reshape(lead + (n_heads * pad,)), -1, axis)


def _pad_to(w, width):
    return jnp.pad(w, [(0, 0)] * (w.ndim - 1) + [(0, width - w.shape[-1])])


def _prep_w_in(w_in):
    a_w = A_HEADS * A_HEAD_DIM
    ck = C_HEADS * C_HEAD_DK
    cv = C_HEADS * C_HEAD_DV
    sizes = (a_w, a_w, a_w, a_w, N_DIR * 2 * A_HEADS, B_WIDTH, B_WIDTH, ck, ck, cv, cv, N_DIR * C_GATE_RANK)
    pts = [sum(sizes[:i + 1]) for i in range(len(sizes) - 1)]
    qa, ka, va, oa, ga, ub, vb, qc, kc, vc, gc, gkc = jnp.split(w_in, pts, axis=-1)
    pa = lambda w: _pad_heads(w, A_HEADS, A_HEAD_DIM, HEAD_PAD)
    pk = lambda w: _pad_heads(w, C_HEADS, C_HEAD_DK, CK_PAD)
    pv = lambda w: _pad_heads(w, C_HEADS, C_HEAD_DV, HEAD_PAD)
    ng = 2 * A_HEADS
    gk_dir = lambda w: _pad_to(jnp.tile(w, (1, GK_COPIES)), LANES)
    cols = [pa(qa), pa(ka), pa(va), pa(oa),
            _pad_to(ga[:, :ng], LANES), _pad_to(ga[:, ng:], LANES),
            ub, vb, pk(qc), pk(kc), pv(vc), pv(gc),
            gk_dir(gkc[:, :C_GATE_RANK]), gk_dir(gkc[:, C_GATE_RANK:])]
    return jnp.concatenate(cols, axis=-1).astype(BF16)


def _prep_w_gk2(w_gk2):
    w = _pad_heads(w_gk2, C_HEADS, C_HEAD_DK, CK_PAD)
    hi = w.astype(BF16)
    lo = (w - hi.astype(F32)).astype(BF16)
    stacked = jnp.concatenate([hi, lo, hi], axis=1)
    return jnp.pad(stacked, ((0, 0), (0, LANES - GK_COPIES * C_GATE_RANK), (0, 0)))


def _prep_w_out(w_out):
    a_w = A_HEADS * A_HEAD_DIM
    wa, wb, wc = w_out[:a_w], w_out[a_w:a_w + B_WIDTH], w_out[a_w + B_WIDTH:]
    wa = _pad_heads(wa, A_HEADS, A_HEAD_DIM, HEAD_PAD, axis=0)
    wc = _pad_heads(wc, C_HEADS, C_HEAD_DV, HEAD_PAD, axis=0)
    return jnp.concatenate([wa, wb, wc], axis=0).astype(BF16)


def _mixer_scans(st, lw, a_state, c_state, sv):
    ha, a_fin, hc, c_fin = [], [], [], []
    for di, reverse in enumerate((False, True)):
        h, cfin, mfin = _mlstm_call(st, lw["conv"], lw["gate_b"], a_state[di][0], a_state[di][1], sv, reverse)
        ha.append(h)
        a_fin.append((cfin, mfin))
        o, sfin = _gla_call(st, lw["w_gk2"], lw["b_gk"], c_state[di], sv, reverse)
        hc.append(o)
        c_fin.append(sfin)
    s = _sgu_call(st, lw["sgu_g"], lw["sgu_w"], lw["sgu_bias"], sv)
    return ha, hc, s, a_fin, c_fin


def kernel(x, c, ctx, c_ctx, norm1_g, norm2_g, w_ada, b_ada, w_in, mlstm_conv, mlstm_gate_b,
           mlstm_norm_g, gla_w_gk2, gla_b_gk, gla_norm_g, sgu_norm_g, sgu_w, sgu_b, w_out,
           w_ffn_in, w_ffn_out, final_g):
    bsz, seq, d = x.shape
    ctx_len = ctx.shape[1]
    depth = w_in.shape[0]
    n_cond = 2 * SUBLANES
    cond = jnp.zeros((n_cond, d), F32).at[:bsz].set(c).at[bsz].set(c_ctx)
    mods = _ada_call(cond, w_ada, b_ada).reshape(depth, n_cond, 6, d)

    fg = final_g.reshape(1, d)
    x_lat, x_ctx = x, ctx
    for l in range(depth):
        need_ctx = l < depth - 1
        mod_lat = mods[l, :bsz]
        mod_ctx = mods[l, bsz:bsz + 1]
        gb = mlstm_gate_b[l].reshape(N_DIR, 1, 2 * A_HEADS)
        lw = {
            "conv": jnp.stack([_pad_heads(mlstm_conv[l][:, :A_HEADS * A_HEAD_DIM], A_HEADS, A_HEAD_DIM, HEAD_PAD),
                               _pad_heads(mlstm_conv[l][:, A_HEADS * A_HEAD_DIM:], A_HEADS, A_HEAD_DIM, HEAD_PAD)]),
            "gate_b": _pad_to(gb, LANES),
            "w_gk2": _prep_w_gk2(gla_w_gk2[l]),
            "b_gk": _pad_heads(gla_b_gk[l], C_HEADS, C_HEAD_DK, CK_PAD).reshape(N_DIR, 1, CK_W),
            "sgu_g": sgu_norm_g[l].reshape(1, B_WIDTH),
            "sgu_w": sgu_w[l].astype(BF16),
            "sgu_bias": jnp.repeat(sgu_b[l].T, B_GROUP_DIM, axis=1),
        }
        w_in_l = _prep_w_in(w_in[l])
        w_out_l = _prep_w_out(w_out[l])
        na = _pad_heads(mlstm_norm_g[l], A_HEADS, A_HEAD_DIM, HEAD_PAD).reshape(1, A_PAD)
        nc = _pad_heads(jnp.tile(gla_norm_g[l], C_HEADS), C_HEADS, C_HEAD_DV, HEAD_PAD).reshape(1, CV_W)
        g1 = norm1_g[l].reshape(1, d)
        g2 = norm2_g[l].reshape(1, d)
        w_ffn_in_l = w_ffn_in[l].astype(BF16)
        w_ffn_out_l = w_ffn_out[l].astype(BF16)

        st_ctx = _proj_in_call(x_ctx, mod_ctx, g1, w_in_l, tm=256)
        st_lat = _proj_in_call(x_lat, mod_lat, g1, w_in_l, tm=256)

        a0 = (jnp.zeros((bsz, A_HEADS, HEAD_PAD, HEAD_PAD), F32), jnp.zeros((bsz, SUBLANES, LANES), F32))
        c0 = jnp.zeros((bsz, C_HEADS, CK_PAD, HEAD_PAD), F32)
        sv_ctx = _SeqView(ctx_len, False)
        sv_lat = _SeqView(seq, l % 2 == 1)
        ha_c, hc_c, s_c, a_state, c_state = _mixer_scans(st_ctx, lw, (a0, a0), (c0, c0), sv_ctx)
        ha_l, hc_l, s_l, _, _ = _mixer_scans(st_lat, lw, a_state, c_state, sv_lat)

        x_lat = _mix_out_call(x_lat, mod_lat, ha_l[0], ha_l[1], hc_l[0], hc_l[1], st_lat["oa"], st_lat["gc"], s_l,
                              na, nc, w_out_l, tm=256)
        x_lat = _ffn_call(x_lat, mod_lat, g2, w_ffn_in_l, w_ffn_out_l, fg, not need_ctx, tm=512, nk=2)
        if need_ctx:
            x_ctx = _mix_out_call(x_ctx, mod_ctx, ha_c[0], ha_c[1], hc_c[0], hc_c[1], st_ctx["oa"], st_ctx["gc"], s_c,
                                  na, nc, w_out_l, tm=256)
            x_ctx = _ffn_call(x_ctx, mod_ctx, g2, w_ffn_in_l, w_ffn_out_l, fg, False, tm=256, nk=2)
    return x_lat
```

```python
import functools
import math

import jax
import jax.numpy as jnp
from jax import lax
from jax.experimental import pallas as pl
from jax.experimental.pallas import tpu as pltpu

F32 = jnp.float32
BF16 = jnp.bfloat16
HIGHEST = lax.Precision.HIGHEST

LANES = 128
SUBLANES = 8
VMEM_LIMIT_BYTES = 56 * 1024 * 1024

GRID_W = 64
RMS_EPS = 1e-6
A_HEADS = 4
A_HEAD_DIM = 96
B_GROUPS = 4
B_GROUP_DIM = 64
B_WIDTH = B_GROUPS * B_GROUP_DIM
C_HEADS = 4
C_HEAD_DK = 48
C_HEAD_DV = 96
C_GATE_RANK = 16
C_GATE_NORM = 16.0
N_DIR = 2

HEAD_PAD = LANES
A_PAD = A_HEADS * HEAD_PAD
CK_PAD = 64
CK_W = C_HEADS * CK_PAD
CV_W = C_HEADS * HEAD_PAD
ONES_LANE = A_HEAD_DIM

SEQ_BLOCK = 128
SEQ_GROUP = SUBLANES
GK_COPIES = 3

_IN_LAYOUT = (("qa", A_PAD, 1), ("ka", A_PAD, 1), ("va", A_PAD, 1), ("oa", A_PAD, 1),
              ("ga", LANES, N_DIR),
              ("ub", B_WIDTH, 1), ("vb", B_WIDTH, 1),
              ("qc", CK_W, 1), ("kc", CK_W, 1), ("vc", CV_W, 1), ("gc", CV_W, 1),
              ("gk", LANES, N_DIR))
N_IN_PAD = sum(w * n for _, w, n in _IN_LAYOUT)


def _cparams(n_axes):
    return pltpu.CompilerParams(dimension_semantics=("arbitrary",) * n_axes,
                                vmem_limit_bytes=VMEM_LIMIT_BYTES)


def _log_sigmoid(x):
    return jnp.minimum(x, 0.0) - jnp.log(1.0 + jnp.exp(-jnp.abs(x)))


def _silu(x):
    return x * jax.nn.sigmoid(x)


def _gelu_tanh(x):
    c = math.sqrt(2.0 / math.pi)
    return 0.5 * x * (1.0 + jnp.tanh(c * (x + 0.044715 * (x * x * x))))


def _rms_scale(x, n):
    return lax.rsqrt(jnp.sum(x * x, axis=-1, keepdims=True) * (1.0 / n) + RMS_EPS)


def _split_hi_lo(x):
    hi = x.astype(BF16)
    lo = (x - hi.astype(F32)).astype(BF16)
    return hi, lo


def _ada_kernel(s_ref, w_ref, b_ref, o_ref):
    s = _silu(s_ref[...])
    o_ref[0] = jnp.dot(s, w_ref[0], precision=HIGHEST, preferred_element_type=F32) + b_ref[0]


def _ada_call(cond, w_ada, b_ada):
    depth, d, n6 = w_ada.shape
    nb = cond.shape[0]
    tn = 1536
    return pl.pallas_call(
        _ada_kernel,
        grid=(depth, n6 // tn),
        in_specs=[pl.BlockSpec((nb, d), lambda l, j: (0, 0)),
                  pl.BlockSpec((1, d, tn), lambda l, j: (l, 0, j)),
                  pl.BlockSpec((1, 1, tn), lambda l, j: (l, 0, j))],
        out_specs=pl.BlockSpec((1, nb, tn), lambda l, j: (l, 0, j)),
        out_shape=jax.ShapeDtypeStruct((depth, nb, n6), F32),
        compiler_params=_cparams(2),
        name="ada_mod",
    )(cond, w_ada, b_ada.reshape(depth, 1, n6))


def _proj_in_kernel(x_ref, mod_ref, g_ref, w_ref, *out_refs):
    x = x_ref[0]
    y = x * _rms_scale(x, x.shape[-1]) * g_ref[...]
    h = (y * (1.0 + mod_ref[0, 1:2, :]) + mod_ref[0, 0:1, :]).astype(BF16)
    off = 0
    for (_, width, n_dir), o_ref in zip(_IN_LAYOUT, out_refs):
        for di in range(n_dir):
            res = jnp.dot(h, w_ref[:, off:off + width], preferred_element_type=F32)
            for j in range(width // LANES):
                piece = res[:, j * LANES:(j + 1) * LANES]
                if n_dir == 1:
                    o_ref[0, j] = piece
                else:
                    o_ref[di, 0, j] = piece
            off += width


def _proj_in_call(x, mod, g, w, tm):
    bsz, t, d = x.shape
    mod_b = mod.shape[0]
    mod_map = (lambda b, i: (b, 0, 0)) if mod_b > 1 else (lambda b, i: (0, 0, 0))
    out_shapes, out_specs = [], []
    for _, wd, n_dir in _IN_LAYOUT:
        nch = wd // LANES
        if n_dir == 1:
            out_shapes.append(jax.ShapeDtypeStruct((bsz, nch, t, LANES), F32))
            out_specs.append(pl.BlockSpec((1, nch, tm, LANES), lambda b, i: (b, 0, i, 0)))
        else:
            out_shapes.append(jax.ShapeDtypeStruct((n_dir, bsz, nch, t, LANES), F32))
            out_specs.append(pl.BlockSpec((n_dir, 1, nch, tm, LANES), lambda b, i: (0, b, 0, i, 0)))
    outs = pl.pallas_call(
        _proj_in_kernel,
        grid=(bsz, t // tm),
        in_specs=[pl.BlockSpec((1, tm, d), lambda b, i: (b, i, 0)),
                  pl.BlockSpec((1, 6, d), mod_map),
                  pl.BlockSpec((1, d), lambda b, i: (0, 0)),
                  pl.BlockSpec((d, N_IN_PAD), lambda b, i: (0, 0))],
        out_specs=out_specs,
        out_shape=out_shapes,
        compiler_params=_cparams(2),
        name="proj_in",
    )(x, mod, g, w)
    return dict(zip([n for n, _, _ in _IN_LAYOUT], outs))


class _SeqLayout:
    def __init__(self, t_len, column_major):
        self.cm = column_major
        self.t_len = t_len
        self.nblk = t_len // SEQ_BLOCK
        self.group = min(SEQ_GROUP, self.nblk)
        self.ngrp = self.nblk // self.group
        if column_major:
            assert t_len == SEQ_BLOCK * GRID_W and self.group == SEQ_GROUP

    def view(self, a):
        if not self.cm:
            return a
        return a.reshape(a.shape[:-2] + (SEQ_BLOCK, GRID_W, LANES))

    def unview(self, a):
        if not self.cm:
            return a
        return a.reshape(a.shape[:-3] + (self.t_len, LANES))

    def spec(self, c, lead_map, reverse=False):
        def grp(g):
            return self.ngrp - 1 - g if reverse else g

        n_lead = len(lead_map(0, 0))
        nch = c // LANES
        if self.cm:
            return pl.BlockSpec((1,) * n_lead + (nch, SEQ_BLOCK, self.group, LANES),
                                lambda b, g: lead_map(b, g) + (0, 0, grp(g), 0))
        return pl.BlockSpec((1,) * n_lead + (nch, self.group * SEQ_BLOCK, LANES),
                            lambda b, g: lead_map(b, g) + (0, grp(g), 0))

    def halo_specs(self, c):
        nch = c // LANES
        if self.cm:
            shp = (1, nch, SUBLANES, self.group, LANES)
            prev = pl.BlockSpec(shp, lambda b, g: (b, 0, SEQ_BLOCK // SUBLANES - 1, jnp.maximum(g - 1, 0), 0))
            nxt = pl.BlockSpec(shp, lambda b, g: (b, 0, 0, jnp.minimum(g + 1, self.ngrp - 1), 0))
        else:
            per = self.group * SEQ_BLOCK // SUBLANES
            last = self.t_len // SUBLANES - 1
            shp = (1, nch, SUBLANES, LANES)
            prev = pl.BlockSpec(shp, lambda b, g: (b, 0, jnp.maximum(g * per - 1, 0), 0))
            nxt = pl.BlockSpec(shp, lambda b, g: (b, 0, jnp.minimum((g + 1) * per, last), 0))
        return prev, nxt

    @staticmethod
    def _cat(pieces):
        return pieces[0] if len(pieces) == 1 else jnp.concatenate(pieces, axis=1)

    def halo_prev_row(self, ref):
        r = SUBLANES - 1
        if self.cm:
            return self._cat([ref[0, j, r:r + 1, self.group - 1, :] for j in range(ref.shape[1])])
        return self._cat([ref[0, j, r:r + 1, :] for j in range(ref.shape[1])])

    def halo_next_row(self, ref):
        if self.cm:
            return self._cat([ref[0, j, 0:1, 0, :] for j in range(ref.shape[1])])
        return self._cat([ref[0, j, 0:1, :] for j in range(ref.shape[1])])

    def _piece(self, ref, lead, j, i):
        if self.cm:
            flat = ref.at[lead + (j,)].reshape(self.group * SEQ_BLOCK, LANES)
            return flat, (pl.ds(i, SEQ_BLOCK, stride=self.group), slice(None))
        start = i * SEQ_BLOCK
        if not isinstance(i, int):
            start = pl.multiple_of(start, SEQ_BLOCK)
        return ref, lead + (j, pl.ds(start, SEQ_BLOCK), slice(None))

    def load(self, ref, i, n_lead=1):
        lead = (0,) * n_lead
        pieces = []
        for j in range(ref.shape[n_lead]):
            r, idx = self._piece(ref, lead, j, i)
            pieces.append(r[idx])
        return self._cat(pieces)

    def store(self, ref, i, val, n_lead=1):
        lead = (0,) * n_lead
        for j in range(ref.shape[n_lead]):
            r, idx = self._piece(ref, lead, j, i)
            r[idx] = val[:, j * LANES:(j + 1) * LANES]

    def row(self, ref, i, r):
        if self.cm:
            return self._cat([ref[0, j, r:r + 1, i, :] for j in range(ref.shape[1])])
        t = i * SEQ_BLOCK + r
        return self._cat([ref[0, j, t:t + 1, :] for j in range(ref.shape[1])])


def _seq_prep_kernel(lay, q_ref, qp_ref, qn_ref, k_ref, kp_ref, kn_ref, u_ref, v_ref, cw_ref, ng_ref, w_ref,
                     bias_ref, qo_ref, ko_ref, s_ref):
    g = pl.program_id(1)
    L = SEQ_BLOCK
    row = lax.broadcasted_iota(jnp.int32, (L, 1), 0)
    has_prev = (g > 0).astype(F32)
    has_next = (g < lay.ngrp - 1).astype(F32)
    grp = lax.broadcasted_iota(jnp.int32, (1, B_WIDTH), 1) // B_GROUP_DIM

    for i in range(lay.group):
        for x_ref, xp_ref, xn_ref, o_ref, w, scale in ((q_ref, qp_ref, qn_ref, qo_ref, cw_ref[0], A_HEAD_DIM ** -0.5),
                                                       (k_ref, kp_ref, kn_ref, ko_ref, cw_ref[1], None)):
            x = lay.load(x_ref, i)
            prev_row = lay.row(x_ref, i - 1, L - 1) if i > 0 else lay.halo_prev_row(xp_ref) * has_prev
            next_row = lay.row(x_ref, i + 1, 0) if i < lay.group - 1 else lay.halo_next_row(xn_ref) * has_next
            x_dn = jnp.where(row == 0, prev_row, pltpu.roll(x, 1, 0))
            x_up = jnp.where(row == L - 1, next_row, pltpu.roll(x, L - 1, 0))
            y = _silu(w[0:1, :] * x_dn + w[1:2, :] * x + w[2:3, :] * x_up)
            lay.store(o_ref, i, y if scale is None else y * scale)

        u = _gelu_tanh(lay.load(u_ref, i))
        v = _gelu_tanh(lay.load(v_ref, i))
        v = v * _rms_scale(v, B_WIDTH) * ng_ref[...]
        mixed = bias_ref[...]
        for gi in range(B_GROUPS):
            vg = jnp.where(grp == gi, v, 0.0).astype(BF16)
            mixed = mixed + jnp.dot(w_ref[gi], vg, preferred_element_type=F32)
        lay.store(s_ref, i, u * mixed)


def _seq_prep_call(st, conv_w, norm_g, w_s, bias, lay):
    bsz = st["qa"].shape[0]
    lead = lambda b, g: (b,)
    tile = lambda c: lay.spec(c, lead)
    hp, hn = lay.halo_specs(A_PAD)
    full = lambda shp: pl.BlockSpec(shp, lambda b, g: (0,) * len(shp))
    qa, ka, ub, vb = (lay.view(st[n]) for n in ("qa", "ka", "ub", "vb"))
    qo, ko, s = pl.pallas_call(
        functools.partial(_seq_prep_kernel, lay),
        grid=(bsz, lay.ngrp),
        in_specs=[tile(A_PAD), hp, hn, tile(A_PAD), hp, hn, tile(B_WIDTH), tile(B_WIDTH),
                  full((2, 3, A_PAD)), full((1, B_WIDTH)), full((B_GROUPS, SEQ_BLOCK, SEQ_BLOCK)),
                  full((SEQ_BLOCK, B_WIDTH))],
        out_specs=[tile(A_PAD), tile(A_PAD), tile(B_WIDTH)],
        out_shape=[jax.ShapeDtypeStruct(qa.shape, F32), jax.ShapeDtypeStruct(ka.shape, F32),
                   jax.ShapeDtypeStruct(ub.shape, F32)],
        compiler_params=_cparams(2),
        name="seq_prep",
    )(qa, qa, qa, ka, ka, ka, ub, vb, conv_w, norm_g, w_s, bias)
    return lay.unview(qo), lay.unview(ko), lay.unview(s)


def _mlstm_kernel(lay, reverse, q_ref, k_ref, v_ref, g_ref, gb_ref, cmask_ref, cinit_ref, minit_ref,
                  h_ref, cfin_ref, mfin_ref, c_sc, m_sc):
    gidx = pl.program_id(1)
    L = SEQ_BLOCK

    @pl.when(gidx == 0)
    def _():
        c_sc[...] = cinit_ref[0]
        m_sc[...] = minit_ref[0]

    lane = lax.broadcasted_iota(jnp.int32, (1, A_PAD), 1) % HEAD_PAD
    ri = lax.broadcasted_iota(jnp.int32, (L, L), 0)
    ci = lax.broadcasted_iota(jnp.int32, (L, L), 1)
    past = (ci >= ri) if reverse else (ci <= ri)
    last = 0 if reverse else L - 1
    out_lane = lax.broadcasted_iota(jnp.int32, (1, HEAD_PAD), 1)

    def chunk(ii, carry):
        i = lay.group - 1 - ii if reverse else ii
        q = lay.load(q_ref, i).astype(BF16)
        k = lay.load(k_ref, i).astype(BF16)
        v = jnp.where(lane == ONES_LANE, 1.0, lay.load(v_ref, i))

        g = lay.load(g_ref, i, n_lead=2) + gb_ref[0]
        logf = _log_sigmoid(g)
        lf_hi, lf_lo = _split_hi_lo(logf)
        cs = jnp.dot(cmask_ref[...], jnp.concatenate([lf_hi, lf_lo], axis=1), preferred_element_type=F32)
        bc = cs[:, 0:LANES] + cs[:, LANES:2 * LANES]
        g_t = g.T
        bc_t = bc.T

        outs = []
        for h in range(A_HEADS):
            sl = slice(h * HEAD_PAD, (h + 1) * HEAD_PAD)
            fcol = A_HEADS + h
            qh, kh, vh = q[:, sl], k[:, sl], v[:, sl]
            b_col = bc[:, fcol:fcol + 1]
            b_row = bc_t[fcol:fcol + 1, :]
            i_col = g[:, h:h + 1]
            i_row = g_t[h:h + 1, :]
            m_prev = m_sc[h:h + 1, 0:1]
            b_last = bc[last:last + 1, fcol:fcol + 1]
            ct = c_sc[h]

            d_log = jnp.where(past, b_col - b_row + i_row, -jnp.inf)
            inter_log = b_col + m_prev
            m_t = jnp.maximum(inter_log, jnp.max(d_log, axis=1, keepdims=True))
            qk = lax.dot_general(qh, kh, (((1,), (1,)), ((), ())), preferred_element_type=F32)
            s = qk * jnp.exp(d_log - m_t)
            inter = jnp.exp(inter_log - m_t)
            num = jnp.dot(s.astype(BF16), vh.astype(BF16), preferred_element_type=F32) \
                + inter * jnp.dot(qh, ct.astype(BF16), preferred_element_type=F32)
            den = num[:, ONES_LANE:ONES_LANE + 1]
            inv = 1.0 / jnp.maximum(jnp.abs(den), jnp.exp(-m_t))
            outs.append(jnp.where(out_lane < A_HEAD_DIM, num * inv, 0.0))

            w_log = b_last - b_col + i_col
            m_new = jnp.maximum(b_last + m_prev, jnp.max(w_log, axis=0, keepdims=True))
            decay = jnp.exp(b_last + m_prev - m_new)
            wv = (jnp.exp(w_log - m_new) * vh).astype(BF16)
            upd = lax.dot_general(kh, wv, (((0,), (0,)), ((), ())), preferred_element_type=F32)
            c_sc[h] = decay * ct + upd
            m_sc[h:h + 1, :] = jnp.broadcast_to(m_new, (1, LANES))
        lay.store(h_ref, i, jnp.concatenate(outs, axis=1))
        return carry

    lax.fori_loop(0, lay.group, chunk, 0)

    @pl.when(gidx == lay.ngrp - 1)
    def _():
        cfin_ref[0] = c_sc[...]
        mfin_ref[0] = m_sc[...]


def _mlstm_call(qconv, kconv, st, gate_b, cinit, minit, lay, reverse):
    bsz = qconv.shape[0]
    di = 1 if reverse else 0
    tile = lambda c: lay.spec(c, lambda b, g: (b,), reverse)
    c_spec = pl.BlockSpec((1, A_HEADS, HEAD_PAD, HEAD_PAD), lambda b, g: (b, 0, 0, 0))
    m_spec = pl.BlockSpec((1, SUBLANES, LANES), lambda b, g: (b, 0, 0))
    qa, ka, va, ga = lay.view(qconv), lay.view(kconv), lay.view(st["va"]), lay.view(st["ga"])
    cmask, _ = _scan_masks(reverse)
    h, cfin, mfin = pl.pallas_call(
        functools.partial(_mlstm_kernel, lay, reverse),
        grid=(bsz, lay.ngrp),
        in_specs=[tile(A_PAD), tile(A_PAD), tile(A_PAD),
                  lay.spec(LANES, lambda b, g: (di, b), reverse),
                  pl.BlockSpec((1, 1, LANES), lambda b, g: (di, 0, 0)),
                  pl.BlockSpec(cmask.shape, lambda b, g: (0, 0)),
                  c_spec, m_spec],
        out_specs=[tile(A_PAD), c_spec, m_spec],
        out_shape=[jax.ShapeDtypeStruct(qa.shape, F32),
                   jax.ShapeDtypeStruct(cinit.shape, F32),
                   jax.ShapeDtypeStruct(minit.shape, F32)],
        scratch_shapes=[pltpu.VMEM((A_HEADS, HEAD_PAD, HEAD_PAD), F32),
                        pltpu.VMEM((SUBLANES, LANES), F32)],
        compiler_params=_cparams(2),
        name="mlstm_scan_bwd" if reverse else "mlstm_scan_fwd",
    )(qa, ka, va, ga, gate_b, cmask, cinit, minit)
    return lay.unview(h), cfin, mfin


def _gla_kernel(lay, reverse, q_ref, k_ref, v_ref, gk_ref, w2_ref, bgk_ref, cmask_ref, lmask_ref, sinit_ref,
                o_ref, sfin_ref, s_sc, sbd_sc):
    gidx = pl.program_id(1)
    L = SEQ_BLOCK

    @pl.when(gidx == 0)
    def _():
        s_sc[...] = sinit_ref[0]
        sbd_sc[...] = jnp.zeros_like(sbd_sc)
        for h in range(C_HEADS):
            sbd_sc[h * CK_PAD:(h + 1) * CK_PAD, h * HEAD_PAD:(h + 1) * HEAD_PAD] = sinit_ref[0, h].astype(BF16)

    glane = lax.broadcasted_iota(jnp.int32, (1, LANES), 1)
    row = lax.broadcasted_iota(jnp.int32, (L, 1), 0)
    hlane = glane // CK_PAD
    col = lambda x, h: x[:, (h // 2) * LANES:(h // 2 + 1) * LANES]
    last = 0 if reverse else L - 1
    ones = jnp.ones((L, HEAD_PAD), BF16)

    def rows_from(x, idx_of_row_block, rows_per_block):
        n = L // rows_per_block
        return jnp.concatenate([jnp.broadcast_to(x[idx_of_row_block(i):idx_of_row_block(i) + 1],
                                                 (rows_per_block, x.shape[1])) for i in range(n)], axis=0)

    def chunk(ii, carry):
        i = lay.group - 1 - ii if reverse else ii
        g = lay.load(gk_ref, i, n_lead=2)
        g_hi = g.astype(BF16).astype(F32)
        g_split = jnp.where(glane < 2 * C_GATE_RANK, g_hi, g - g_hi).astype(BF16)
        gk = jnp.dot(g_split, w2_ref[0], preferred_element_type=F32) + bgk_ref[0]
        la = _log_sigmoid(gk) * (1.0 / C_GATE_NORM)
        la_hi, la_lo = _split_hi_lo(la)
        cs = jnp.dot(cmask_ref[...], jnp.concatenate([la_hi, la_lo], axis=1), preferred_element_type=F32)
        bs = cs[:, 0:CK_W] + cs[:, CK_W:2 * CK_W]
        btot = bs[last:last + 1]

        qs = lay.load(q_ref, i) * (C_HEAD_DK ** -0.5)
        kk = lay.load(k_ref, i)
        vb = lay.load(v_ref, i).astype(BF16)
        k_head = [jnp.where(hlane == h % 2, col(kk, h), 0.0) for h in range(C_HEADS)]
        att = [None] * C_HEADS

        def add_level(q_t, k_mul, level, k_rows=None):
            q_b = q_t.astype(BF16)
            for h in range(C_HEADS):
                k_t = k_head[h] if k_mul is None else k_head[h] * col(k_mul, h)
                if k_rows is not None:
                    k_t = jnp.where(k_rows, k_t, 0.0)
                a = lax.dot_general(col(q_b, h), k_t.astype(BF16), (((1,), (1,)), ((), ())),
                                    preferred_element_type=F32)
                if level is not None:
                    a = a * lmask_ref[level]
                att[h] = a if att[h] is None else att[h] + a

        level = 0
        m = L // 2
        while m >= 1:
            pos = row % (2 * m)
            is_q = (pos < m) if reverse else (pos >= m)
            k_rows = None
            if m == 1:
                q_t = jnp.where(is_q, qs * jnp.exp(la), 0.0)
                k_mul = None
                k_rows = jnp.logical_not(is_q)
            else:
                bnd = m if reverse else m - 1
                if 2 * m >= SUBLANES:
                    pref = rows_from(bs, lambda r: r * 2 * m + bnd, 2 * m)
                else:
                    p0 = rows_from(bs, lambda r: r * SUBLANES + bnd, SUBLANES)
                    p1 = rows_from(bs, lambda r: r * SUBLANES + 2 * m + bnd, SUBLANES)
                    pref = jnp.where(row % SUBLANES < 2 * m, p0, p1)
                q_t = qs * jnp.exp(jnp.where(is_q, bs - pref, -jnp.inf))
                k_mul = jnp.exp(jnp.where(is_q, -jnp.inf, pref - bs))
            add_level(q_t, k_mul, None if m == L // 2 else level, k_rows)
            level += 1
            m //= 2
        add_level(qs, None, level)

        o_inter = jnp.dot((qs * jnp.exp(bs)).astype(BF16), sbd_sc[...], preferred_element_type=F32)
        o_heads = []
        for h in range(C_HEADS):
            cols = slice(h * HEAD_PAD, (h + 1) * HEAD_PAD)
            o_heads.append(o_inter[:, cols] + jnp.dot(att[h].astype(BF16), vb[:, cols],
                                                      preferred_element_type=F32))
        lay.store(o_ref, i, jnp.concatenate(o_heads, axis=1))

        ke_t = (kk * jnp.exp(btot - bs)).T.astype(BF16)
        tdot = lambda a: lax.dot_general(a, ones, (((0,), (0,)), ((), ())), preferred_element_type=F32)
        dec_col = jnp.exp(tdot(la_hi) + tdot(la_lo))
        for h in range(C_HEADS):
            rows = slice(h * CK_PAD, (h + 1) * CK_PAD)
            cols = slice(h * HEAD_PAD, (h + 1) * HEAD_PAD)
            upd = jnp.dot(ke_t[rows], vb[:, cols], preferred_element_type=F32)
            s_new = dec_col[rows] * s_sc[h] + upd
            s_sc[h] = s_new
            sbd_sc[rows, cols] = s_new.astype(BF16)
        return carry

    lax.fori_loop(0, lay.group, chunk, 0)

    @pl.when(gidx == lay.ngrp - 1)
    def _():
        sfin_ref[0] = s_sc[...]


def _scan_masks(reverse):
    t = jnp.arange(SEQ_BLOCK)[:, None]
    u = jnp.arange(SEQ_BLOCK)[None, :]
    cmask = ((u >= t) if reverse else (u <= t)).astype(BF16)
    sizes = []
    m = SEQ_BLOCK
    while m >= 1:
        sizes.append(m)
        m //= 2
    lmask = jnp.stack([(t // sz) == (u // sz) for sz in sizes]).astype(F32)
    return cmask, lmask


def _gla_call(st, w2s, b_gk, sinit, lay, reverse):
    bsz = st["qc"].shape[0]
    di = 1 if reverse else 0
    tile = lambda c: lay.spec(c, lambda b, g: (b,), reverse)
    full = lambda shp: pl.BlockSpec(shp, lambda b, g: (0,) * len(shp))
    state_spec = pl.BlockSpec((1, C_HEADS, CK_PAD, HEAD_PAD), lambda b, g: (b, 0, 0, 0))
    qc, kc, vc, gk = (lay.view(st[n]) for n in ("qc", "kc", "vc", "gk"))
    cmask, lmask = _scan_masks(reverse)
    o, sfin = pl.pallas_call(
        functools.partial(_gla_kernel, lay, reverse),
        grid=(bsz, lay.ngrp),
        in_specs=[tile(CK_W), tile(CK_W), tile(CV_W),
                  lay.spec(LANES, lambda b, g: (di, b), reverse),
                  pl.BlockSpec((1, LANES, CK_W), lambda b, g: (di, 0, 0)),
                  pl.BlockSpec((1, 1, CK_W), lambda b, g: (di, 0, 0)),
                  full(cmask.shape), full(lmask.shape), state_spec],
        out_specs=[tile(CV_W), state_spec],
        out_shape=[jax.ShapeDtypeStruct(vc.shape, F32),
                   jax.ShapeDtypeStruct(sinit.shape, F32)],
        scratch_shapes=[pltpu.VMEM((C_HEADS, CK_PAD, HEAD_PAD), F32),
                        pltpu.VMEM((CK_W, CV_W), BF16)],
        compiler_params=_cparams(2),
        name="gla_scan_bwd" if reverse else "gla_scan_fwd",
    )(qc, kc, vc, gk, w2s, b_gk, cmask, lmask, sinit)
    return lay.unview(o), sfin


def _mix_out_kernel(x_ref, mod_ref, haf_ref, hab_ref, hcf_ref, hcb_ref, oa_ref, gc_ref, s_ref, na_ref, nc_ref,
                    w_ref, o_ref):
    def head_norm(xh, gain):
        return xh * _rms_scale(xh, A_HEAD_DIM) * gain

    acc = None
    for j in range(B_WIDTH // LANES):
        r0 = A_PAD + j * LANES
        part = jnp.dot(s_ref[0, j].astype(BF16), w_ref[r0:r0 + LANES, :], preferred_element_type=F32)
        acc = part if acc is None else acc + part
    for h in range(A_HEADS):
        sl = slice(h * HEAD_PAD, (h + 1) * HEAD_PAD)
        a = head_norm(haf_ref[0, h] + hab_ref[0, h], na_ref[:, sl])
        c = head_norm(hcf_ref[0, h] + hcb_ref[0, h], nc_ref[:, sl])
        ah = (a * jax.nn.sigmoid(oa_ref[0, h])).astype(BF16)
        ch = (c * _silu(gc_ref[0, h])).astype(BF16)
        acc = acc + jnp.dot(ah, w_ref[h * HEAD_PAD:(h + 1) * HEAD_PAD, :], preferred_element_type=F32)
        c0 = A_PAD + B_WIDTH + h * HEAD_PAD
        acc = acc + jnp.dot(ch, w_ref[c0:c0 + HEAD_PAD, :], preferred_element_type=F32)
    o_ref[0] = x_ref[0] + mod_ref[0, 2:3, :] * acc


def _mix_out_call(x, mod, haf, hab, hcf, hcb, oa, gc, s, na, nc, w, tm):
    bsz, t, d = x.shape
    mod_map = (lambda b, i: (b, 0, 0)) if mod.shape[0] > 1 else (lambda b, i: (0, 0, 0))
    row = lambda c: pl.BlockSpec((1, tm, c), lambda b, i: (b, i, 0))
    chunked = lambda c: pl.BlockSpec((1, c // LANES, tm, LANES), lambda b, i: (b, 0, i, 0))
    full = lambda shp: pl.BlockSpec(shp, lambda b, i: (0,) * len(shp))
    return pl.pallas_call(
        _mix_out_kernel,
        grid=(bsz, t // tm),
        in_specs=[row(d), pl.BlockSpec((1, 6, d), mod_map), chunked(A_PAD), chunked(A_PAD), chunked(CV_W),
                  chunked(CV_W), chunked(A_PAD), chunked(CV_W), chunked(B_WIDTH),
                  full((1, A_PAD)), full((1, CV_W)), full(w.shape)],
        out_specs=row(d),
        out_shape=jax.ShapeDtypeStruct(x.shape, F32),
        compiler_params=_cparams(2),
        name="mix_out",
    )(x, mod, haf, hab, hcf, hcb, oa, gc, s, na, nc, w)


def _ffn_kernel(nk, final_norm, x_ref, mod_ref, g_ref, wg_ref, wu_ref, wo_ref, fg_ref, o_ref, h_sc, acc_sc):
    kk = pl.program_id(2)

    @pl.when(kk == 0)
    def _():
        x = x_ref[0]
        y = x * _rms_scale(x, x.shape[-1]) * g_ref[...]
        h_sc[...] = (y * (1.0 + mod_ref[0, 4:5, :]) + mod_ref[0, 3:4, :]).astype(BF16)
        acc_sc[...] = jnp.zeros_like(acc_sc)

    h = h_sc[...]
    gate = jnp.dot(h, wg_ref[...], preferred_element_type=F32)
    up = jnp.dot(h, wu_ref[...], preferred_element_type=F32)
    act = (_silu(gate) * up).astype(BF16)
    acc_sc[...] += jnp.dot(act, wo_ref[...], preferred_element_type=F32)

    @pl.when(kk == nk - 1)
    def _():
        y = x_ref[0] + mod_ref[0, 5:6, :] * acc_sc[...]
        if final_norm:
            y = y * _rms_scale(y, y.shape[-1]) * fg_ref[...]
        o_ref[0] = y


def _ffn_call(x, mod, g, w_in, w_out, final_g, final_norm, tm, nk):
    bsz, t, d = x.shape
    d_ff = w_out.shape[0]
    tk = d_ff // nk
    mod_map = (lambda b, i, k: (b, 0, 0)) if mod.shape[0] > 1 else (lambda b, i, k: (0, 0, 0))
    return pl.pallas_call(
        functools.partial(_ffn_kernel, nk, final_norm),
        grid=(bsz, t // tm, nk),
        in_specs=[pl.BlockSpec((1, tm, d), lambda b, i, k: (b, i, 0)),
                  pl.BlockSpec((1, 6, d), mod_map),
                  pl.BlockSpec((1, d), lambda b, i, k: (0, 0)),
                  pl.BlockSpec((d, tk), lambda b, i, k: (0, k)),
                  pl.BlockSpec((d, tk), lambda b, i, k: (0, nk + k)),
                  pl.BlockSpec((tk, d), lambda b, i, k: (k, 0)),
                  pl.BlockSpec((1, d), lambda b, i, k: (0, 0))],
        out_specs=pl.BlockSpec((1, tm, d), lambda b, i, k: (b, i, 0)),
        out_shape=jax.ShapeDtypeStruct(x.shape, F32),
        scratch_shapes=[pltpu.VMEM((tm, d), BF16), pltpu.VMEM((tm, d), F32)],
        compiler_params=_cparams(3),
        name="ffn",
    )(x, mod, g, w_in, w_in, w_out, final_g)


def _pad_heads(w, n_heads, dim, pad, axis=-1):
    w = jnp.moveaxis(w, axis, -1)
    lead = w.shape[:-1]
    w = w.reshape(lead + (n_heads, dim))
    w = jnp.pad(w, [(0, 0)] * len(lead) + [(0, 0), (0, pad - dim)])
    return jnp.moveaxis(w.reshape(lead + (n_heads * pad,)), -1, axis)


def _pad_to(w, width):
    return jnp.pad(w, [(0, 0)] * (w.ndim - 1) + [(0, width - w.shape[-1])])


def _prep_w_in(w_in):
    a_w = A_HEADS * A_HEAD_DIM
    ck = C_HEADS * C_HEAD_DK
    cv = C_HEADS * C_HEAD_DV
    sizes = (a_w, a_w, a_w, a_w, N_DIR * 2 * A_HEADS, B_WIDTH, B_WIDTH, ck, ck, cv, cv, N_DIR * C_GATE_RANK)
    pts = [sum(sizes[:i + 1]) for i in range(len(sizes) - 1)]
    qa, ka, va, oa, ga, ub, vb, qc, kc, vc, gc, gkc = jnp.split(w_in, pts, axis=-1)
    pa = lambda w: _pad_heads(w, A_HEADS, A_HEAD_DIM, HEAD_PAD)
    pk = lambda w: _pad_heads(w, C_HEADS, C_HEAD_DK, CK_PAD)
    pv = lambda w: _pad_heads(w, C_HEADS, C_HEAD_DV, HEAD_PAD)
    ng = 2 * A_HEADS
    gk_dir = lambda w: _pad_to(jnp.tile(w, (1, GK_COPIES)), LANES)
    cols = [pa(qa), pa(ka), pa(va), pa(oa),
            _pad_to(ga[:, :ng], LANES), _pad_to(ga[:, ng:], LANES),
            ub, vb, pk(qc), pk(kc), pv(vc), pv(gc),
            gk_dir(gkc[:, :C_GATE_RANK]), gk_dir(gkc[:, C_GATE_RANK:])]
    return jnp.concatenate(cols, axis=-1).astype(BF16)


def _prep_w_gk2(w_gk2):
    w = _pad_heads(w_gk2, C_HEADS, C_HEAD_DK, CK_PAD)
    hi = w.astype(BF16)
    lo = (w - hi.astype(F32)).astype(BF16)
    stacked = jnp.concatenate([hi, lo, hi], axis=1)
    return jnp.pad(stacked, ((0, 0), (0, LANES - GK_COPIES * C_GATE_RANK), (0, 0)))


def _prep_w_out(w_out):
    a_w = A_HEADS * A_HEAD_DIM
    wa, wb, wc = w_out[:a_w], w_out[a_w:a_w + B_WIDTH], w_out[a_w + B_WIDTH:]
    wa = _pad_heads(wa, A_HEADS, A_HEAD_DIM, HEAD_PAD, axis=0)
    wc = _pad_heads(wc, C_HEADS, C_HEAD_DV, HEAD_PAD, axis=0)
    return jnp.concatenate([wa, wb, wc], axis=0).astype(BF16)


def _mixer_scans(st, lw, a_state, c_state, lay):
    qconv, kconv, s = _seq_prep_call(st, lw["conv"], lw["sgu_g"], lw["sgu_w"], lw["sgu_bias"], lay)
    ha, a_fin, hc, c_fin = [], [], [], []
    for di, reverse in enumerate((False, True)):
        h, cfin, mfin = _mlstm_call(qconv, kconv, st, lw["gate_b"], a_state[di][0], a_state[di][1], lay, reverse)
        ha.append(h)
        a_fin.append((cfin, mfin))
        o, sfin = _gla_call(st, lw["w_gk2"], lw["b_gk"], c_state[di], lay, reverse)
        hc.append(o)
        c_fin.append(sfin)
    return ha, hc, s, a_fin, c_fin


def kernel(x, c, ctx, c_ctx, norm1_g, norm2_g, w_ada, b_ada, w_in, mlstm_conv, mlstm_gate_b,
           mlstm_norm_g, gla_w_gk2, gla_b_gk, gla_norm_g, sgu_norm_g, sgu_w, sgu_b, w_out,
           w_ffn_in, w_ffn_out, final_g):
    bsz, seq, d = x.shape
    ctx_len = ctx.shape[1]
    depth = w_in.shape[0]
    n_cond = 2 * SUBLANES
    cond = jnp.zeros((n_cond, d), F32).at[:bsz].set(c).at[bsz].set(c_ctx)
    mods = _ada_call(cond, w_ada, b_ada).reshape(depth, n_cond, 6, d)

    fg = final_g.reshape(1, d)
    x_lat, x_ctx = x, ctx
    for l in range(depth):
        need_ctx = l < depth - 1
        mod_lat = mods[l, :bsz]
        mod_ctx = mods[l, bsz:bsz + 1]
        gb = mlstm_gate_b[l].reshape(N_DIR, 1, 2 * A_HEADS)
        lw = {
            "conv": jnp.stack([_pad_heads(mlstm_conv[l][:, :A_HEADS * A_HEAD_DIM], A_HEADS, A_HEAD_DIM, HEAD_PAD),
                               _pad_heads(mlstm_conv[l][:, A_HEADS * A_HEAD_DIM:], A_HEADS, A_HEAD_DIM, HEAD_PAD)]),
            "gate_b": _pad_to(gb, LANES),
            "w_gk2": _prep_w_gk2(gla_w_gk2[l]),
            "b_gk": _pad_heads(gla_b_gk[l], C_HEADS, C_HEAD_DK, CK_PAD).reshape(N_DIR, 1, CK_W),
            "sgu_g": sgu_norm_g[l].reshape(1, B_WIDTH),
            "sgu_w": sgu_w[l].astype(BF16),
            "sgu_bias": jnp.repeat(sgu_b[l].T, B_GROUP_DIM, axis=1),
        }
        w_in_l = _prep_w_in(w_in[l])
        w_out_l = _prep_w_out(w_out[l])
        na = _pad_heads(mlstm_norm_g[l], A_HEADS, A_HEAD_DIM, HEAD_PAD).reshape(1, A_PAD)
        nc = _pad_heads(jnp.tile(gla_norm_g[l], C_HEADS), C_HEADS, C_HEAD_DV, HEAD_PAD).reshape(1, CV_W)
        g1 = norm1_g[l].reshape(1, d)
        g2 = norm2_g[l].reshape(1, d)
        w_ffn_in_l = w_ffn_in[l].astype(BF16)
        w_ffn_out_l = w_ffn_out[l].astype(BF16)

        st_ctx = _proj_in_call(x_ctx, mod_ctx, g1, w_in_l, tm=256)
        st_lat = _proj_in_call(x_lat, mod_lat, g1, w_in_l, tm=256)

        a0 = (jnp.zeros((bsz, A_HEADS, HEAD_PAD, HEAD_PAD), F32), jnp.zeros((bsz, SUBLANES, LANES), F32))
        c0 = jnp.zeros((bsz, C_HEADS, CK_PAD, HEAD_PAD), F32)
        lay_ctx = _SeqLayout(ctx_len, False)
        lay_lat = _SeqLayout(seq, l % 2 == 1)
        ha_c, hc_c, s_c, a_state, c_state = _mixer_scans(st_ctx, lw, (a0, a0), (c0, c0), lay_ctx)
        ha_l, hc_l, s_l, _, _ = _mixer_scans(st_lat, lw, a_state, c_state, lay_lat)

        x_lat = _mix_out_call(x_lat, mod_lat, ha_l[0], ha_l[1], hc_l[0], hc_l[1], st_lat["oa"], st_lat["gc"], s_l,
                              na, nc, w_out_l, tm=256)
        x_lat = _ffn_call(x_lat, mod_lat, g2, w_ffn_in_l, w_ffn_out_l, fg, not need_ctx, tm=512, nk=2)
        if need_ctx:
            x_ctx = _mix_out_call(x_ctx, mod_ctx, ha_c[0], ha_c[1], hc_c[0], hc_c[1], st_ctx["oa"], st_ctx["gc"], s_c,
                                  na, nc, w_out_l, tm=256)
            x_ctx = _ffn_call(x_ctx, mod_ctx, g2, w_ffn_in_l, w_ffn_out_l, fg, False, tm=256, nk=2)
    return x_lat
```

```python
import functools
import math

import jax
import jax.numpy as jnp
from jax import lax
from jax.experimental import pallas as pl
from jax.experimental.pallas import tpu as pltpu

F32 = jnp.float32
BF16 = jnp.bfloat16
HIGHEST = lax.Precision.HIGHEST

LANES = 128
SUBLANES = 8
VMEM_LIMIT_BYTES = 56 * 1024 * 1024

GRID_W = 64
RMS_EPS = 1e-6
A_HEADS = 4
A_HEAD_DIM = 96
B_GROUPS = 4
B_GROUP_DIM = 64
B_WIDTH = B_GROUPS * B_GROUP_DIM
C_HEADS = 4
C_HEAD_DK = 48
C_HEAD_DV = 96
C_GATE_RANK = 16
C_GATE_NORM = 16.0
N_DIR = 2

HEAD_PAD = LANES
A_PAD = A_HEADS * HEAD_PAD
CK_PAD = 64
CK_W = C_HEADS * CK_PAD
CV_W = C_HEADS * HEAD_PAD
ONES_LANE = A_HEAD_DIM

SEQ_BLOCK = 128
SEQ_GROUP = SUBLANES
MLSTM_BATCH = 1
GLA_BATCH = 2
GK_COPIES = 3

PROJ_ROWS = 512
MIX_ROWS = 512
FFN_ROWS = 512
FFN_SPLIT = 1

_IN_LAYOUT = (("qa", A_PAD, 1), ("ka", A_PAD, 1), ("va", A_PAD, 1), ("oa", A_PAD, 1),
              ("ga", LANES, N_DIR),
              ("ub", B_WIDTH, 1), ("vb", B_WIDTH, 1),
              ("qc", CK_W, 1), ("kc", CK_W, 1), ("vc", CV_W, 1), ("gc", CV_W, 1),
              ("gk", LANES, N_DIR))
N_IN_PAD = sum(w * n for _, w, n in _IN_LAYOUT)


def _cparams(n_axes):
    return pltpu.CompilerParams(dimension_semantics=("arbitrary",) * n_axes,
                                vmem_limit_bytes=VMEM_LIMIT_BYTES)


def _resident(shape, n_grid):
    zeros = (0,) * len(shape)
    maps = {2: lambda a, b: zeros, 3: lambda a, b, c: zeros}
    return pl.BlockSpec(shape, maps[n_grid], pipeline_mode=pl.Buffered(1))


def _row_tile(t_len, want):
    return min(want, t_len)


def _log_sigmoid(x):
    return jnp.minimum(x, 0.0) - jnp.log(1.0 + jnp.exp(-jnp.abs(x)))


def _silu(x):
    return x * jax.nn.sigmoid(x)


def _gelu_tanh(x):
    c = math.sqrt(2.0 / math.pi)
    return 0.5 * x * (1.0 + jnp.tanh(c * (x + 0.044715 * (x * x * x))))


def _rms_scale(x, n):
    return lax.rsqrt(jnp.sum(x * x, axis=-1, keepdims=True) * (1.0 / n) + RMS_EPS)


def _split_hi_lo(x):
    hi = x.astype(BF16)
    lo = (x - hi.astype(F32)).astype(BF16)
    return hi, lo


def _ada_kernel(s_ref, w_ref, b_ref, o_ref):
    s = _silu(s_ref[...])
    o_ref[0] = jnp.dot(s, w_ref[0], precision=HIGHEST, preferred_element_type=F32) + b_ref[0]


def _ada_call(cond, w_ada, b_ada):
    depth, d, n6 = w_ada.shape
    nb = cond.shape[0]
    tn = 1536
    return pl.pallas_call(
        _ada_kernel,
        grid=(depth, n6 // tn),
        in_specs=[pl.BlockSpec((nb, d), lambda l, j: (0, 0)),
                  pl.BlockSpec((1, d, tn), lambda l, j: (l, 0, j)),
                  pl.BlockSpec((1, 1, tn), lambda l, j: (l, 0, j))],
        out_specs=pl.BlockSpec((1, nb, tn), lambda l, j: (l, 0, j)),
        out_shape=jax.ShapeDtypeStruct((depth, nb, n6), F32),
        compiler_params=_cparams(2),
        name="ada_mod",
    )(cond, w_ada, b_ada.reshape(depth, 1, n6))


def _proj_in_kernel(x_ref, mod_ref, g_ref, w_ref, *out_refs):
    x = x_ref[0]
    y = x * _rms_scale(x, x.shape[-1]) * g_ref[...]
    h = (y * (1.0 + mod_ref[0, 1:2, :]) + mod_ref[0, 0:1, :]).astype(BF16)
    off = 0
    for (_, width, n_dir), o_ref in zip(_IN_LAYOUT, out_refs):
        for di in range(n_dir):
            res = jnp.dot(h, w_ref[:, off:off + width], preferred_element_type=F32)
            for j in range(width // LANES):
                piece = res[:, j * LANES:(j + 1) * LANES]
                if n_dir == 1:
                    o_ref[0, j] = piece
                else:
                    o_ref[di, 0, j] = piece
            off += width


def _proj_in_call(x, mod, g, w, tm):
    bsz, t, d = x.shape
    mod_b = mod.shape[0]
    mod_map = (lambda b, i: (b, 0, 0)) if mod_b > 1 else (lambda b, i: (0, 0, 0))
    out_shapes, out_specs = [], []
    for _, wd, n_dir in _IN_LAYOUT:
        nch = wd // LANES
        if n_dir == 1:
            out_shapes.append(jax.ShapeDtypeStruct((bsz, nch, t, LANES), F32))
            out_specs.append(pl.BlockSpec((1, nch, tm, LANES), lambda b, i: (b, 0, i, 0)))
        else:
            out_shapes.append(jax.ShapeDtypeStruct((n_dir, bsz, nch, t, LANES), F32))
            out_specs.append(pl.BlockSpec((n_dir, 1, nch, tm, LANES), lambda b, i: (0, b, 0, i, 0)))
    outs = pl.pallas_call(
        _proj_in_kernel,
        grid=(bsz, t // tm),
        in_specs=[pl.BlockSpec((1, tm, d), lambda b, i: (b, i, 0)),
                  pl.BlockSpec((1, 6, d), mod_map),
                  pl.BlockSpec((1, d), lambda b, i: (0, 0)),
                  _resident((d, N_IN_PAD), 2)],
        out_specs=out_specs,
        out_shape=out_shapes,
        compiler_params=_cparams(2),
        name="proj_in",
    )(x, mod, g, w)
    return dict(zip([n for n, _, _ in _IN_LAYOUT], outs))


class _SeqLayout:
    def __init__(self, t_len, column_major):
        self.cm = column_major
        self.t_len = t_len
        self.nblk = t_len // SEQ_BLOCK
        self.group = min(SEQ_GROUP, self.nblk)
        self.ngrp = self.nblk // self.group
        if column_major:
            assert t_len == SEQ_BLOCK * GRID_W and self.group == SEQ_GROUP

    def view(self, a):
        if not self.cm:
            return a
        return a.reshape(a.shape[:-2] + (SEQ_BLOCK, GRID_W, LANES))

    def unview(self, a):
        if not self.cm:
            return a
        return a.reshape(a.shape[:-3] + (self.t_len, LANES))

    def spec(self, c, lead_map, reverse=False, batch=1):
        def grp(g):
            return self.ngrp - 1 - g if reverse else g

        n_lead = len(lead_map(0, 0))
        lead_blk = (1,) * (n_lead - 1) + (batch,)
        nch = c // LANES
        if self.cm:
            return pl.BlockSpec(lead_blk + (nch, SEQ_BLOCK, self.group, LANES),
                                lambda b, g: lead_map(b, g) + (0, 0, grp(g), 0))
        return pl.BlockSpec(lead_blk + (nch, self.group * SEQ_BLOCK, LANES),
                            lambda b, g: lead_map(b, g) + (0, grp(g), 0))

    def halo_specs(self, c):
        nch = c // LANES
        if self.cm:
            shp = (1, nch, SUBLANES, self.group, LANES)
            prev = pl.BlockSpec(shp, lambda b, g: (b, 0, SEQ_BLOCK // SUBLANES - 1, jnp.maximum(g - 1, 0), 0))
            nxt = pl.BlockSpec(shp, lambda b, g: (b, 0, 0, jnp.minimum(g + 1, self.ngrp - 1), 0))
        else:
            per = self.group * SEQ_BLOCK // SUBLANES
            last = self.t_len // SUBLANES - 1
            shp = (1, nch, SUBLANES, LANES)
            prev = pl.BlockSpec(shp, lambda b, g: (b, 0, jnp.maximum(g * per - 1, 0), 0))
            nxt = pl.BlockSpec(shp, lambda b, g: (b, 0, jnp.minimum((g + 1) * per, last), 0))
        return prev, nxt

    @staticmethod
    def _cat(pieces):
        return pieces[0] if len(pieces) == 1 else jnp.concatenate(pieces, axis=1)

    def halo_prev_row(self, ref):
        r = SUBLANES - 1
        if self.cm:
            return self._cat([ref[0, j, r:r + 1, self.group - 1, :] for j in range(ref.shape[1])])
        return self._cat([ref[0, j, r:r + 1, :] for j in range(ref.shape[1])])

    def halo_next_row(self, ref):
        if self.cm:
            return self._cat([ref[0, j, 0:1, 0, :] for j in range(ref.shape[1])])
        return self._cat([ref[0, j, 0:1, :] for j in range(ref.shape[1])])

    def _piece(self, ref, lead, j, i):
        if self.cm:
            flat = ref.at[lead + (j,)].reshape(self.group * SEQ_BLOCK, LANES)
            return flat, (pl.ds(i, SEQ_BLOCK, stride=self.group), slice(None))
        start = i * SEQ_BLOCK
        if not isinstance(i, int):
            start = pl.multiple_of(start, SEQ_BLOCK)
        return ref, lead + (j, pl.ds(start, SEQ_BLOCK), slice(None))

    def load(self, ref, i, lead=(0,)):
        pieces = []
        for j in range(ref.shape[len(lead)]):
            r, idx = self._piece(ref, lead, j, i)
            pieces.append(r[idx])
        return self._cat(pieces)

    def store(self, ref, i, val, lead=(0,)):
        for j in range(ref.shape[len(lead)]):
            r, idx = self._piece(ref, lead, j, i)
            r[idx] = val[:, j * LANES:(j + 1) * LANES]

    def row(self, ref, i, r):
        if self.cm:
            return self._cat([ref[0, j, r:r + 1, i, :] for j in range(ref.shape[1])])
        t = i * SEQ_BLOCK + r
        return self._cat([ref[0, j, t:t + 1, :] for j in range(ref.shape[1])])


def _seq_prep_kernel(lay, q_ref, qp_ref, qn_ref, k_ref, kp_ref, kn_ref, u_ref, v_ref, cw_ref, ng_ref, w_ref,
                     bias_ref, qo_ref, ko_ref, s_ref):
    g = pl.program_id(1)
    L = SEQ_BLOCK
    row = lax.broadcasted_iota(jnp.int32, (L, 1), 0)
    has_prev = (g > 0).astype(F32)
    has_next = (g < lay.ngrp - 1).astype(F32)
    grp = lax.broadcasted_iota(jnp.int32, (1, B_WIDTH), 1) // B_GROUP_DIM

    for i in range(lay.group):
        for x_ref, xp_ref, xn_ref, o_ref, w, scale in ((q_ref, qp_ref, qn_ref, qo_ref, cw_ref[0], A_HEAD_DIM ** -0.5),
                                                       (k_ref, kp_ref, kn_ref, ko_ref, cw_ref[1], None)):
            x = lay.load(x_ref, i)
            prev_row = lay.row(x_ref, i - 1, L - 1) if i > 0 else lay.halo_prev_row(xp_ref) * has_prev
            next_row = lay.row(x_ref, i + 1, 0) if i < lay.group - 1 else lay.halo_next_row(xn_ref) * has_next
            x_dn = jnp.where(row == 0, prev_row, pltpu.roll(x, 1, 0))
            x_up = jnp.where(row == L - 1, next_row, pltpu.roll(x, L - 1, 0))
            y = _silu(w[0:1, :] * x_dn + w[1:2, :] * x + w[2:3, :] * x_up)
            lay.store(o_ref, i, y if scale is None else y * scale)

        u = _gelu_tanh(lay.load(u_ref, i))
        v = _gelu_tanh(lay.load(v_ref, i))
        v = v * _rms_scale(v, B_WIDTH) * ng_ref[...]
        mixed = bias_ref[...]
        for gi in range(B_GROUPS):
            vg = jnp.where(grp == gi, v, 0.0).astype(BF16)
            mixed = mixed + jnp.dot(w_ref[gi], vg, preferred_element_type=F32)
        lay.store(s_ref, i, u * mixed)


def _seq_prep_call(st, conv_w, norm_g, w_s, bias, lay):
    bsz = st["qa"].shape[0]
    lead = lambda b, g: (b,)
    tile = lambda c: lay.spec(c, lead)
    hp, hn = lay.halo_specs(A_PAD)
    full = lambda shp: pl.BlockSpec(shp, lambda b, g: (0,) * len(shp))
    qa, ka, ub, vb = (lay.view(st[n]) for n in ("qa", "ka", "ub", "vb"))
    qo, ko, s = pl.pallas_call(
        functools.partial(_seq_prep_kernel, lay),
        grid=(bsz, lay.ngrp),
        in_specs=[tile(A_PAD), hp, hn, tile(A_PAD), hp, hn, tile(B_WIDTH), tile(B_WIDTH),
                  full((2, 3, A_PAD)), full((1, B_WIDTH)), full((B_GROUPS, SEQ_BLOCK, SEQ_BLOCK)),
                  full((SEQ_BLOCK, B_WIDTH))],
        out_specs=[tile(A_PAD), tile(A_PAD), tile(B_WIDTH)],
        out_shape=[jax.ShapeDtypeStruct(qa.shape, F32), jax.ShapeDtypeStruct(ka.shape, F32),
                   jax.ShapeDtypeStruct(ub.shape, F32)],
        compiler_params=_cparams(2),
        name="seq_prep",
    )(qa, qa, qa, ka, ka, ka, ub, vb, conv_w, norm_g, w_s, bias)
    return lay.unview(qo), lay.unview(ko), lay.unview(s)


def _mlstm_kernel(lay, reverse, q_ref, k_ref, v_ref, g_ref, gb_ref, cmask_ref, cinit_ref, minit_ref,
                  h_ref, cfin_ref, mfin_ref, c_sc, m_sc):
    gidx = pl.program_id(1)
    L = SEQ_BLOCK

    @pl.when(gidx == 0)
    def _():
        c_sc[...] = cinit_ref[...]
        m_sc[...] = minit_ref[...]

    lane = lax.broadcasted_iota(jnp.int32, (1, A_PAD), 1) % HEAD_PAD
    ri = lax.broadcasted_iota(jnp.int32, (L, L), 0)
    ci = lax.broadcasted_iota(jnp.int32, (L, L), 1)
    past = (ci >= ri) if reverse else (ci <= ri)
    last = 0 if reverse else L - 1
    out_lane = lax.broadcasted_iota(jnp.int32, (1, HEAD_PAD), 1)

    def chunk_of_row(bb, i):
        q = lay.load(q_ref, i, (bb,)).astype(BF16)
        k = lay.load(k_ref, i, (bb,)).astype(BF16)
        v = jnp.where(lane == ONES_LANE, 1.0, lay.load(v_ref, i, (bb,)))

        g = lay.load(g_ref, i, (0, bb)) + gb_ref[0]
        logf = _log_sigmoid(g)
        lf_hi, lf_lo = _split_hi_lo(logf)
        cs = jnp.dot(cmask_ref[...], jnp.concatenate([lf_hi, lf_lo], axis=1), preferred_element_type=F32)
        bc = cs[:, 0:LANES] + cs[:, LANES:2 * LANES]
        g_t = g.T
        bc_t = bc.T

        outs = []
        for h in range(A_HEADS):
            sl = slice(h * HEAD_PAD, (h + 1) * HEAD_PAD)
            fcol = A_HEADS + h
            qh, kh, vh = q[:, sl], k[:, sl], v[:, sl]
            b_col = bc[:, fcol:fcol + 1]
            b_row = bc_t[fcol:fcol + 1, :]
            i_col = g[:, h:h + 1]
            i_row = g_t[h:h + 1, :]
            m_prev = m_sc[bb, h:h + 1, 0:1]
            b_last = bc[last:last + 1, fcol:fcol + 1]
            ct = c_sc[bb, h]

            d_log = jnp.where(past, b_col - b_row + i_row, -jnp.inf)
            inter_log = b_col + m_prev
            m_t = jnp.maximum(inter_log, jnp.max(d_log, axis=1, keepdims=True))
            qk = lax.dot_general(qh, kh, (((1,), (1,)), ((), ())), preferred_element_type=F32)
            s = qk * jnp.exp(d_log - m_t)
            inter = jnp.exp(inter_log - m_t)
            num = jnp.dot(s.astype(BF16), vh.astype(BF16), preferred_element_type=F32) \
                + inter * jnp.dot(qh, ct.astype(BF16), preferred_element_type=F32)
            den = num[:, ONES_LANE:ONES_LANE + 1]
            inv = 1.0 / jnp.maximum(jnp.abs(den), jnp.exp(-m_t))
            outs.append(jnp.where(out_lane < A_HEAD_DIM, num * inv, 0.0))

            w_log = b_last - b_col + i_col
            m_new = jnp.maximum(b_last + m_prev, jnp.max(w_log, axis=0, keepdims=True))
            decay = jnp.exp(b_last + m_prev - m_new)
            wv = (jnp.exp(w_log - m_new) * vh).astype(BF16)
            upd = lax.dot_general(kh, wv, (((0,), (0,)), ((), ())), preferred_element_type=F32)
            c_sc[bb, h] = decay * ct + upd
            m_sc[bb, h:h + 1, :] = jnp.broadcast_to(m_new, (1, LANES))
        lay.store(h_ref, i, jnp.concatenate(outs, axis=1), (bb,))

    def chunk(ii, carry):
        i = lay.group - 1 - ii if reverse else ii
        for bb in range(MLSTM_BATCH):
            chunk_of_row(bb, i)
        return carry

    lax.fori_loop(0, lay.group, chunk, 0)

    @pl.when(gidx == lay.ngrp - 1)
    def _():
        cfin_ref[...] = c_sc[...]
        mfin_ref[...] = m_sc[...]


def _mlstm_call(qconv, kconv, st, gate_b, cinit, minit, lay, reverse):
    bsz = qconv.shape[0]
    di = 1 if reverse else 0
    nb = MLSTM_BATCH
    tile = lambda c: lay.spec(c, lambda b, g: (b,), reverse, batch=nb)
    c_spec = pl.BlockSpec((nb, A_HEADS, HEAD_PAD, HEAD_PAD), lambda b, g: (b, 0, 0, 0))
    m_spec = pl.BlockSpec((nb, SUBLANES, LANES), lambda b, g: (b, 0, 0))
    qa, ka, va, ga = lay.view(qconv), lay.view(kconv), lay.view(st["va"]), lay.view(st["ga"])
    cmask, _ = _scan_masks(reverse)
    h, cfin, mfin = pl.pallas_call(
        functools.partial(_mlstm_kernel, lay, reverse),
        grid=(bsz // nb, lay.ngrp),
        in_specs=[tile(A_PAD), tile(A_PAD), tile(A_PAD),
                  lay.spec(LANES, lambda b, g: (di, b), reverse, batch=nb),
                  pl.BlockSpec((1, 1, LANES), lambda b, g: (di, 0, 0)),
                  pl.BlockSpec(cmask.shape, lambda b, g: (0, 0)),
                  c_spec, m_spec],
        out_specs=[tile(A_PAD), c_spec, m_spec],
        out_shape=[jax.ShapeDtypeStruct(qa.shape, F32),
                   jax.ShapeDtypeStruct(cinit.shape, F32),
                   jax.ShapeDtypeStruct(minit.shape, F32)],
        scratch_shapes=[pltpu.VMEM((nb, A_HEADS, HEAD_PAD, HEAD_PAD), F32),
                        pltpu.VMEM((nb, SUBLANES, LANES), F32)],
        compiler_params=_cparams(2),
        name="mlstm_scan_bwd" if reverse else "mlstm_scan_fwd",
    )(qa, ka, va, ga, gate_b, cmask, cinit, minit)
    return lay.unview(h), cfin, mfin


def _gla_kernel(lay, reverse, q_ref, k_ref, v_ref, gk_ref, w2_ref, bgk_ref, cmask_ref, lmask_ref, sinit_ref,
                o_ref, sfin_ref, s_sc, sbd_sc):
    gidx = pl.program_id(1)
    L = SEQ_BLOCK

    @pl.when(gidx == 0)
    def _():
        s_sc[...] = sinit_ref[...]
        sbd_sc[...] = jnp.zeros_like(sbd_sc)
        for bb in range(GLA_BATCH):
            for h in range(C_HEADS):
                sbd_sc[bb, h * CK_PAD:(h + 1) * CK_PAD, h * HEAD_PAD:(h + 1) * HEAD_PAD] = \
                    sinit_ref[bb, h].astype(BF16)

    glane = lax.broadcasted_iota(jnp.int32, (1, LANES), 1)
    row = lax.broadcasted_iota(jnp.int32, (L, 1), 0)
    hlane = glane // CK_PAD
    col = lambda x, h: x[:, (h // 2) * LANES:(h // 2 + 1) * LANES]
    last = 0 if reverse else L - 1
    ones = jnp.ones((L, HEAD_PAD), BF16)

    def rows_from(x, idx_of_row_block, rows_per_block):
        n = L // rows_per_block
        return jnp.concatenate([jnp.broadcast_to(x[idx_of_row_block(i):idx_of_row_block(i) + 1],
                                                 (rows_per_block, x.shape[1])) for i in range(n)], axis=0)

    def chunk_of_row(bb, i):
        g = lay.load(gk_ref, i, (0, bb))
        g_hi = g.astype(BF16).astype(F32)
        g_split = jnp.where(glane < 2 * C_GATE_RANK, g_hi, g - g_hi).astype(BF16)
        gk = jnp.dot(g_split, w2_ref[0], preferred_element_type=F32) + bgk_ref[0]
        la = _log_sigmoid(gk) * (1.0 / C_GATE_NORM)
        la_hi, la_lo = _split_hi_lo(la)
        cs = jnp.dot(cmask_ref[...], jnp.concatenate([la_hi, la_lo], axis=1), preferred_element_type=F32)
        bs = cs[:, 0:CK_W] + cs[:, CK_W:2 * CK_W]
        btot = bs[last:last + 1]

        qs = lay.load(q_ref, i, (bb,)) * (C_HEAD_DK ** -0.5)
        kk = lay.load(k_ref, i, (bb,))
        vb = lay.load(v_ref, i, (bb,)).astype(BF16)
        k_head = [jnp.where(hlane == h % 2, col(kk, h), 0.0) for h in range(C_HEADS)]
        att = [None] * C_HEADS

        def add_level(q_t, k_mul, level, k_rows=None):
            q_b = q_t.astype(BF16)
            for h in range(C_HEADS):
                k_t = k_head[h] if k_mul is None else k_head[h] * col(k_mul, h)
                if k_rows is not None:
                    k_t = jnp.where(k_rows, k_t, 0.0)
                a = lax.dot_general(col(q_b, h), k_t.astype(BF16), (((1,), (1,)), ((), ())),
                                    preferred_element_type=F32)
                if level is not None:
                    a = a * lmask_ref[level]
                att[h] = a if att[h] is None else att[h] + a

        level = 0
        m = L // 2
        while m >= 1:
            pos = row % (2 * m)
            is_q = (pos < m) if reverse else (pos >= m)
            k_rows = None
            if m == 1:
                q_t = jnp.where(is_q, qs * jnp.exp(la), 0.0)
                k_mul = None
                k_rows = jnp.logical_not(is_q)
            else:
                bnd = m if reverse else m - 1
                if 2 * m >= SUBLANES:
                    pref = rows_from(bs, lambda r: r * 2 * m + bnd, 2 * m)
                else:
                    p0 = rows_from(bs, lambda r: r * SUBLANES + bnd, SUBLANES)
                    p1 = rows_from(bs, lambda r: r * SUBLANES + 2 * m + bnd, SUBLANES)
                    pref = jnp.where(row % SUBLANES < 2 * m, p0, p1)
                q_t = qs * jnp.exp(jnp.where(is_q, bs - pref, -jnp.inf))
                k_mul = jnp.exp(jnp.where(is_q, -jnp.inf, pref - bs))
            add_level(q_t, k_mul, None if m == L // 2 else level, k_rows)
            level += 1
            m //= 2
        add_level(qs, None, level)

        o_inter = jnp.dot((qs * jnp.exp(bs)).astype(BF16), sbd_sc[bb], preferred_element_type=F32)
        o_heads = []
        for h in range(C_HEADS):
            cols = slice(h * HEAD_PAD, (h + 1) * HEAD_PAD)
            o_heads.append(o_inter[:, cols] + jnp.dot(att[h].astype(BF16), vb[:, cols],
                                                      preferred_element_type=F32))
        lay.store(o_ref, i, jnp.concatenate(o_heads, axis=1), (bb,))

        ke_t = (kk * jnp.exp(btot - bs)).T.astype(BF16)
        tdot = lambda a: lax.dot_general(a, ones, (((0,), (0,)), ((), ())), preferred_element_type=F32)
        dec_col = jnp.exp(tdot(la_hi) + tdot(la_lo))
        for h in range(C_HEADS):
            rows = slice(h * CK_PAD, (h + 1) * CK_PAD)
            cols = slice(h * HEAD_PAD, (h + 1) * HEAD_PAD)
            upd = jnp.dot(ke_t[rows], vb[:, cols], preferred_element_type=F32)
            s_new = dec_col[rows] * s_sc[bb, h] + upd
            s_sc[bb, h] = s_new
            sbd_sc[bb, rows, cols] = s_new.astype(BF16)

    def chunk(ii, carry):
        i = lay.group - 1 - ii if reverse else ii
        for bb in range(GLA_BATCH):
            chunk_of_row(bb, i)
        return carry

    lax.fori_loop(0, lay.group, chunk, 0)

    @pl.when(gidx == lay.ngrp - 1)
    def _():
        sfin_ref[...] = s_sc[...]


def _scan_masks(reverse):
    t = jnp.arange(SEQ_BLOCK)[:, None]
    u = jnp.arange(SEQ_BLOCK)[None, :]
    cmask = ((u >= t) if reverse else (u <= t)).astype(BF16)
    sizes = []
    m = SEQ_BLOCK
    while m >= 1:
        sizes.append(m)
        m //= 2
    lmask = jnp.stack([(t // sz) == (u // sz) for sz in sizes]).astype(F32)
    return cmask, lmask


def _gla_call(st, w2s, b_gk, sinit, lay, reverse):
    bsz = st["qc"].shape[0]
    di = 1 if reverse else 0
    nb = GLA_BATCH
    tile = lambda c: lay.spec(c, lambda b, g: (b,), reverse, batch=nb)
    full = lambda shp: pl.BlockSpec(shp, lambda b, g: (0,) * len(shp))
    state_spec = pl.BlockSpec((nb, C_HEADS, CK_PAD, HEAD_PAD), lambda b, g: (b, 0, 0, 0))
    qc, kc, vc, gk = (lay.view(st[n]) for n in ("qc", "kc", "vc", "gk"))
    cmask, lmask = _scan_masks(reverse)
    o, sfin = pl.pallas_call(
        functools.partial(_gla_kernel, lay, reverse),
        grid=(bsz // nb, lay.ngrp),
        in_specs=[tile(CK_W), tile(CK_W), tile(CV_W),
                  lay.spec(LANES, lambda b, g: (di, b), reverse, batch=nb),
                  pl.BlockSpec((1, LANES, CK_W), lambda b, g: (di, 0, 0)),
                  pl.BlockSpec((1, 1, CK_W), lambda b, g: (di, 0, 0)),
                  full(cmask.shape), full(lmask.shape), state_spec],
        out_specs=[tile(CV_W), state_spec],
        out_shape=[jax.ShapeDtypeStruct(vc.shape, F32),
                   jax.ShapeDtypeStruct(sinit.shape, F32)],
        scratch_shapes=[pltpu.VMEM((nb, C_HEADS, CK_PAD, HEAD_PAD), F32),
                        pltpu.VMEM((nb, CK_W, CV_W), BF16)],
        compiler_params=_cparams(2),
        name="gla_scan_bwd" if reverse else "gla_scan_fwd",
    )(qc, kc, vc, gk, w2s, b_gk, cmask, lmask, sinit)
    return lay.unview(o), sfin


def _mix_out_kernel(x_ref, mod_ref, haf_ref, hab_ref, hcf_ref, hcb_ref, oa_ref, gc_ref, s_ref, na_ref, nc_ref,
                    w_ref, o_ref):
    def head_norm(xh, gain):
        return xh * _rms_scale(xh, A_HEAD_DIM) * gain

    acc = None
    for j in range(B_WIDTH // LANES):
        r0 = A_PAD + j * LANES
        part = jnp.dot(s_ref[0, j].astype(BF16), w_ref[r0:r0 + LANES, :], preferred_element_type=F32)
        acc = part if acc is None else acc + part
    for h in range(A_HEADS):
        sl = slice(h * HEAD_PAD, (h + 1) * HEAD_PAD)
        a = head_norm(haf_ref[0, h] + hab_ref[0, h], na_ref[:, sl])
        c = head_norm(hcf_ref[0, h] + hcb_ref[0, h], nc_ref[:, sl])
        ah = (a * jax.nn.sigmoid(oa_ref[0, h])).astype(BF16)
        ch = (c * _silu(gc_ref[0, h])).astype(BF16)
        acc = acc + jnp.dot(ah, w_ref[h * HEAD_PAD:(h + 1) * HEAD_PAD, :], preferred_element_type=F32)
        c0 = A_PAD + B_WIDTH + h * HEAD_PAD
        acc = acc + jnp.dot(ch, w_ref[c0:c0 + HEAD_PAD, :], preferred_element_type=F32)
    o_ref[0] = x_ref[0] + mod_ref[0, 2:3, :] * acc


def _mix_out_call(x, mod, haf, hab, hcf, hcb, oa, gc, s, na, nc, w, tm):
    bsz, t, d = x.shape
    mod_map = (lambda b, i: (b, 0, 0)) if mod.shape[0] > 1 else (lambda b, i: (0, 0, 0))
    row = lambda c: pl.BlockSpec((1, tm, c), lambda b, i: (b, i, 0))
    chunked = lambda c: pl.BlockSpec((1, c // LANES, tm, LANES), lambda b, i: (b, 0, i, 0))
    full = lambda shp: pl.BlockSpec(shp, lambda b, i: (0,) * len(shp))
    return pl.pallas_call(
        _mix_out_kernel,
        grid=(bsz, t // tm),
        in_specs=[row(d), pl.BlockSpec((1, 6, d), mod_map), chunked(A_PAD), chunked(A_PAD), chunked(CV_W),
                  chunked(CV_W), chunked(A_PAD), chunked(CV_W), chunked(B_WIDTH),
                  full((1, A_PAD)), full((1, CV_W)), _resident(w.shape, 2)],
        out_specs=row(d),
        out_shape=jax.ShapeDtypeStruct(x.shape, F32),
        compiler_params=_cparams(2),
        name="mix_out",
    )(x, mod, haf, hab, hcf, hcb, oa, gc, s, na, nc, w)


def _ffn_kernel(nk, final_norm, x_ref, mod_ref, g_ref, wg_ref, wu_ref, wo_ref, fg_ref, o_ref, h_sc, acc_sc):
    kk = pl.program_id(2)

    @pl.when(kk == 0)
    def _():
        x = x_ref[0]
        y = x * _rms_scale(x, x.shape[-1]) * g_ref[...]
        h_sc[...] = (y * (1.0 + mod_ref[0, 4:5, :]) + mod_ref[0, 3:4, :]).astype(BF16)
        acc_sc[...] = jnp.zeros_like(acc_sc)

    h = h_sc[...]
    gate = jnp.dot(h, wg_ref[...], preferred_element_type=F32)
    up = jnp.dot(h, wu_ref[...], preferred_element_type=F32)
    act = (_silu(gate) * up).astype(BF16)
    acc_sc[...] += jnp.dot(act, wo_ref[...], preferred_element_type=F32)

    @pl.when(kk == nk - 1)
    def _():
        y = x_ref[0] + mod_ref[0, 5:6, :] * acc_sc[...]
        if final_norm:
            y = y * _rms_scale(y, y.shape[-1]) * fg_ref[...]
        o_ref[0] = y


def _ffn_call(x, mod, g, w_in, w_out, final_g, final_norm, tm, nk):
    bsz, t, d = x.shape
    d_ff = w_out.shape[0]
    tk = d_ff // nk
    wmode = {"pipeline_mode": pl.Buffered(1)} if nk == 1 else {}
    mod_map = (lambda b, i, k: (b, 0, 0)) if mod.shape[0] > 1 else (lambda b, i, k: (0, 0, 0))
    return pl.pallas_call(
        functools.partial(_ffn_kernel, nk, final_norm),
        grid=(bsz, t // tm, nk),
        in_specs=[pl.BlockSpec((1, tm, d), lambda b, i, k: (b, i, 0)),
                  pl.BlockSpec((1, 6, d), mod_map),
                  pl.BlockSpec((1, d), lambda b, i, k: (0, 0)),
                  pl.BlockSpec((d, tk), lambda b, i, k: (0, k), **wmode),
                  pl.BlockSpec((d, tk), lambda b, i, k: (0, nk + k), **wmode),
                  pl.BlockSpec((tk, d), lambda b, i, k: (k, 0), **wmode),
                  pl.BlockSpec((1, d), lambda b, i, k: (0, 0))],
        out_specs=pl.BlockSpec((1, tm, d), lambda b, i, k: (b, i, 0)),
        out_shape=jax.ShapeDtypeStruct(x.shape, F32),
        scratch_shapes=[pltpu.VMEM((tm, d), BF16), pltpu.VMEM((tm, d), F32)],
        compiler_params=_cparams(3),
        name="ffn",
    )(x, mod, g, w_in, w_in, w_out, final_g)


def _pad_heads(w, n_heads, dim, pad, axis=-1):
    w = jnp.moveaxis(w, axis, -1)
    lead = w.shape[:-1]
    w = w.reshape(lead + (n_heads, dim))
    w = jnp.pad(w, [(0, 0)] * len(lead) + [(0, 0), (0, pad - dim)])
    return jnp.moveaxis(w.reshape(lead + (n_heads * pad,)), -1, axis)


def _pad_to(w, width):
    return jnp.pad(w, [(0, 0)] * (w.ndim - 1) + [(0, width - w.shape[-1])])


def _prep_w_in(w_in):
    a_w = A_HEADS * A_HEAD_DIM
    ck = C_HEADS * C_HEAD_DK
    cv = C_HEADS * C_HEAD_DV
    sizes = (a_w, a_w, a_w, a_w, N_DIR * 2 * A_HEADS, B_WIDTH, B_WIDTH, ck, ck, cv, cv, N_DIR * C_GATE_RANK)
    pts = [sum(sizes[:i + 1]) for i in range(len(sizes) - 1)]
    qa, ka, va, oa, ga, ub, vb, qc, kc, vc, gc, gkc = jnp.split(w_in, pts, axis=-1)
    pa = lambda w: _pad_heads(w, A_HEADS, A_HEAD_DIM, HEAD_PAD)
    pk = lambda w: _pad_heads(w, C_HEADS, C_HEAD_DK, CK_PAD)
    pv = lambda w: _pad_heads(w, C_HEADS, C_HEAD_DV, HEAD_PAD)
    ng = 2 * A_HEADS
    gk_dir = lambda w: _pad_to(jnp.tile(w, (1, GK_COPIES)), LANES)
    cols = [pa(qa), pa(ka), pa(va), pa(oa),
            _pad_to(ga[:, :ng], LANES), _pad_to(ga[:, ng:], LANES),
            ub, vb, pk(qc), pk(kc), pv(vc), pv(gc),
            gk_dir(gkc[:, :C_GATE_RANK]), gk_dir(gkc[:, C_GATE_RANK:])]
    return jnp.concatenate(cols, axis=-1).astype(BF16)


def _prep_w_gk2(w_gk2):
    w = _pad_heads(w_gk2, C_HEADS, C_HEAD_DK, CK_PAD)
    hi = w.astype(BF16)
    lo = (w - hi.astype(F32)).astype(BF16)
    stacked = jnp.concatenate([hi, lo, hi], axis=1)
    return jnp.pad(stacked, ((0, 0), (0, LANES - GK_COPIES * C_GATE_RANK), (0, 0)))


def _prep_w_out(w_out):
    a_w = A_HEADS * A_HEAD_DIM
    wa, wb, wc = w_out[:a_w], w_out[a_w:a_w + B_WIDTH], w_out[a_w + B_WIDTH:]
    wa = _pad_heads(wa, A_HEADS, A_HEAD_DIM, HEAD_PAD, axis=0)
    wc = _pad_heads(wc, C_HEADS, C_HEAD_DV, HEAD_PAD, axis=0)
    return jnp.concatenate([wa, wb, wc], axis=0).astype(BF16)


def _mixer_scans(st, lw, a_state, c_state, lay):
    qconv, kconv, s = _seq_prep_call(st, lw["conv"], lw["sgu_g"], lw["sgu_w"], lw["sgu_bias"], lay)
    ha, a_fin, hc, c_fin = [], [], [], []
    for di, reverse in enumerate((False, True)):
        h, cfin, mfin = _mlstm_call(qconv, kconv, st, lw["gate_b"], a_state[di][0], a_state[di][1], lay, reverse)
        ha.append(h)
        a_fin.append((cfin, mfin))
        o, sfin = _gla_call(st, lw["w_gk2"], lw["b_gk"], c_state[di], lay, reverse)
        hc.append(o)
        c_fin.append(sfin)
    return ha, hc, s, a_fin, c_fin


def kernel(x, c, ctx, c_ctx, norm1_g, norm2_g, w_ada, b_ada, w_in, mlstm_conv, mlstm_gate_b,
           mlstm_norm_g, gla_w_gk2, gla_b_gk, gla_norm_g, sgu_norm_g, sgu_w, sgu_b, w_out,
           w_ffn_in, w_ffn_out, final_g):
    bsz, seq, d = x.shape
    ctx_len = ctx.shape[1]
    depth = w_in.shape[0]
    n_cond = 2 * SUBLANES
    cond = jnp.zeros((n_cond, d), F32).at[:bsz].set(c).at[bsz].set(c_ctx)
    mods = _ada_call(cond, w_ada, b_ada).reshape(depth, n_cond, 6, d)

    fg = final_g.reshape(1, d)
    x_lat, x_ctx = x, ctx
    for l in range(depth):
        need_ctx = l < depth - 1
        mod_lat = mods[l, :bsz]
        mod_ctx = mods[l, bsz:bsz + 1]
        gb = mlstm_gate_b[l].reshape(N_DIR, 1, 2 * A_HEADS)
        lw = {
            "conv": jnp.stack([_pad_heads(mlstm_conv[l][:, :A_HEADS * A_HEAD_DIM], A_HEADS, A_HEAD_DIM, HEAD_PAD),
                               _pad_heads(mlstm_conv[l][:, A_HEADS * A_HEAD_DIM:], A_HEADS, A_HEAD_DIM, HEAD_PAD)]),
            "gate_b": _pad_to(gb, LANES),
            "w_gk2": _prep_w_gk2(gla_w_gk2[l]),
            "b_gk": _pad_heads(gla_b_gk[l], C_HEADS, C_HEAD_DK, CK_PAD).reshape(N_DIR, 1, CK_W),
            "sgu_g": sgu_norm_g[l].reshape(1, B_WIDTH),
            "sgu_w": sgu_w[l].astype(BF16),
            "sgu_bias": jnp.repeat(sgu_b[l].T, B_GROUP_DIM, axis=1),
        }
        w_in_l = _prep_w_in(w_in[l])
        w_out_l = _prep_w_out(w_out[l])
        na = _pad_heads(mlstm_norm_g[l], A_HEADS, A_HEAD_DIM, HEAD_PAD).reshape(1, A_PAD)
        nc = _pad_heads(jnp.tile(gla_norm_g[l], C_HEADS), C_HEADS, C_HEAD_DV, HEAD_PAD).reshape(1, CV_W)
        g1 = norm1_g[l].reshape(1, d)
        g2 = norm2_g[l].reshape(1, d)
        w_ffn_in_l = w_ffn_in[l].astype(BF16)
        w_ffn_out_l = w_ffn_out[l].astype(BF16)

        st_ctx = _proj_in_call(x_ctx, mod_ctx, g1, w_in_l, tm=_row_tile(ctx_len, PROJ_ROWS))
        st_lat = _proj_in_call(x_lat, mod_lat, g1, w_in_l, tm=_row_tile(seq, PROJ_ROWS))

        a0 = (jnp.zeros((bsz, A_HEADS, HEAD_PAD, HEAD_PAD), F32), jnp.zeros((bsz, SUBLANES, LANES), F32))
        c0 = jnp.zeros((bsz, C_HEADS, CK_PAD, HEAD_PAD), F32)
        lay_ctx = _SeqLayout(ctx_len, False)
        lay_lat = _SeqLayout(seq, l % 2 == 1)
        ha_c, hc_c, s_c, a_state, c_state = _mixer_scans(st_ctx, lw, (a0, a0), (c0, c0), lay_ctx)
        ha_l, hc_l, s_l, _, _ = _mixer_scans(st_lat, lw, a_state, c_state, lay_lat)

        x_lat = _mix_out_call(x_lat, mod_lat, ha_l[0], ha_l[1], hc_l[0], hc_l[1], st_lat["oa"], st_lat["gc"], s_l,
                              na, nc, w_out_l, tm=_row_tile(seq, MIX_ROWS))
        x_lat = _ffn_call(x_lat, mod_lat, g2, w_ffn_in_l, w_ffn_out_l, fg, not need_ctx,
                          tm=_row_tile(seq, FFN_ROWS), nk=FFN_SPLIT)
        if need_ctx:
            x_ctx = _mix_out_call(x_ctx, mod_ctx, ha_c[0], ha_c[1], hc_c[0], hc_c[1], st_ctx["oa"], st_ctx["gc"], s_c,
                                  na, nc, w_out_l, tm=_row_tile(ctx_len, MIX_ROWS))
            x_ctx = _ffn_call(x_ctx, mod_ctx, g2, w_ffn_in_l, w_ffn_out_l, fg, False,
                              tm=_row_tile(ctx_len, FFN_ROWS), nk=FFN_SPLIT)
    return x_lat
```

```python
import functools
import math

import jax
import jax.numpy as jnp
from jax import lax
from jax.experimental import pallas as pl
from jax.experimental.pallas import tpu as pltpu

F32 = jnp.float32
BF16 = jnp.bfloat16
HIGHEST = lax.Precision.HIGHEST

LANES = 128
SUBLANES = 8
VMEM_LIMIT_BYTES = 56 * 1024 * 1024

GRID_W = 64
RMS_EPS = 1e-6
A_HEADS = 4
A_HEAD_DIM = 96
B_GROUPS = 4
B_GROUP_DIM = 64
B_WIDTH = B_GROUPS * B_GROUP_DIM
C_HEADS = 4
C_HEAD_DK = 48
C_HEAD_DV = 96
C_GATE_RANK = 16
C_GATE_NORM = 16.0
N_DIR = 2

HEAD_PAD = LANES
A_PAD = A_HEADS * HEAD_PAD
CK_PAD = 64
CK_W = C_HEADS * CK_PAD
CV_W = C_HEADS * HEAD_PAD
ONES_LANE = A_HEAD_DIM

SEQ_BLOCK = 128
SEQ_GROUP = SUBLANES
MLSTM_BATCH = 2
GLA_BATCH = 2
GK_COPIES = 3

PROJ_ROWS = 512
MIX_ROWS = 512
FFN_ROWS = 512
FFN_SPLIT = 1

_IN_LAYOUT = (("qa", A_PAD, 1), ("ka", A_PAD, 1), ("va", A_PAD, 1), ("oa", A_PAD, 1),
              ("ga", LANES, N_DIR),
              ("ub", B_WIDTH, 1), ("vb", B_WIDTH, 1),
              ("qc", CK_W, 1), ("kc", CK_W, 1), ("vc", CV_W, 1), ("gc", CV_W, 1),
              ("gk", LANES, N_DIR))
N_IN_PAD = sum(w * n for _, w, n in _IN_LAYOUT)


def _cparams(n_axes):
    return pltpu.CompilerParams(dimension_semantics=("arbitrary",) * n_axes,
                                vmem_limit_bytes=VMEM_LIMIT_BYTES)


def _resident(shape, n_grid):
    zeros = (0,) * len(shape)
    maps = {2: lambda a, b: zeros, 3: lambda a, b, c: zeros}
    return pl.BlockSpec(shape, maps[n_grid], pipeline_mode=pl.Buffered(1))


def _row_tile(t_len, want):
    return min(want, t_len)


def _log_sigmoid(x):
    return jnp.minimum(x, 0.0) - jnp.log(1.0 + jnp.exp(-jnp.abs(x)))


def _silu(x):
    return x * jax.nn.sigmoid(x)


def _gelu_tanh(x):
    c = math.sqrt(2.0 / math.pi)
    return 0.5 * x * (1.0 + jnp.tanh(c * (x + 0.044715 * (x * x * x))))


def _rms_scale(x, n):
    return lax.rsqrt(jnp.sum(x * x, axis=-1, keepdims=True) * (1.0 / n) + RMS_EPS)


def _split_hi_lo(x):
    hi = x.astype(BF16)
    lo = (x - hi.astype(F32)).astype(BF16)
    return hi, lo


def _ada_kernel(s_ref, w_ref, b_ref, o_ref):
    s = _silu(s_ref[...])
    o_ref[0] = jnp.dot(s, w_ref[0], precision=HIGHEST, preferred_element_type=F32) + b_ref[0]


def _ada_call(cond, w_ada, b_ada):
    depth, d, n6 = w_ada.shape
    nb = cond.shape[0]
    tn = 1536
    return pl.pallas_call(
        _ada_kernel,
        grid=(depth, n6 // tn),
        in_specs=[pl.BlockSpec((nb, d), lambda l, j: (0, 0)),
                  pl.BlockSpec((1, d, tn), lambda l, j: (l, 0, j)),
                  pl.BlockSpec((1, 1, tn), lambda l, j: (l, 0, j))],
        out_specs=pl.BlockSpec((1, nb, tn), lambda l, j: (l, 0, j)),
        out_shape=jax.ShapeDtypeStruct((depth, nb, n6), F32),
        compiler_params=_cparams(2),
        name="ada_mod",
    )(cond, w_ada, b_ada.reshape(depth, 1, n6))


def _proj_in_kernel(x_ref, mod_ref, g_ref, w_ref, *out_refs):
    x = x_ref[0]
    y = x * _rms_scale(x, x.shape[-1]) * g_ref[...]
    h = (y * (1.0 + mod_ref[0, 1:2, :]) + mod_ref[0, 0:1, :]).astype(BF16)
    off = 0
    for (_, width, n_dir), o_ref in zip(_IN_LAYOUT, out_refs):
        for di in range(n_dir):
            res = jnp.dot(h, w_ref[:, off:off + width], preferred_element_type=F32)
            for j in range(width // LANES):
                piece = res[:, j * LANES:(j + 1) * LANES]
                if n_dir == 1:
                    o_ref[0, j] = piece
                else:
                    o_ref[di, 0, j] = piece
            off += width


def _proj_in_call(x, mod, g, w, tm):
    bsz, t, d = x.shape
    mod_b = mod.shape[0]
    mod_map = (lambda b, i: (b, 0, 0)) if mod_b > 1 else (lambda b, i: (0, 0, 0))
    out_shapes, out_specs = [], []
    for _, wd, n_dir in _IN_LAYOUT:
        nch = wd // LANES
        if n_dir == 1:
            out_shapes.append(jax.ShapeDtypeStruct((bsz, nch, t, LANES), F32))
            out_specs.append(pl.BlockSpec((1, nch, tm, LANES), lambda b, i: (b, 0, i, 0)))
        else:
            out_shapes.append(jax.ShapeDtypeStruct((n_dir, bsz, nch, t, LANES), F32))
            out_specs.append(pl.BlockSpec((n_dir, 1, nch, tm, LANES), lambda b, i: (0, b, 0, i, 0)))
    outs = pl.pallas_call(
        _proj_in_kernel,
        grid=(bsz, t // tm),
        in_specs=[pl.BlockSpec((1, tm, d), lambda b, i: (b, i, 0)),
                  pl.BlockSpec((1, 6, d), mod_map),
                  pl.BlockSpec((1, d), lambda b, i: (0, 0)),
                  _resident((d, N_IN_PAD), 2)],
        out_specs=out_specs,
        out_shape=out_shapes,
        compiler_params=_cparams(2),
        name="proj_in",
    )(x, mod, g, w)
    return dict(zip([n for n, _, _ in _IN_LAYOUT], outs))


class _SeqLayout:
    def __init__(self, t_len, column_major):
        self.cm = column_major
        self.t_len = t_len
        self.nblk = t_len // SEQ_BLOCK
        self.group = min(SEQ_GROUP, self.nblk)
        self.ngrp = self.nblk // self.group
        if column_major:
            assert t_len == SEQ_BLOCK * GRID_W and self.group == SEQ_GROUP

    def view(self, a):
        if not self.cm:
            return a
        return a.reshape(a.shape[:-2] + (SEQ_BLOCK, GRID_W, LANES))

    def unview(self, a):
        if not self.cm:
            return a
        return a.reshape(a.shape[:-3] + (self.t_len, LANES))

    def spec(self, c, lead_map, reverse=False, batch=1):
        def grp(g):
            return self.ngrp - 1 - g if reverse else g

        n_lead = len(lead_map(0, 0))
        lead_blk = (1,) * (n_lead - 1) + (batch,)
        nch = c // LANES
        if self.cm:
            return pl.BlockSpec(lead_blk + (nch, SEQ_BLOCK, self.group, LANES),
                                lambda b, g: lead_map(b, g) + (0, 0, grp(g), 0))
        return pl.BlockSpec(lead_blk + (nch, self.group * SEQ_BLOCK, LANES),
                            lambda b, g: lead_map(b, g) + (0, grp(g), 0))

    def halo_specs(self, c):
        nch = c // LANES
        if self.cm:
            shp = (1, nch, SUBLANES, self.group, LANES)
            prev = pl.BlockSpec(shp, lambda b, g: (b, 0, SEQ_BLOCK // SUBLANES - 1, jnp.maximum(g - 1, 0), 0))
            nxt = pl.BlockSpec(shp, lambda b, g: (b, 0, 0, jnp.minimum(g + 1, self.ngrp - 1), 0))
        else:
            per = self.group * SEQ_BLOCK // SUBLANES
            last = self.t_len // SUBLANES - 1
            shp = (1, nch, SUBLANES, LANES)
            prev = pl.BlockSpec(shp, lambda b, g: (b, 0, jnp.maximum(g * per - 1, 0), 0))
            nxt = pl.BlockSpec(shp, lambda b, g: (b, 0, jnp.minimum((g + 1) * per, last), 0))
        return prev, nxt

    @staticmethod
    def _cat(pieces):
        return pieces[0] if len(pieces) == 1 else jnp.concatenate(pieces, axis=1)

    def halo_prev_row(self, ref):
        r = SUBLANES - 1
        if self.cm:
            return self._cat([ref[0, j, r:r + 1, self.group - 1, :] for j in range(ref.shape[1])])
        return self._cat([ref[0, j, r:r + 1, :] for j in range(ref.shape[1])])

    def halo_next_row(self, ref):
        if self.cm:
            return self._cat([ref[0, j, 0:1, 0, :] for j in range(ref.shape[1])])
        return self._cat([ref[0, j, 0:1, :] for j in range(ref.shape[1])])

    def _piece(self, ref, lead, j, i):
        if self.cm:
            flat = ref.at[lead + (j,)].reshape(self.group * SEQ_BLOCK, LANES)
            return flat, (pl.ds(i, SEQ_BLOCK, stride=self.group), slice(None))
        start = i * SEQ_BLOCK
        if not isinstance(i, int):
            start = pl.multiple_of(start, SEQ_BLOCK)
        return ref, lead + (j, pl.ds(start, SEQ_BLOCK), slice(None))

    def load(self, ref, i, lead=(0,)):
        pieces = []
        for j in range(ref.shape[len(lead)]):
            r, idx = self._piece(ref, lead, j, i)
            pieces.append(r[idx])
        return self._cat(pieces)

    def store(self, ref, i, val, lead=(0,)):
        for j in range(ref.shape[len(lead)]):
            r, idx = self._piece(ref, lead, j, i)
            r[idx] = val[:, j * LANES:(j + 1) * LANES]

    def row(self, ref, i, r):
        if self.cm:
            return self._cat([ref[0, j, r:r + 1, i, :] for j in range(ref.shape[1])])
        t = i * SEQ_BLOCK + r
        return self._cat([ref[0, j, t:t + 1, :] for j in range(ref.shape[1])])


def _seq_prep_kernel(lay, q_ref, qp_ref, qn_ref, k_ref, kp_ref, kn_ref, u_ref, v_ref, cw_ref, ng_ref, w_ref,
                     bias_ref, qo_ref, ko_ref, s_ref):
    g = pl.program_id(1)
    L = SEQ_BLOCK
    row = lax.broadcasted_iota(jnp.int32, (L, 1), 0)
    has_prev = (g > 0).astype(F32)
    has_next = (g < lay.ngrp - 1).astype(F32)
    grp = lax.broadcasted_iota(jnp.int32, (1, B_WIDTH), 1) // B_GROUP_DIM

    for i in range(lay.group):
        for x_ref, xp_ref, xn_ref, o_ref, w, scale in ((q_ref, qp_ref, qn_ref, qo_ref, cw_ref[0], A_HEAD_DIM ** -0.5),
                                                       (k_ref, kp_ref, kn_ref, ko_ref, cw_ref[1], None)):
            x = lay.load(x_ref, i)
            prev_row = lay.row(x_ref, i - 1, L - 1) if i > 0 else lay.halo_prev_row(xp_ref) * has_prev
            next_row = lay.row(x_ref, i + 1, 0) if i < lay.group - 1 else lay.halo_next_row(xn_ref) * has_next
            x_dn = jnp.where(row == 0, prev_row, pltpu.roll(x, 1, 0))
            x_up = jnp.where(row == L - 1, next_row, pltpu.roll(x, L - 1, 0))
            y = _silu(w[0:1, :] * x_dn + w[1:2, :] * x + w[2:3, :] * x_up)
            lay.store(o_ref, i, y if scale is None else y * scale)

        u = _gelu_tanh(lay.load(u_ref, i))
        v = _gelu_tanh(lay.load(v_ref, i))
        v = v * _rms_scale(v, B_WIDTH) * ng_ref[...]
        mixed = bias_ref[...]
        for gi in range(B_GROUPS):
            vg = jnp.where(grp == gi, v, 0.0).astype(BF16)
            mixed = mixed + jnp.dot(w_ref[gi], vg, preferred_element_type=F32)
        lay.store(s_ref, i, u * mixed)


def _seq_prep_call(st, conv_w, norm_g, w_s, bias, lay):
    bsz = st["qa"].shape[0]
    lead = lambda b, g: (b,)
    tile = lambda c: lay.spec(c, lead)
    hp, hn = lay.halo_specs(A_PAD)
    full = lambda shp: pl.BlockSpec(shp, lambda b, g: (0,) * len(shp))
    qa, ka, ub, vb = (lay.view(st[n]) for n in ("qa", "ka", "ub", "vb"))
    qo, ko, s = pl.pallas_call(
        functools.partial(_seq_prep_kernel, lay),
        grid=(bsz, lay.ngrp),
        in_specs=[tile(A_PAD), hp, hn, tile(A_PAD), hp, hn, tile(B_WIDTH), tile(B_WIDTH),
                  full((2, 3, A_PAD)), full((1, B_WIDTH)), full((B_GROUPS, SEQ_BLOCK, SEQ_BLOCK)),
                  full((SEQ_BLOCK, B_WIDTH))],
        out_specs=[tile(A_PAD), tile(A_PAD), tile(B_WIDTH)],
        out_shape=[jax.ShapeDtypeStruct(qa.shape, F32), jax.ShapeDtypeStruct(ka.shape, F32),
                   jax.ShapeDtypeStruct(ub.shape, F32)],
        compiler_params=_cparams(2),
        name="seq_prep",
    )(qa, qa, qa, ka, ka, ka, ub, vb, conv_w, norm_g, w_s, bias)
    return lay.unview(qo), lay.unview(ko), lay.unview(s)


def _mlstm_kernel(lay, reverse, q_ref, k_ref, v_ref, g_ref, gb_ref, cmask_ref, cinit_ref, minit_ref,
                  h_ref, cfin_ref, mfin_ref, c_sc, m_sc):
    gidx = pl.program_id(1)
    L = SEQ_BLOCK

    @pl.when(gidx == 0)
    def _():
        c_sc[...] = cinit_ref[...]
        m_sc[...] = minit_ref[...]

    lane = lax.broadcasted_iota(jnp.int32, (1, A_PAD), 1) % HEAD_PAD
    ri = lax.broadcasted_iota(jnp.int32, (L, L), 0)
    ci = lax.broadcasted_iota(jnp.int32, (L, L), 1)
    past = (ci >= ri) if reverse else (ci <= ri)
    last = 0 if reverse else L - 1
    out_lane = lax.broadcasted_iota(jnp.int32, (1, HEAD_PAD), 1)

    def chunk(ii, carry):
        i = lay.group - 1 - ii if reverse else ii
        rows = range(MLSTM_BATCH)
        q = [lay.load(q_ref, i, (bb,)).astype(BF16) for bb in rows]
        k = [lay.load(k_ref, i, (bb,)).astype(BF16) for bb in rows]
        v = [jnp.where(lane == ONES_LANE, 1.0, lay.load(v_ref, i, (bb,))) for bb in rows]
        g = [lay.load(g_ref, i, (0, bb)) + gb_ref[0] for bb in rows]
        lf = [_split_hi_lo(_log_sigmoid(g[bb])) for bb in rows]
        cs = [jnp.dot(cmask_ref[...], jnp.concatenate(lf[bb], axis=1), preferred_element_type=F32) for bb in rows]
        bc = [cs[bb][:, 0:LANES] + cs[bb][:, LANES:2 * LANES] for bb in rows]
        g_t = [g[bb].T for bb in rows]
        bc_t = [bc[bb].T for bb in rows]

        units = [(bb, h) for bb in rows for h in range(A_HEADS)]
        each = lambda f: {u: f(*u) for u in units}
        sl = lambda h: slice(h * HEAD_PAD, (h + 1) * HEAD_PAD)
        fc = lambda h: slice(A_HEADS + h, A_HEADS + h + 1)
        b_col = each(lambda bb, h: bc[bb][:, fc(h)])
        i_col = each(lambda bb, h: g[bb][:, h:h + 1])
        m_prev = each(lambda bb, h: m_sc[bb, h:h + 1, 0:1])
        b_last = each(lambda bb, h: bc[bb][last:last + 1, fc(h)])
        ct = each(lambda bb, h: c_sc[bb, h])
        c_row = each(lambda bb, h: g_t[bb][h:h + 1, :] - bc_t[bb][fc(h), :])
        u_col = each(lambda bb, h: jnp.maximum(
            m_prev[bb, h], jnp.max(jnp.where(past, c_row[bb, h], -jnp.inf), axis=1, keepdims=True)))
        qk = each(lambda bb, h: lax.dot_general(q[bb][:, sl(h)], k[bb][:, sl(h)], (((1,), (1,)), ((), ())),
                                                preferred_element_type=F32))
        s = each(lambda bb, h: (qk[bb, h] * jnp.exp(jnp.where(past, c_row[bb, h] - u_col[bb, h], -jnp.inf))
                                ).astype(BF16))
        inter = each(lambda bb, h: jnp.exp(m_prev[bb, h] - u_col[bb, h]))
        carry_in = each(lambda bb, h: jnp.dot(q[bb][:, sl(h)], ct[bb, h].astype(BF16), preferred_element_type=F32))
        num = each(lambda bb, h: jnp.dot(s[bb, h], v[bb][:, sl(h)].astype(BF16), preferred_element_type=F32)
                   + inter[bb, h] * carry_in[bb, h])
        inv = each(lambda bb, h: 1.0 / jnp.maximum(jnp.abs(num[bb, h][:, ONES_LANE:ONES_LANE + 1]),
                                                   jnp.exp(-(b_col[bb, h] + u_col[bb, h]))))
        outs = each(lambda bb, h: jnp.where(out_lane < A_HEAD_DIM, num[bb, h] * inv[bb, h], 0.0))
        for bb in rows:
            lay.store(h_ref, i, jnp.concatenate([outs[bb, h] for h in range(A_HEADS)], axis=1), (bb,))

        w_log = each(lambda bb, h: b_last[bb, h] - b_col[bb, h] + i_col[bb, h])
        m_new = each(lambda bb, h: jnp.maximum(b_last[bb, h] + m_prev[bb, h],
                                               jnp.max(w_log[bb, h], axis=0, keepdims=True)))
        wv = each(lambda bb, h: (jnp.exp(w_log[bb, h] - m_new[bb, h]) * v[bb][:, sl(h)]).astype(BF16))
        for bb, h in units:
            upd = lax.dot_general(k[bb][:, sl(h)], wv[bb, h], (((0,), (0,)), ((), ())), preferred_element_type=F32)
            c_sc[bb, h] = jnp.exp(b_last[bb, h] + m_prev[bb, h] - m_new[bb, h]) * ct[bb, h] + upd
            m_sc[bb, h:h + 1, :] = jnp.broadcast_to(m_new[bb, h], (1, LANES))
        return carry

    lax.fori_loop(0, lay.group, chunk, 0)

    @pl.when(gidx == lay.ngrp - 1)
    def _():
        cfin_ref[...] = c_sc[...]
        mfin_ref[...] = m_sc[...]


def _mlstm_call(qconv, kconv, st, gate_b, cinit, minit, lay, reverse):
    bsz = qconv.shape[0]
    di = 1 if reverse else 0
    nb = MLSTM_BATCH
    tile = lambda c: lay.spec(c, lambda b, g: (b,), reverse, batch=nb)
    c_spec = pl.BlockSpec((nb, A_HEADS, HEAD_PAD, HEAD_PAD), lambda b, g: (b, 0, 0, 0))
    m_spec = pl.BlockSpec((nb, SUBLANES, LANES), lambda b, g: (b, 0, 0))
    qa, ka, va, ga = lay.view(qconv), lay.view(kconv), lay.view(st["va"]), lay.view(st["ga"])
    cmask, _ = _scan_masks(reverse)
    h, cfin, mfin = pl.pallas_call(
        functools.partial(_mlstm_kernel, lay, reverse),
        grid=(bsz // nb, lay.ngrp),
        in_specs=[tile(A_PAD), tile(A_PAD), tile(A_PAD),
                  lay.spec(LANES, lambda b, g: (di, b), reverse, batch=nb),
                  pl.BlockSpec((1, 1, LANES), lambda b, g: (di, 0, 0)),
                  pl.BlockSpec(cmask.shape, lambda b, g: (0, 0)),
                  c_spec, m_spec],
        out_specs=[tile(A_PAD), c_spec, m_spec],
        out_shape=[jax.ShapeDtypeStruct(qa.shape, F32),
                   jax.ShapeDtypeStruct(cinit.shape, F32),
                   jax.ShapeDtypeStruct(minit.shape, F32)],
        scratch_shapes=[pltpu.VMEM((nb, A_HEADS, HEAD_PAD, HEAD_PAD), F32),
                        pltpu.VMEM((nb, SUBLANES, LANES), F32)],
        compiler_params=_cparams(2),
        name="mlstm_scan_bwd" if reverse else "mlstm_scan_fwd",
    )(qa, ka, va, ga, gate_b, cmask, cinit, minit)
    return lay.unview(h), cfin, mfin


def _gla_kernel(lay, reverse, q_ref, k_ref, v_ref, gk_ref, w2_ref, bgk_ref, cmask_ref, lmask_ref, sinit_ref,
                o_ref, sfin_ref, s_sc, sbd_sc):
    gidx = pl.program_id(1)
    L = SEQ_BLOCK

    @pl.when(gidx == 0)
    def _():
        s_sc[...] = sinit_ref[...]
        sbd_sc[...] = jnp.zeros_like(sbd_sc)
        for bb in range(GLA_BATCH):
            for h in range(C_HEADS):
                sbd_sc[bb, h * CK_PAD:(h + 1) * CK_PAD, h * HEAD_PAD:(h + 1) * HEAD_PAD] = \
                    sinit_ref[bb, h].astype(BF16)

    glane = lax.broadcasted_iota(jnp.int32, (1, LANES), 1)
    row = lax.broadcasted_iota(jnp.int32, (L, 1), 0)
    hlane = glane // CK_PAD
    col = lambda x, h: x[:, (h // 2) * LANES:(h // 2 + 1) * LANES]
    last = 0 if reverse else L - 1
    ones = jnp.ones((L, HEAD_PAD), BF16)

    def rows_from(x, idx_of_row_block, rows_per_block):
        n = L // rows_per_block
        return jnp.concatenate([jnp.broadcast_to(x[idx_of_row_block(i):idx_of_row_block(i) + 1],
                                                 (rows_per_block, x.shape[1])) for i in range(n)], axis=0)

    def chunk_of_row(bb, i):
        g = lay.load(gk_ref, i, (0, bb))
        g_hi = g.astype(BF16).astype(F32)
        g_split = jnp.where(glane < 2 * C_GATE_RANK, g_hi, g - g_hi).astype(BF16)
        gk = jnp.dot(g_split, w2_ref[0], preferred_element_type=F32) + bgk_ref[0]
        la = _log_sigmoid(gk) * (1.0 / C_GATE_NORM)
        la_hi, la_lo = _split_hi_lo(la)
        cs = jnp.dot(cmask_ref[...], jnp.concatenate([la_hi, la_lo], axis=1), preferred_element_type=F32)
        bs = cs[:, 0:CK_W] + cs[:, CK_W:2 * CK_W]
        btot = bs[last:last + 1]

        qs = lay.load(q_ref, i, (bb,)) * (C_HEAD_DK ** -0.5)
        kk = lay.load(k_ref, i, (bb,))
        vb = lay.load(v_ref, i, (bb,)).astype(BF16)
        k_head = [jnp.where(hlane == h % 2, col(kk, h), 0.0) for h in range(C_HEADS)]
        att = [None] * C_HEADS

        def add_level(q_t, k_mul, level, k_rows=None):
            q_b = q_t.astype(BF16)
            for h in range(C_HEADS):
                k_t = k_head[h] if k_mul is None else k_head[h] * col(k_mul, h)
                if k_rows is not None:
                    k_t = jnp.where(k_rows, k_t, 0.0)
                a = lax.dot_general(col(q_b, h), k_t.astype(BF16), (((1,), (1,)), ((), ())),
                                    preferred_element_type=F32)
                if level is not None:
                    a = a * lmask_ref[level]
                att[h] = a if att[h] is None else att[h] + a

        level = 0
        m = L // 2
        while m >= 1:
            pos = row % (2 * m)
            is_q = (pos < m) if reverse else (pos >= m)
            k_rows = None
            if m == 1:
                q_t = jnp.where(is_q, qs * jnp.exp(la), 0.0)
                k_mul = None
                k_rows = jnp.logical_not(is_q)
            else:
                bnd = m if reverse else m - 1
                if 2 * m >= SUBLANES:
                    pref = rows_from(bs, lambda r: r * 2 * m + bnd, 2 * m)
                else:
                    p0 = rows_from(bs, lambda r: r * SUBLANES + bnd, SUBLANES)
                    p1 = rows_from(bs, lambda r: r * SUBLANES + 2 * m + bnd, SUBLANES)
                    pref = jnp.where(row % SUBLANES < 2 * m, p0, p1)
                q_t = qs * jnp.exp(jnp.where(is_q, bs - pref, -jnp.inf))
                k_mul = jnp.exp(jnp.where(is_q, -jnp.inf, pref - bs))
            add_level(q_t, k_mul, None if m == L // 2 else level, k_rows)
            level += 1
            m //= 2
        add_level(qs, None, level)

        o_inter = jnp.dot((qs * jnp.exp(bs)).astype(BF16), sbd_sc[bb], preferred_element_type=F32)
        o_heads = []
        for h in range(C_HEADS):
            cols = slice(h * HEAD_PAD, (h + 1) * HEAD_PAD)
            o_heads.append(o_inter[:, cols] + jnp.dot(att[h].astype(BF16), vb[:, cols],
                                                      preferred_element_type=F32))
        lay.store(o_ref, i, jnp.concatenate(o_heads, axis=1), (bb,))

        ke_t = (kk * jnp.exp(btot - bs)).T.astype(BF16)
        tdot = lambda a: lax.dot_general(a, ones, (((0,), (0,)), ((), ())), preferred_element_type=F32)
        dec_col = jnp.exp(tdot(la_hi) + tdot(la_lo))
        for h in range(C_HEADS):
            rows = slice(h * CK_PAD, (h + 1) * CK_PAD)
            cols = slice(h * HEAD_PAD, (h + 1) * HEAD_PAD)
            upd = jnp.dot(ke_t[rows], vb[:, cols], preferred_element_type=F32)
            s_new = dec_col[rows] * s_sc[bb, h] + upd
            s_sc[bb, h] = s_new
            sbd_sc[bb, rows, cols] = s_new.astype(BF16)

    def chunk(ii, carry):
        i = lay.group - 1 - ii if reverse else ii
        for bb in range(GLA_BATCH):
            chunk_of_row(bb, i)
        return carry

    lax.fori_loop(0, lay.group, chunk, 0)

    @pl.when(gidx == lay.ngrp - 1)
    def _():
        sfin_ref[...] = s_sc[...]


def _scan_masks(reverse):
    t = jnp.arange(SEQ_BLOCK)[:, None]
    u = jnp.arange(SEQ_BLOCK)[None, :]
    cmask = ((u >= t) if reverse else (u <= t)).astype(BF16)
    sizes = []
    m = SEQ_BLOCK
    while m >= 1:
        sizes.append(m)
        m //= 2
    lmask = jnp.stack([(t // sz) == (u // sz) for sz in sizes]).astype(F32)
    return cmask, lmask


def _gla_call(st, w2s, b_gk, sinit, lay, reverse):
    bsz = st["qc"].shape[0]
    di = 1 if reverse else 0
    nb = GLA_BATCH
    tile = lambda c: lay.spec(c, lambda b, g: (b,), reverse, batch=nb)
    full = lambda shp: pl.BlockSpec(shp, lambda b, g: (0,) * len(shp))
    state_spec = pl.BlockSpec((nb, C_HEADS, CK_PAD, HEAD_PAD), lambda b, g: (b, 0, 0, 0))
    qc, kc, vc, gk = (lay.view(st[n]) for n in ("qc", "kc", "vc", "gk"))
    cmask, lmask = _scan_masks(reverse)
    o, sfin = pl.pallas_call(
        functools.partial(_gla_kernel, lay, reverse),
        grid=(bsz // nb, lay.ngrp),
        in_specs=[tile(CK_W), tile(CK_W), tile(CV_W),
                  lay.spec(LANES, lambda b, g: (di, b), reverse, batch=nb),
                  pl.BlockSpec((1, LANES, CK_W), lambda b, g: (di, 0, 0)),
                  pl.BlockSpec((1, 1, CK_W), lambda b, g: (di, 0, 0)),
                  full(cmask.shape), full(lmask.shape), state_spec],
        out_specs=[tile(CV_W), state_spec],
        out_shape=[jax.ShapeDtypeStruct(vc.shape, F32),
                   jax.ShapeDtypeStruct(sinit.shape, F32)],
        scratch_shapes=[pltpu.VMEM((nb, C_HEADS, CK_PAD, HEAD_PAD), F32),
                        pltpu.VMEM((nb, CK_W, CV_W), BF16)],
        compiler_params=_cparams(2),
        name="gla_scan_bwd" if reverse else "gla_scan_fwd",
    )(qc, kc, vc, gk, w2s, b_gk, cmask, lmask, sinit)
    return lay.unview(o), sfin


def _mix_out_kernel(x_ref, mod_ref, haf_ref, hab_ref, hcf_ref, hcb_ref, oa_ref, gc_ref, s_ref, na_ref, nc_ref,
                    w_ref, o_ref):
    def head_norm(xh, gain):
        return xh * _rms_scale(xh, A_HEAD_DIM) * gain

    acc = None
    for j in range(B_WIDTH // LANES):
        r0 = A_PAD + j * LANES
        part = jnp.dot(s_ref[0, j].astype(BF16), w_ref[r0:r0 + LANES, :], preferred_element_type=F32)
        acc = part if acc is None else acc + part
    for h in range(A_HEADS):
        sl = slice(h * HEAD_PAD, (h + 1) * HEAD_PAD)
        a = head_norm(haf_ref[0, h] + hab_ref[0, h], na_ref[:, sl])
        c = head_norm(hcf_ref[0, h] + hcb_ref[0, h], nc_ref[:, sl])
        ah = (a * jax.nn.sigmoid(oa_ref[0, h])).astype(BF16)
        ch = (c * _silu(gc_ref[0, h])).astype(BF16)
        acc = acc + jnp.dot(ah, w_ref[h * HEAD_PAD:(h + 1) * HEAD_PAD, :], preferred_element_type=F32)
        c0 = A_PAD + B_WIDTH + h * HEAD_PAD
        acc = acc + jnp.dot(ch, w_ref[c0:c0 + HEAD_PAD, :], preferred_element_type=F32)
    o_ref[0] = x_ref[0] + mod_ref[0, 2:3, :] * acc


def _mix_out_call(x, mod, haf, hab, hcf, hcb, oa, gc, s, na, nc, w, tm):
    bsz, t, d = x.shape
    mod_map = (lambda b, i: (b, 0, 0)) if mod.shape[0] > 1 else (lambda b, i: (0, 0, 0))
    row = lambda c: pl.BlockSpec((1, tm, c), lambda b, i: (b, i, 0))
    chunked = lambda c: pl.BlockSpec((1, c // LANES, tm, LANES), lambda b, i: (b, 0, i, 0))
    full = lambda shp: pl.BlockSpec(shp, lambda b, i: (0,) * len(shp))
    return pl.pallas_call(
        _mix_out_kernel,
        grid=(bsz, t // tm),
        in_specs=[row(d), pl.BlockSpec((1, 6, d), mod_map), chunked(A_PAD), chunked(A_PAD), chunked(CV_W),
                  chunked(CV_W), chunked(A_PAD), chunked(CV_W), chunked(B_WIDTH),
                  full((1, A_PAD)), full((1, CV_W)), _resident(w.shape, 2)],
        out_specs=row(d),
        out_shape=jax.ShapeDtypeStruct(x.shape, F32),
        compiler_params=_cparams(2),
        name="mix_out",
    )(x, mod, haf, hab, hcf, hcb, oa, gc, s, na, nc, w)


def _ffn_kernel(nk, final_norm, x_ref, mod_ref, g_ref, wg_ref, wu_ref, wo_ref, fg_ref, o_ref, h_sc, acc_sc):
    kk = pl.program_id(2)

    @pl.when(kk == 0)
    def _():
        x = x_ref[0]
        y = x * _rms_scale(x, x.shape[-1]) * g_ref[...]
        h_sc[...] = (y * (1.0 + mod_ref[0, 4:5, :]) + mod_ref[0, 3:4, :]).astype(BF16)
        acc_sc[...] = jnp.zeros_like(acc_sc)

    h = h_sc[...]
    gate = jnp.dot(h, wg_ref[...], preferred_element_type=F32)
    up = jnp.dot(h, wu_ref[...], preferred_element_type=F32)
    act = (_silu(gate) * up).astype(BF16)
    acc_sc[...] += jnp.dot(act, wo_ref[...], preferred_element_type=F32)

    @pl.when(kk == nk - 1)
    def _():
        y = x_ref[0] + mod_ref[0, 5:6, :] * acc_sc[...]
        if final_norm:
            y = y * _rms_scale(y, y.shape[-1]) * fg_ref[...]
        o_ref[0] = y


def _ffn_call(x, mod, g, w_in, w_out, final_g, final_norm, tm, nk):
    bsz, t, d = x.shape
    d_ff = w_out.shape[0]
    tk = d_ff // nk
    wmode = {"pipeline_mode": pl.Buffered(1)} if nk == 1 else {}
    mod_map = (lambda b, i, k: (b, 0, 0)) if mod.shape[0] > 1 else (lambda b, i, k: (0, 0, 0))
    return pl.pallas_call(
        functools.partial(_ffn_kernel, nk, final_norm),
        grid=(bsz, t // tm, nk),
        in_specs=[pl.BlockSpec((1, tm, d), lambda b, i, k: (b, i, 0)),
                  pl.BlockSpec((1, 6, d), mod_map),
                  pl.BlockSpec((1, d), lambda b, i, k: (0, 0)),
                  pl.BlockSpec((d, tk), lambda b, i, k: (0, k), **wmode),
                  pl.BlockSpec((d, tk), lambda b, i, k: (0, nk + k), **wmode),
                  pl.BlockSpec((tk, d), lambda b, i, k: (k, 0), **wmode),
                  pl.BlockSpec((1, d), lambda b, i, k: (0, 0))],
        out_specs=pl.BlockSpec((1, tm, d), lambda b, i, k: (b, i, 0)),
        out_shape=jax.ShapeDtypeStruct(x.shape, F32),
        scratch_shapes=[pltpu.VMEM((tm, d), BF16), pltpu.VMEM((tm, d), F32)],
        compiler_params=_cparams(3),
        name="ffn",
    )(x, mod, g, w_in, w_in, w_out, final_g)


def _pad_heads(w, n_heads, dim, pad, axis=-1):
    w = jnp.moveaxis(w, axis, -1)
    lead = w.shape[:-1]
    w = w.reshape(lead + (n_heads, dim))
    w = jnp.pad(w, [(0, 0)] * len(lead) + [(0, 0), (0, pad - dim)])
    return jnp.moveaxis(w.reshape(lead + (n_heads * pad,)), -1, axis)


def _pad_to(w, width):
    return jnp.pad(w, [(0, 0)] * (w.ndim - 1) + [(0, width - w.shape[-1])])


def _prep_w_in(w_in):
    a_w = A_HEADS * A_HEAD_DIM
    ck = C_HEADS * C_HEAD_DK
    cv = C_HEADS * C_HEAD_DV
    sizes = (a_w, a_w, a_w, a_w, N_DIR * 2 * A_HEADS, B_WIDTH, B_WIDTH, ck, ck, cv, cv, N_DIR * C_GATE_RANK)
    pts = [sum(sizes[:i + 1]) for i in range(len(sizes) - 1)]
    qa, ka, va, oa, ga, ub, vb, qc, kc, vc, gc, gkc = jnp.split(w_in, pts, axis=-1)
    pa = lambda w: _pad_heads(w, A_HEADS, A_HEAD_DIM, HEAD_PAD)
    pk = lambda w: _pad_heads(w, C_HEADS, C_HEAD_DK, CK_PAD)
    pv = lambda w: _pad_heads(w, C_HEADS, C_HEAD_DV, HEAD_PAD)
    ng = 2 * A_HEADS
    gk_dir = lambda w: _pad_to(jnp.tile(w, (1, GK_COPIES)), LANES)
    cols = [pa(qa), pa(ka), pa(va), pa(oa),
            _pad_to(ga[:, :ng], LANES), _pad_to(ga[:, ng:], LANES),
            ub, vb, pk(qc), pk(kc), pv(vc), pv(gc),
            gk_dir(gkc[:, :C_GATE_RANK]), gk_dir(gkc[:, C_GATE_RANK:])]
    return jnp.concatenate(cols, axis=-1).astype(BF16)


def _prep_w_gk2(w_gk2):
    w = _pad_heads(w_gk2, C_HEADS, C_HEAD_DK, CK_PAD)
    hi = w.astype(BF16)
    lo = (w - hi.astype(F32)).astype(BF16)
    stacked = jnp.concatenate([hi, lo, hi], axis=1)
    return jnp.pad(stacked, ((0, 0), (0, LANES - GK_COPIES * C_GATE_RANK), (0, 0)))


def _prep_w_out(w_out):
    a_w = A_HEADS * A_HEAD_DIM
    wa, wb, wc = w_out[:a_w], w_out[a_w:a_w + B_WIDTH], w_out[a_w + B_WIDTH:]
    wa = _pad_heads(wa, A_HEADS, A_HEAD_DIM, HEAD_PAD, axis=0)
    wc = _pad_heads(wc, C_HEADS, C_HEAD_DV, HEAD_PAD, axis=0)
    return jnp.concatenate([wa, wb, wc], axis=0).astype(BF16)


def _mixer_scans(st, lw, a_state, c_state, lay):
    qconv, kconv, s = _seq_prep_call(st, lw["conv"], lw["sgu_g"], lw["sgu_w"], lw["sgu_bias"], lay)
    ha, a_fin, hc, c_fin = [], [], [], []
    for di, reverse in enumerate((False, True)):
        h, cfin, mfin = _mlstm_call(qconv, kconv, st, lw["gate_b"], a_state[di][0], a_state[di][1], lay, reverse)
        ha.append(h)
        a_fin.append((cfin, mfin))
        o, sfin = _gla_call(st, lw["w_gk2"], lw["b_gk"], c_state[di], lay, reverse)
        hc.append(o)
        c_fin.append(sfin)
    return ha, hc, s, a_fin, c_fin


def kernel(x, c, ctx, c_ctx, norm1_g, norm2_g, w_ada, b_ada, w_in, mlstm_conv, mlstm_gate_b,
           mlstm_norm_g, gla_w_gk2, gla_b_gk, gla_norm_g, sgu_norm_g, sgu_w, sgu_b, w_out,
           w_ffn_in, w_ffn_out, final_g):
    bsz, seq, d = x.shape
    ctx_len = ctx.shape[1]
    depth = w_in.shape[0]
    n_cond = 2 * SUBLANES
    cond = jnp.zeros((n_cond, d), F32).at[:bsz].set(c).at[bsz].set(c_ctx)
    mods = _ada_call(cond, w_ada, b_ada).reshape(depth, n_cond, 6, d)

    fg = final_g.reshape(1, d)
    x_lat, x_ctx = x, ctx
    for l in range(depth):
        need_ctx = l < depth - 1
        mod_lat = mods[l, :bsz]
        mod_ctx = mods[l, bsz:bsz + 1]
        gb = mlstm_gate_b[l].reshape(N_DIR, 1, 2 * A_HEADS)
        lw = {
            "conv": jnp.stack([_pad_heads(mlstm_conv[l][:, :A_HEADS * A_HEAD_DIM], A_HEADS, A_HEAD_DIM, HEAD_PAD),
                               _pad_heads(mlstm_conv[l][:, A_HEADS * A_HEAD_DIM:], A_HEADS, A_HEAD_DIM, HEAD_PAD)]),
            "gate_b": _pad_to(gb, LANES),
            "w_gk2": _prep_w_gk2(gla_w_gk2[l]),
            "b_gk": _pad_heads(gla_b_gk[l], C_HEADS, C_HEAD_DK, CK_PAD).reshape(N_DIR, 1, CK_W),
            "sgu_g": sgu_norm_g[l].reshape(1, B_WIDTH),
            "sgu_w": sgu_w[l].astype(BF16),
            "sgu_bias": jnp.repeat(sgu_b[l].T, B_GROUP_DIM, axis=1),
        }
        w_in_l = _prep_w_in(w_in[l])
        w_out_l = _prep_w_out(w_out[l])
        na = _pad_heads(mlstm_norm_g[l], A_HEADS, A_HEAD_DIM, HEAD_PAD).reshape(1, A_PAD)
        nc = _pad_heads(jnp.tile(gla_norm_g[l], C_HEADS), C_HEADS, C_HEAD_DV, HEAD_PAD).reshape(1, CV_W)
        g1 = norm1_g[l].reshape(1, d)
        g2 = norm2_g[l].reshape(1, d)
        w_ffn_in_l = w_ffn_in[l].astype(BF16)
        w_ffn_out_l = w_ffn_out[l].astype(BF16)

        st_ctx = _proj_in_call(x_ctx, mod_ctx, g1, w_in_l, tm=_row_tile(ctx_len, PROJ_ROWS))
        st_lat = _proj_in_call(x_lat, mod_lat, g1, w_in_l, tm=_row_tile(seq, PROJ_ROWS))

        a0 = (jnp.zeros((bsz, A_HEADS, HEAD_PAD, HEAD_PAD), F32), jnp.zeros((bsz, SUBLANES, LANES), F32))
        c0 = jnp.zeros((bsz, C_HEADS, CK_PAD, HEAD_PAD), F32)
        lay_ctx = _SeqLayout(ctx_len, False)
        lay_lat = _SeqLayout(seq, l % 2 == 1)
        ha_c, hc_c, s_c, a_state, c_state = _mixer_scans(st_ctx, lw, (a0, a0), (c0, c0), lay_ctx)
        ha_l, hc_l, s_l, _, _ = _mixer_scans(st_lat, lw, a_state, c_state, lay_lat)

        x_lat = _mix_out_call(x_lat, mod_lat, ha_l[0], ha_l[1], hc_l[0], hc_l[1], st_lat["oa"], st_lat["gc"], s_l,
                              na, nc, w_out_l, tm=_row_tile(seq, MIX_ROWS))
        x_lat = _ffn_call(x_lat, mod_lat, g2, w_ffn_in_l, w_ffn_out_l, fg, not need_ctx,
                          tm=_row_tile(seq, FFN_ROWS), nk=FFN_SPLIT)
        if need_ctx:
            x_ctx = _mix_out_call(x_ctx, mod_ctx, ha_c[0], ha_c[1], hc_c[0], hc_c[1], st_ctx["oa"], st_ctx["gc"], s_c,
                                  na, nc, w_out_l, tm=_row_tile(ctx_len, MIX_ROWS))
            x_ctx = _ffn_call(x_ctx, mod_ctx, g2, w_ffn_in_l, w_ffn_out_l, fg, False,
                              tm=_row_tile(ctx_len, FFN_ROWS), nk=FFN_SPLIT)
    return x_lat
```

```python
import functools
import math

import jax
import jax.numpy as jnp
from jax import lax
from jax.experimental import pallas as pl
from jax.experimental.pallas import tpu as pltpu

F32 = jnp.float32
BF16 = jnp.bfloat16
HIGHEST = lax.Precision.HIGHEST

LANES = 128
SUBLANES = 8
VMEM_LIMIT_BYTES = 56 * 1024 * 1024

GRID_W = 64
RMS_EPS = 1e-6
A_HEADS = 4
A_HEAD_DIM = 96
B_GROUPS = 4
B_GROUP_DIM = 64
B_WIDTH = B_GROUPS * B_GROUP_DIM
C_HEADS = 4
C_HEAD_DK = 48
C_HEAD_DV = 96
C_GATE_RANK = 16
C_GATE_NORM = 16.0
N_DIR = 2

HEAD_PAD = LANES
A_PAD = A_HEADS * HEAD_PAD
CK_PAD = 64
CK_W = C_HEADS * CK_PAD
CV_W = C_HEADS * HEAD_PAD

SEQ_BLOCK = 128
SEQ_GROUP = SUBLANES
MLSTM_BATCH = 2
GLA_BATCH = 2
GK_COPIES = 3

PROJ_ROWS = 512
MIX_ROWS = 512
FFN_ROWS = 512
FFN_SPLIT = 1

_IN_LAYOUT = (("qa", A_PAD, 1), ("ka", A_PAD, 1), ("va", A_PAD, 1), ("oa", A_PAD, 1),
              ("ga", LANES, N_DIR),
              ("ub", B_WIDTH, 1), ("vb", B_WIDTH, 1),
              ("qc", CK_W, 1), ("kc", CK_W, 1), ("vc", CV_W, 1), ("gc", CV_W, 1),
              ("gk", LANES, N_DIR))
N_IN_PAD = sum(w * n for _, w, n in _IN_LAYOUT)


def _cparams(n_axes):
    return pltpu.CompilerParams(dimension_semantics=("arbitrary",) * n_axes,
                                vmem_limit_bytes=VMEM_LIMIT_BYTES)


def _resident(shape, n_grid):
    zeros = (0,) * len(shape)
    maps = {2: lambda a, b: zeros, 3: lambda a, b, c: zeros}
    return pl.BlockSpec(shape, maps[n_grid], pipeline_mode=pl.Buffered(1))


def _row_tile(t_len, want):
    return min(want, t_len)


def _log_sigmoid(x):
    return jnp.minimum(x, 0.0) - jnp.log(1.0 + jnp.exp(-jnp.abs(x)))


def _silu(x):
    return x * jax.nn.sigmoid(x)


def _gelu_tanh(x):
    c = math.sqrt(2.0 / math.pi)
    return 0.5 * x * (1.0 + jnp.tanh(c * (x + 0.044715 * (x * x * x))))


def _rms_scale(x, n):
    return lax.rsqrt(jnp.sum(x * x, axis=-1, keepdims=True) * (1.0 / n) + RMS_EPS)


def _split_hi_lo(x):
    hi = x.astype(BF16)
    lo = (x - hi.astype(F32)).astype(BF16)
    return hi, lo


def _ada_kernel(s_ref, w_ref, b_ref, o_ref):
    s = _silu(s_ref[...])
    o_ref[0] = jnp.dot(s, w_ref[0], precision=HIGHEST, preferred_element_type=F32) + b_ref[0]


def _ada_call(cond, w_ada, b_ada):
    depth, d, n6 = w_ada.shape
    nb = cond.shape[0]
    tn = 1536
    return pl.pallas_call(
        _ada_kernel,
        grid=(depth, n6 // tn),
        in_specs=[pl.BlockSpec((nb, d), lambda l, j: (0, 0)),
                  pl.BlockSpec((1, d, tn), lambda l, j: (l, 0, j)),
                  pl.BlockSpec((1, 1, tn), lambda l, j: (l, 0, j))],
        out_specs=pl.BlockSpec((1, nb, tn), lambda l, j: (l, 0, j)),
        out_shape=jax.ShapeDtypeStruct((depth, nb, n6), F32),
        compiler_params=_cparams(2),
        name="ada_mod",
    )(cond, w_ada, b_ada.reshape(depth, 1, n6))


def _proj_in_kernel(x_ref, mod_ref, g_ref, w_ref, *out_refs):
    x = x_ref[0]
    y = x * _rms_scale(x, x.shape[-1]) * g_ref[...]
    h = (y * (1.0 + mod_ref[0, 1:2, :]) + mod_ref[0, 0:1, :]).astype(BF16)
    off = 0
    for (_, width, n_dir), o_ref in zip(_IN_LAYOUT, out_refs):
        for di in range(n_dir):
            res = jnp.dot(h, w_ref[:, off:off + width], preferred_element_type=F32)
            for j in range(width // LANES):
                piece = res[:, j * LANES:(j + 1) * LANES]
                if n_dir == 1:
                    o_ref[0, j] = piece
                else:
                    o_ref[di, 0, j] = piece
            off += width


def _proj_in_call(x, mod, g, w, tm):
    bsz, t, d = x.shape
    mod_b = mod.shape[0]
    mod_map = (lambda b, i: (b, 0, 0)) if mod_b > 1 else (lambda b, i: (0, 0, 0))
    out_shapes, out_specs = [], []
    for _, wd, n_dir in _IN_LAYOUT:
        nch = wd // LANES
        if n_dir == 1:
            out_shapes.append(jax.ShapeDtypeStruct((bsz, nch, t, LANES), F32))
            out_specs.append(pl.BlockSpec((1, nch, tm, LANES), lambda b, i: (b, 0, i, 0)))
        else:
            out_shapes.append(jax.ShapeDtypeStruct((n_dir, bsz, nch, t, LANES), F32))
            out_specs.append(pl.BlockSpec((n_dir, 1, nch, tm, LANES), lambda b, i: (0, b, 0, i, 0)))
    outs = pl.pallas_call(
        _proj_in_kernel,
        grid=(bsz, t // tm),
        in_specs=[pl.BlockSpec((1, tm, d), lambda b, i: (b, i, 0)),
                  pl.BlockSpec((1, 6, d), mod_map),
                  pl.BlockSpec((1, d), lambda b, i: (0, 0)),
                  _resident((d, N_IN_PAD), 2)],
        out_specs=out_specs,
        out_shape=out_shapes,
        compiler_params=_cparams(2),
        name="proj_in",
    )(x, mod, g, w)
    return dict(zip([n for n, _, _ in _IN_LAYOUT], outs))


class _SeqLayout:
    def __init__(self, t_len, column_major):
        self.cm = column_major
        self.t_len = t_len
        self.nblk = t_len // SEQ_BLOCK
        self.group = min(SEQ_GROUP, self.nblk)
        self.ngrp = self.nblk // self.group
        if column_major:
            assert t_len == SEQ_BLOCK * GRID_W and self.group == SEQ_GROUP

    def view(self, a):
        if not self.cm:
            return a
        return a.reshape(a.shape[:-2] + (SEQ_BLOCK, GRID_W, LANES))

    def unview(self, a):
        if not self.cm:
            return a
        return a.reshape(a.shape[:-3] + (self.t_len, LANES))

    def spec(self, c, lead_map, reverse=False, batch=1):
        def grp(g):
            return self.ngrp - 1 - g if reverse else g

        n_lead = len(lead_map(0, 0))
        lead_blk = (1,) * (n_lead - 1) + (batch,)
        nch = c // LANES
        if self.cm:
            return pl.BlockSpec(lead_blk + (nch, SEQ_BLOCK, self.group, LANES),
                                lambda b, g: lead_map(b, g) + (0, 0, grp(g), 0))
        return pl.BlockSpec(lead_blk + (nch, self.group * SEQ_BLOCK, LANES),
                            lambda b, g: lead_map(b, g) + (0, grp(g), 0))

    def halo_specs(self, c):
        nch = c // LANES
        if self.cm:
            shp = (1, nch, SUBLANES, self.group, LANES)
            prev = pl.BlockSpec(shp, lambda b, g: (b, 0, SEQ_BLOCK // SUBLANES - 1, jnp.maximum(g - 1, 0), 0))
            nxt = pl.BlockSpec(shp, lambda b, g: (b, 0, 0, jnp.minimum(g + 1, self.ngrp - 1), 0))
        else:
            per = self.group * SEQ_BLOCK // SUBLANES
            last = self.t_len // SUBLANES - 1
            shp = (1, nch, SUBLANES, LANES)
            prev = pl.BlockSpec(shp, lambda b, g: (b, 0, jnp.maximum(g * per - 1, 0), 0))
            nxt = pl.BlockSpec(shp, lambda b, g: (b, 0, jnp.minimum((g + 1) * per, last), 0))
        return prev, nxt

    @staticmethod
    def _cat(pieces):
        return pieces[0] if len(pieces) == 1 else jnp.concatenate(pieces, axis=1)

    def halo_prev_row(self, ref):
        r = SUBLANES - 1
        if self.cm:
            return self._cat([ref[0, j, r:r + 1, self.group - 1, :] for j in range(ref.shape[1])])
        return self._cat([ref[0, j, r:r + 1, :] for j in range(ref.shape[1])])

    def halo_next_row(self, ref):
        if self.cm:
            return self._cat([ref[0, j, 0:1, 0, :] for j in range(ref.shape[1])])
        return self._cat([ref[0, j, 0:1, :] for j in range(ref.shape[1])])

    def _piece(self, ref, lead, j, i):
        if self.cm:
            flat = ref.at[lead + (j,)].reshape(self.group * SEQ_BLOCK, LANES)
            return flat, (pl.ds(i, SEQ_BLOCK, stride=self.group), slice(None))
        start = i * SEQ_BLOCK
        if not isinstance(i, int):
            start = pl.multiple_of(start, SEQ_BLOCK)
        return ref, lead + (j, pl.ds(start, SEQ_BLOCK), slice(None))

    def load(self, ref, i, lead=(0,)):
        pieces = []
        for j in range(ref.shape[len(lead)]):
            r, idx = self._piece(ref, lead, j, i)
            pieces.append(r[idx])
        return self._cat(pieces)

    def store(self, ref, i, val, lead=(0,)):
        for j in range(ref.shape[len(lead)]):
            r, idx = self._piece(ref, lead, j, i)
            r[idx] = val[:, j * LANES:(j + 1) * LANES]

    def row(self, ref, i, r):
        if self.cm:
            return self._cat([ref[0, j, r:r + 1, i, :] for j in range(ref.shape[1])])
        t = i * SEQ_BLOCK + r
        return self._cat([ref[0, j, t:t + 1, :] for j in range(ref.shape[1])])


def _seq_prep_kernel(lay, q_ref, qp_ref, qn_ref, k_ref, kp_ref, kn_ref, u_ref, v_ref, cw_ref, ng_ref, w_ref,
                     bias_ref, qo_ref, ko_ref, s_ref):
    g = pl.program_id(1)
    L = SEQ_BLOCK
    row = lax.broadcasted_iota(jnp.int32, (L, 1), 0)
    has_prev = (g > 0).astype(F32)
    has_next = (g < lay.ngrp - 1).astype(F32)
    grp = lax.broadcasted_iota(jnp.int32, (1, B_WIDTH), 1) // B_GROUP_DIM

    for i in range(lay.group):
        for x_ref, xp_ref, xn_ref, o_ref, w, scale in ((q_ref, qp_ref, qn_ref, qo_ref, cw_ref[0], A_HEAD_DIM ** -0.5),
                                                       (k_ref, kp_ref, kn_ref, ko_ref, cw_ref[1], None)):
            x = lay.load(x_ref, i)
            prev_row = lay.row(x_ref, i - 1, L - 1) if i > 0 else lay.halo_prev_row(xp_ref) * has_prev
            next_row = lay.row(x_ref, i + 1, 0) if i < lay.group - 1 else lay.halo_next_row(xn_ref) * has_next
            x_dn = jnp.where(row == 0, prev_row, pltpu.roll(x, 1, 0))
            x_up = jnp.where(row == L - 1, next_row, pltpu.roll(x, L - 1, 0))
            y = _silu(w[0:1, :] * x_dn + w[1:2, :] * x + w[2:3, :] * x_up)
            lay.store(o_ref, i, y if scale is None else y * scale)

        u = _gelu_tanh(lay.load(u_ref, i))
        v = _gelu_tanh(lay.load(v_ref, i))
        v = v * _rms_scale(v, B_WIDTH) * ng_ref[...]
        mixed = bias_ref[...]
        for gi in range(B_GROUPS):
            vg = jnp.where(grp == gi, v, 0.0).astype(BF16)
            mixed = mixed + jnp.dot(w_ref[gi], vg, preferred_element_type=F32)
        lay.store(s_ref, i, u * mixed)


def _seq_prep_call(st, conv_w, norm_g, w_s, bias, lay):
    bsz = st["qa"].shape[0]
    lead = lambda b, g: (b,)
    tile = lambda c: lay.spec(c, lead)
    hp, hn = lay.halo_specs(A_PAD)
    full = lambda shp: pl.BlockSpec(shp, lambda b, g: (0,) * len(shp))
    qa, ka, ub, vb = (lay.view(st[n]) for n in ("qa", "ka", "ub", "vb"))
    qo, ko, s = pl.pallas_call(
        functools.partial(_seq_prep_kernel, lay),
        grid=(bsz, lay.ngrp),
        in_specs=[tile(A_PAD), hp, hn, tile(A_PAD), hp, hn, tile(B_WIDTH), tile(B_WIDTH),
                  full((2, 3, A_PAD)), full((1, B_WIDTH)), full((B_GROUPS, SEQ_BLOCK, SEQ_BLOCK)),
                  full((SEQ_BLOCK, B_WIDTH))],
        out_specs=[tile(A_PAD), tile(A_PAD), tile(B_WIDTH)],
        out_shape=[jax.ShapeDtypeStruct(qa.shape, F32), jax.ShapeDtypeStruct(ka.shape, F32),
                   jax.ShapeDtypeStruct(ub.shape, F32)],
        compiler_params=_cparams(2),
        name="seq_prep",
    )(qa, qa, qa, ka, ka, ka, ub, vb, conv_w, norm_g, w_s, bias)
    return lay.unview(qo), lay.unview(ko), lay.unview(s)


def _mlstm_kernel(lay, reverse, q_ref, k_ref, v_ref, g_ref, gb_ref, cmask_ref, spread_ref, rowsel_ref,
                  cinit_ref, minit_ref, h_ref, cfin_ref, mfin_ref, c_sc, m_sc):
    gidx = pl.program_id(1)
    L = SEQ_BLOCK

    @pl.when(gidx == 0)
    def _():
        c_sc[...] = cinit_ref[...]
        m_sc[...] = minit_ref[...]

    glane = lax.broadcasted_iota(jnp.int32, (1, LANES), 1)
    ri = lax.broadcasted_iota(jnp.int32, (L, L), 0)
    ci = lax.broadcasted_iota(jnp.int32, (L, L), 1)
    past = (ci >= ri) if reverse else (ci <= ri)
    last = 0 if reverse else L - 1
    ones = jnp.ones((L, HEAD_PAD), F32)
    twice = lambda x: jnp.concatenate([x, x], axis=1)

    def chunk(ii, carry):
        i = lay.group - 1 - ii if reverse else ii
        rows = range(MLSTM_BATCH)
        q = [lay.load(q_ref, i, (bb,)).astype(BF16) for bb in rows]
        k = [lay.load(k_ref, i, (bb,)).astype(BF16) for bb in rows]
        v = [lay.load(v_ref, i, (bb,)) for bb in rows]
        g = [lay.load(g_ref, i, (0, bb)) + gb_ref[0] for bb in rows]
        lf = [_split_hi_lo(_log_sigmoid(g[bb])) for bb in rows]
        cs = [jnp.dot(cmask_ref[...], jnp.concatenate(lf[bb], axis=1), preferred_element_type=F32) for bb in rows]
        bc = [cs[bb][:, 0:LANES] + cs[bb][:, LANES:2 * LANES] for bb in rows]
        x = [jnp.concatenate(_split_hi_lo(jnp.where(glane < A_HEADS, g[bb], bc[bb])), axis=1) for bb in rows]
        spread = [jnp.dot(x[bb], spread_ref[...], preferred_element_type=F32) for bb in rows]
        c_rows = [lax.dot_general(rowsel_ref[...], x[bb], (((1,), (1,)), ((), ())), preferred_element_type=F32)
                  for bb in rows]

        units = [(bb, h) for bb in rows for h in range(A_HEADS)]
        each = lambda f: {u: f(*u) for u in units}
        sl = lambda h: slice(h * HEAD_PAD, (h + 1) * HEAD_PAD)
        b_rep = each(lambda bb, h: spread[bb][:, sl(h)])
        c_rep = each(lambda bb, h: spread[bb][:, sl(A_HEADS + h)])
        c_row = each(lambda bb, h: c_rows[bb][h:h + 1, :])
        m_prev = each(lambda bb, h: m_sc[bb, h:h + 1, :])
        c2 = each(lambda bb, h: c_sc[bb, h])
        v2 = each(lambda bb, h: jnp.concatenate([v[bb][:, sl(h)], ones], axis=1))
        u_rep = each(lambda bb, h: jnp.maximum(
            m_prev[bb, h], jnp.max(jnp.where(past, c_row[bb, h], -jnp.inf), axis=1, keepdims=True)))
        qk = each(lambda bb, h: lax.dot_general(q[bb][:, sl(h)], k[bb][:, sl(h)], (((1,), (1,)), ((), ())),
                                                preferred_element_type=F32))
        s = each(lambda bb, h: (qk[bb, h] * jnp.exp(jnp.where(past, c_row[bb, h] - u_rep[bb, h], -jnp.inf))
                                ).astype(BF16))
        inter = each(lambda bb, h: jnp.exp(m_prev[bb, h] - u_rep[bb, h]))
        carry_in = each(lambda bb, h: jnp.dot(q[bb][:, sl(h)], c2[bb, h].astype(BF16), preferred_element_type=F32))
        num2 = each(lambda bb, h: jnp.dot(s[bb, h], v2[bb, h].astype(BF16), preferred_element_type=F32)
                    + twice(inter[bb, h]) * carry_in[bb, h])
        inv = each(lambda bb, h: 1.0 / jnp.maximum(jnp.abs(num2[bb, h][:, HEAD_PAD:]),
                                                   jnp.exp(-(b_rep[bb, h] + u_rep[bb, h]))))
        for bb in rows:
            lay.store(h_ref, i, jnp.concatenate([num2[bb, h][:, :HEAD_PAD] * inv[bb, h] for h in range(A_HEADS)],
                                                axis=1), (bb,))

        b_last = each(lambda bb, h: b_rep[bb, h][last:last + 1, :])
        w_log = each(lambda bb, h: b_last[bb, h] + c_rep[bb, h])
        m_new = each(lambda bb, h: jnp.maximum(b_last[bb, h] + m_prev[bb, h],
                                               jnp.max(w_log[bb, h], axis=0, keepdims=True)))
        wv2 = each(lambda bb, h: (twice(jnp.exp(w_log[bb, h] - m_new[bb, h])) * v2[bb, h]).astype(BF16))
        for bb, h in units:
            upd = lax.dot_general(k[bb][:, sl(h)], wv2[bb, h], (((0,), (0,)), ((), ())), preferred_element_type=F32)
            decay = jnp.exp(b_last[bb, h] + m_prev[bb, h] - m_new[bb, h])
            c_sc[bb, h] = twice(decay) * c2[bb, h] + upd
            m_sc[bb, h:h + 1, :] = m_new[bb, h]
        return carry

    lax.fori_loop(0, lay.group, chunk, 0)

    @pl.when(gidx == lay.ngrp - 1)
    def _():
        cfin_ref[...] = c_sc[...]
        mfin_ref[...] = m_sc[...]


def _mlstm_gate_matrices():
    lane = jnp.arange(LANES)
    blocks = []
    for h in range(A_HEADS):
        blocks.append(jnp.broadcast_to((lane == A_HEADS + h).astype(F32)[:, None], (LANES, LANES)))
    for h in range(A_HEADS):
        col = (lane == h).astype(F32) - (lane == A_HEADS + h).astype(F32)
        blocks.append(jnp.broadcast_to(col[:, None], (LANES, LANES)))
    half = jnp.concatenate(blocks, axis=1)
    spread = jnp.concatenate([half, half], axis=0).astype(BF16)
    sel = jnp.stack([(lane == h).astype(F32) - (lane == A_HEADS + h).astype(F32) for h in range(A_HEADS)]
                    + [jnp.zeros((LANES,), F32)] * (2 * SUBLANES - A_HEADS))
    rowsel = jnp.concatenate([sel, sel], axis=1).astype(BF16)
    return spread, rowsel


def _mlstm_call(qconv, kconv, st, gate_b, cinit, minit, lay, reverse):
    bsz = qconv.shape[0]
    di = 1 if reverse else 0
    nb = MLSTM_BATCH
    tile = lambda c: lay.spec(c, lambda b, g: (b,), reverse, batch=nb)
    full = lambda shp: pl.BlockSpec(shp, lambda b, g: (0,) * len(shp))
    c_spec = pl.BlockSpec((nb, A_HEADS, HEAD_PAD, 2 * HEAD_PAD), lambda b, g: (b, 0, 0, 0))
    m_spec = pl.BlockSpec((nb, SUBLANES, LANES), lambda b, g: (b, 0, 0))
    qa, ka, va, ga = lay.view(qconv), lay.view(kconv), lay.view(st["va"]), lay.view(st["ga"])
    cmask, _ = _scan_masks(reverse)
    spread, rowsel = _mlstm_gate_matrices()
    h, cfin, mfin = pl.pallas_call(
        functools.partial(_mlstm_kernel, lay, reverse),
        grid=(bsz // nb, lay.ngrp),
        in_specs=[tile(A_PAD), tile(A_PAD), tile(A_PAD),
                  lay.spec(LANES, lambda b, g: (di, b), reverse, batch=nb),
                  pl.BlockSpec((1, 1, LANES), lambda b, g: (di, 0, 0)),
                  full(cmask.shape), full(spread.shape), full(rowsel.shape),
                  c_spec, m_spec],
        out_specs=[tile(A_PAD), c_spec, m_spec],
        out_shape=[jax.ShapeDtypeStruct(qa.shape, F32),
                   jax.ShapeDtypeStruct(cinit.shape, F32),
                   jax.ShapeDtypeStruct(minit.shape, F32)],
        scratch_shapes=[pltpu.VMEM((nb, A_HEADS, HEAD_PAD, 2 * HEAD_PAD), F32),
                        pltpu.VMEM((nb, SUBLANES, LANES), F32)],
        compiler_params=_cparams(2),
        name="mlstm_scan_bwd" if reverse else "mlstm_scan_fwd",
    )(qa, ka, va, ga, gate_b, cmask, spread, rowsel, cinit, minit)
    return lay.unview(h), cfin, mfin


def _gla_kernel(lay, reverse, q_ref, k_ref, v_ref, gk_ref, w2_ref, bgk_ref, cmask_ref, lmask_ref, sinit_ref,
                o_ref, sfin_ref, s_sc, sbd_sc):
    gidx = pl.program_id(1)
    L = SEQ_BLOCK

    @pl.when(gidx == 0)
    def _():
        s_sc[...] = sinit_ref[...]
        sbd_sc[...] = jnp.zeros_like(sbd_sc)
        for bb in range(GLA_BATCH):
            for h in range(C_HEADS):
                sbd_sc[bb, h * CK_PAD:(h + 1) * CK_PAD, h * HEAD_PAD:(h + 1) * HEAD_PAD] = \
                    sinit_ref[bb, h].astype(BF16)

    glane = lax.broadcasted_iota(jnp.int32, (1, LANES), 1)
    row = lax.broadcasted_iota(jnp.int32, (L, 1), 0)
    hlane = glane // CK_PAD
    col = lambda x, h: x[:, (h // 2) * LANES:(h // 2 + 1) * LANES]
    last = 0 if reverse else L - 1
    ones = jnp.ones((L, HEAD_PAD), BF16)

    def rows_from(x, idx_of_row_block, rows_per_block):
        n = L // rows_per_block
        return jnp.concatenate([jnp.broadcast_to(x[idx_of_row_block(i):idx_of_row_block(i) + 1],
                                                 (rows_per_block, x.shape[1])) for i in range(n)], axis=0)

    def chunk(ii, carry):
        i = lay.group - 1 - ii if reverse else ii
        rows = range(GLA_BATCH)
        heads = range(C_HEADS)
        g = [lay.load(gk_ref, i, (0, bb)) for bb in rows]
        g_hi = [g[bb].astype(BF16).astype(F32) for bb in rows]
        g_split = [jnp.where(glane < 2 * C_GATE_RANK, g_hi[bb], g[bb] - g_hi[bb]).astype(BF16) for bb in rows]
        la = [_log_sigmoid(jnp.dot(g_split[bb], w2_ref[0], preferred_element_type=F32) + bgk_ref[0])
              * (1.0 / C_GATE_NORM) for bb in rows]
        la_split = [_split_hi_lo(la[bb]) for bb in rows]
        cs = [jnp.dot(cmask_ref[...], jnp.concatenate(la_split[bb], axis=1), preferred_element_type=F32)
              for bb in rows]
        bs = [cs[bb][:, 0:CK_W] + cs[bb][:, CK_W:2 * CK_W] for bb in rows]
        qs = [lay.load(q_ref, i, (bb,)) * (C_HEAD_DK ** -0.5) for bb in rows]
        kk = [lay.load(k_ref, i, (bb,)) for bb in rows]
        vb = [lay.load(v_ref, i, (bb,)).astype(BF16) for bb in rows]
        k_head = [[jnp.where(hlane == h % 2, col(kk[bb], h), 0.0) for h in heads] for bb in rows]
        att = [[None] * C_HEADS for _ in rows]

        def add_level(bb, q_t, k_mul, level, k_rows=None):
            q_b = q_t.astype(BF16)
            for h in heads:
                k_t = k_head[bb][h] if k_mul is None else k_head[bb][h] * col(k_mul, h)
                if k_rows is not None:
                    k_t = jnp.where(k_rows, k_t, 0.0)
                a = lax.dot_general(col(q_b, h), k_t.astype(BF16), (((1,), (1,)), ((), ())),
                                    preferred_element_type=F32)
                if level is not None:
                    a = a * lmask_ref[level]
                att[bb][h] = a if att[bb][h] is None else att[bb][h] + a

        def level_factors(bb, m):
            bnd = m if reverse else m - 1
            if m >= SUBLANES:
                zero = jnp.zeros((m, CK_W), F32)
                q_parts, k_parts = [], []
                for base in range(0, L, 2 * m):
                    pref = bs[bb][base + bnd:base + bnd + 1]
                    lo, hi = slice(base, base + m), slice(base + m, base + 2 * m)
                    q_half, k_half = (lo, hi) if reverse else (hi, lo)
                    q_piece = qs[bb][q_half] * jnp.exp(bs[bb][q_half] - pref)
                    k_piece = jnp.exp(pref - bs[bb][k_half])
                    q_parts += [q_piece, zero] if reverse else [zero, q_piece]
                    k_parts += [zero, k_piece] if reverse else [k_piece, zero]
                return jnp.concatenate(q_parts, axis=0), jnp.concatenate(k_parts, axis=0)
            pos = row % (2 * m)
            is_q = (pos < m) if reverse else (pos >= m)
            if m == 1:
                return jnp.where(is_q, qs[bb] * jnp.exp(la[bb]), 0.0), None
            if 2 * m >= SUBLANES:
                pref = rows_from(bs[bb], lambda r: r * 2 * m + bnd, 2 * m)
            else:
                p0 = rows_from(bs[bb], lambda r: r * SUBLANES + bnd, SUBLANES)
                p1 = rows_from(bs[bb], lambda r: r * SUBLANES + 2 * m + bnd, SUBLANES)
                pref = jnp.where(row % SUBLANES < 2 * m, p0, p1)
            return (qs[bb] * jnp.exp(jnp.where(is_q, bs[bb] - pref, -jnp.inf)),
                    jnp.exp(jnp.where(is_q, -jnp.inf, pref - bs[bb])))

        level = 0
        m = L // 2
        while m >= 1:
            key_rows = (row % 2 == (1 if reverse else 0)) if m == 1 else None
            for bb in rows:
                q_t, k_mul = level_factors(bb, m)
                add_level(bb, q_t, k_mul, None if m == L // 2 else level, key_rows)
            level += 1
            m //= 2
        for bb in rows:
            add_level(bb, qs[bb], None, level)

        o_inter = [jnp.dot((qs[bb] * jnp.exp(bs[bb])).astype(BF16), sbd_sc[bb], preferred_element_type=F32)
                   for bb in rows]
        for bb in rows:
            o_heads = []
            for h in heads:
                cols = slice(h * HEAD_PAD, (h + 1) * HEAD_PAD)
                o_heads.append(o_inter[bb][:, cols] + jnp.dot(att[bb][h].astype(BF16), vb[bb][:, cols],
                                                              preferred_element_type=F32))
            lay.store(o_ref, i, jnp.concatenate(o_heads, axis=1), (bb,))

        tdot = lambda a: lax.dot_general(a, ones, (((0,), (0,)), ((), ())), preferred_element_type=F32)
        for bb in rows:
            btot = bs[bb][last:last + 1]
            ke_t = (kk[bb] * jnp.exp(btot - bs[bb])).T.astype(BF16)
            dec_col = jnp.exp(tdot(la_split[bb][0]) + tdot(la_split[bb][1]))
            for h in heads:
                krows = slice(h * CK_PAD, (h + 1) * CK_PAD)
                cols = slice(h * HEAD_PAD, (h + 1) * HEAD_PAD)
                upd = jnp.dot(ke_t[krows], vb[bb][:, cols], preferred_element_type=F32)
                s_new = dec_col[krows] * s_sc[bb, h] + upd
                s_sc[bb, h] = s_new
                sbd_sc[bb, krows, cols] = s_new.astype(BF16)
        return carry

    lax.fori_loop(0, lay.group, chunk, 0)

    @pl.when(gidx == lay.ngrp - 1)
    def _():
        sfin_ref[...] = s_sc[...]


def _scan_masks(reverse):
    t = jnp.arange(SEQ_BLOCK)[:, None]
    u = jnp.arange(SEQ_BLOCK)[None, :]
    cmask = ((u >= t) if reverse else (u <= t)).astype(BF16)
    sizes = []
    m = SEQ_BLOCK
    while m >= 1:
        sizes.append(m)
        m //= 2
    lmask = jnp.stack([(t // sz) == (u // sz) for sz in sizes]).astype(F32)
    return cmask, lmask


def _gla_call(st, w2s, b_gk, sinit, lay, reverse):
    bsz = st["qc"].shape[0]
    di = 1 if reverse else 0
    nb = GLA_BATCH
    tile = lambda c: lay.spec(c, lambda b, g: (b,), reverse, batch=nb)
    full = lambda shp: pl.BlockSpec(shp, lambda b, g: (0,) * len(shp))
    state_spec = pl.BlockSpec((nb, C_HEADS, CK_PAD, HEAD_PAD), lambda b, g: (b, 0, 0, 0))
    qc, kc, vc, gk = (lay.view(st[n]) for n in ("qc", "kc", "vc", "gk"))
    cmask, lmask = _scan_masks(reverse)
    o, sfin = pl.pallas_call(
        functools.partial(_gla_kernel, lay, reverse),
        grid=(bsz // nb, lay.ngrp),
        in_specs=[tile(CK_W), tile(CK_W), tile(CV_W),
                  lay.spec(LANES, lambda b, g: (di, b), reverse, batch=nb),
                  pl.BlockSpec((1, LANES, CK_W), lambda b, g: (di, 0, 0)),
                  pl.BlockSpec((1, 1, CK_W), lambda b, g: (di, 0, 0)),
                  full(cmask.shape), full(lmask.shape), state_spec],
        out_specs=[tile(CV_W), state_spec],
        out_shape=[jax.ShapeDtypeStruct(vc.shape, F32),
                   jax.ShapeDtypeStruct(sinit.shape, F32)],
        scratch_shapes=[pltpu.VMEM((nb, C_HEADS, CK_PAD, HEAD_PAD), F32),
                        pltpu.VMEM((nb, CK_W, CV_W), BF16)],
        compiler_params=_cparams(2),
        name="gla_scan_bwd" if reverse else "gla_scan_fwd",
    )(qc, kc, vc, gk, w2s, b_gk, cmask, lmask, sinit)
    return lay.unview(o), sfin


def _mix_out_kernel(x_ref, mod_ref, haf_ref, hab_ref, hcf_ref, hcb_ref, oa_ref, gc_ref, s_ref, na_ref, nc_ref,
                    w_ref, o_ref):
    def head_norm(xh, gain):
        return xh * _rms_scale(xh, A_HEAD_DIM) * gain

    acc = None
    for j in range(B_WIDTH // LANES):
        r0 = A_PAD + j * LANES
        part = jnp.dot(s_ref[0, j].astype(BF16), w_ref[r0:r0 + LANES, :], preferred_element_type=F32)
        acc = part if acc is None else acc + part
    for h in range(A_HEADS):
        sl = slice(h * HEAD_PAD, (h + 1) * HEAD_PAD)
        a = head_norm(haf_ref[0, h] + hab_ref[0, h], na_ref[:, sl])
        c = head_norm(hcf_ref[0, h] + hcb_ref[0, h], nc_ref[:, sl])
        ah = (a * jax.nn.sigmoid(oa_ref[0, h])).astype(BF16)
        ch = (c * _silu(gc_ref[0, h])).astype(BF16)
        acc = acc + jnp.dot(ah, w_ref[h * HEAD_PAD:(h + 1) * HEAD_PAD, :], preferred_element_type=F32)
        c0 = A_PAD + B_WIDTH + h * HEAD_PAD
        acc = acc + jnp.dot(ch, w_ref[c0:c0 + HEAD_PAD, :], preferred_element_type=F32)
    o_ref[0] = x_ref[0] + mod_ref[0, 2:3, :] * acc


def _mix_out_call(x, mod, haf, hab, hcf, hcb, oa, gc, s, na, nc, w, tm):
    bsz, t, d = x.shape
    mod_map = (lambda b, i: (b, 0, 0)) if mod.shape[0] > 1 else (lambda b, i: (0, 0, 0))
    row = lambda c: pl.BlockSpec((1, tm, c), lambda b, i: (b, i, 0))
    chunked = lambda c: pl.BlockSpec((1, c // LANES, tm, LANES), lambda b, i: (b, 0, i, 0))
    full = lambda shp: pl.BlockSpec(shp, lambda b, i: (0,) * len(shp))
    return pl.pallas_call(
        _mix_out_kernel,
        grid=(bsz, t // tm),
        in_specs=[row(d), pl.BlockSpec((1, 6, d), mod_map), chunked(A_PAD), chunked(A_PAD), chunked(CV_W),
                  chunked(CV_W), chunked(A_PAD), chunked(CV_W), chunked(B_WIDTH),
                  full((1, A_PAD)), full((1, CV_W)), _resident(w.shape, 2)],
        out_specs=row(d),
        out_shape=jax.ShapeDtypeStruct(x.shape, F32),
        compiler_params=_cparams(2),
        name="mix_out",
    )(x, mod, haf, hab, hcf, hcb, oa, gc, s, na, nc, w)


def _ffn_kernel(nk, final_norm, x_ref, mod_ref, g_ref, wg_ref, wu_ref, wo_ref, fg_ref, o_ref, h_sc, acc_sc):
    kk = pl.program_id(2)

    @pl.when(kk == 0)
    def _():
        x = x_ref[0]
        y = x * _rms_scale(x, x.shape[-1]) * g_ref[...]
        h_sc[...] = (y * (1.0 + mod_ref[0, 4:5, :]) + mod_ref[0, 3:4, :]).astype(BF16)
        acc_sc[...] = jnp.zeros_like(acc_sc)

    h = h_sc[...]
    gate = jnp.dot(h, wg_ref[...], preferred_element_type=F32)
    up = jnp.dot(h, wu_ref[...], preferred_element_type=F32)
    act = (_silu(gate) * up).astype(BF16)
    acc_sc[...] += jnp.dot(act, wo_ref[...], preferred_element_type=F32)

    @pl.when(kk == nk - 1)
    def _():
        y = x_ref[0] + mod_ref[0, 5:6, :] * acc_sc[...]
        if final_norm:
            y = y * _rms_scale(y, y.shape[-1]) * fg_ref[...]
        o_ref[0] = y


def _ffn_call(x, mod, g, w_in, w_out, final_g, final_norm, tm, nk):
    bsz, t, d = x.shape
    d_ff = w_out.shape[0]
    tk = d_ff // nk
    wmode = {"pipeline_mode": pl.Buffered(1)} if nk == 1 else {}
    mod_map = (lambda b, i, k: (b, 0, 0)) if mod.shape[0] > 1 else (lambda b, i, k: (0, 0, 0))
    return pl.pallas_call(
        functools.partial(_ffn_kernel, nk, final_norm),
        grid=(bsz, t // tm, nk),
        in_specs=[pl.BlockSpec((1, tm, d), lambda b, i, k: (b, i, 0)),
                  pl.BlockSpec((1, 6, d), mod_map),
                  pl.BlockSpec((1, d), lambda b, i, k: (0, 0)),
                  pl.BlockSpec((d, tk), lambda b, i, k: (0, k), **wmode),
                  pl.BlockSpec((d, tk), lambda b, i, k: (0, nk + k), **wmode),
                  pl.BlockSpec((tk, d), lambda b, i, k: (k, 0), **wmode),
                  pl.BlockSpec((1, d), lambda b, i, k: (0, 0))],
        out_specs=pl.BlockSpec((1, tm, d), lambda b, i, k: (b, i, 0)),
        out_shape=jax.ShapeDtypeStruct(x.shape, F32),
        scratch_shapes=[pltpu.VMEM((tm, d), BF16), pltpu.VMEM((tm, d), F32)],
        compiler_params=_cparams(3),
        name="ffn",
    )(x, mod, g, w_in, w_in, w_out, final_g)


def _pad_heads(w, n_heads, dim, pad, axis=-1):
    w = jnp.moveaxis(w, axis, -1)
    lead = w.shape[:-1]
    w = w.reshape(lead + (n_heads, dim))
    w = jnp.pad(w, [(0, 0)] * len(lead) + [(0, 0), (0, pad - dim)])
    return jnp.moveaxis(w.reshape(lead + (n_heads * pad,)), -1, axis)


def _pad_to(w, width):
    return jnp.pad(w, [(0, 0)] * (w.ndim - 1) + [(0, width - w.shape[-1])])


def _prep_w_in(w_in):
    a_w = A_HEADS * A_HEAD_DIM
    ck = C_HEADS * C_HEAD_DK
    cv = C_HEADS * C_HEAD_DV
    sizes = (a_w, a_w, a_w, a_w, N_DIR * 2 * A_HEADS, B_WIDTH, B_WIDTH, ck, ck, cv, cv, N_DIR * C_GATE_RANK)
    pts = [sum(sizes[:i + 1]) for i in range(len(sizes) - 1)]
    qa, ka, va, oa, ga, ub, vb, qc, kc, vc, gc, gkc = jnp.split(w_in, pts, axis=-1)
    pa = lambda w: _pad_heads(w, A_HEADS, A_HEAD_DIM, HEAD_PAD)
    pk = lambda w: _pad_heads(w, C_HEADS, C_HEAD_DK, CK_PAD)
    pv = lambda w: _pad_heads(w, C_HEADS, C_HEAD_DV, HEAD_PAD)
    ng = 2 * A_HEADS
    gk_dir = lambda w: _pad_to(jnp.tile(w, (1, GK_COPIES)), LANES)
    cols = [pa(qa), pa(ka), pa(va), pa(oa),
            _pad_to(ga[:, :ng], LANES), _pad_to(ga[:, ng:], LANES),
            ub, vb, pk(qc), pk(kc), pv(vc), pv(gc),
            gk_dir(gkc[:, :C_GATE_RANK]), gk_dir(gkc[:, C_GATE_RANK:])]
    return jnp.concatenate(cols, axis=-1).astype(BF16)


def _prep_w_gk2(w_gk2):
    w = _pad_heads(w_gk2, C_HEADS, C_HEAD_DK, CK_PAD)
    hi = w.astype(BF16)
    lo = (w - hi.astype(F32)).astype(BF16)
    stacked = jnp.concatenate([hi, lo, hi], axis=1)
    return jnp.pad(stacked, ((0, 0), (0, LANES - GK_COPIES * C_GATE_RANK), (0, 0)))


def _prep_w_out(w_out):
    a_w = A_HEADS * A_HEAD_DIM
    wa, wb, wc = w_out[:a_w], w_out[a_w:a_w + B_WIDTH], w_out[a_w + B_WIDTH:]
    wa = _pad_heads(wa, A_HEADS, A_HEAD_DIM, HEAD_PAD, axis=0)
    wc = _pad_heads(wc, C_HEADS, C_HEAD_DV, HEAD_PAD, axis=0)
    return jnp.concatenate([wa, wb, wc], axis=0).astype(BF16)


def _mixer_scans(st, lw, a_state, c_state, lay):
    qconv, kconv, s = _seq_prep_call(st, lw["conv"], lw["sgu_g"], lw["sgu_w"], lw["sgu_bias"], lay)
    ha, a_fin, hc, c_fin = [], [], [], []
    for di, reverse in enumerate((False, True)):
        h, cfin, mfin = _mlstm_call(qconv, kconv, st, lw["gate_b"], a_state[di][0], a_state[di][1], lay, reverse)
        ha.append(h)
        a_fin.append((cfin, mfin))
        o, sfin = _gla_call(st, lw["w_gk2"], lw["b_gk"], c_state[di], lay, reverse)
        hc.append(o)
        c_fin.append(sfin)
    return ha, hc, s, a_fin, c_fin


def kernel(x, c, ctx, c_ctx, norm1_g, norm2_g, w_ada, b_ada, w_in, mlstm_conv, mlstm_gate_b,
           mlstm_norm_g, gla_w_gk2, gla_b_gk, gla_norm_g, sgu_norm_g, sgu_w, sgu_b, w_out,
           w_ffn_in, w_ffn_out, final_g):
    bsz, seq, d = x.shape
    ctx_len = ctx.shape[1]
    depth = w_in.shape[0]
    n_cond = 2 * SUBLANES
    cond = jnp.zeros((n_cond, d), F32).at[:bsz].set(c).at[bsz].set(c_ctx)
    mods = _ada_call(cond, w_ada, b_ada).reshape(depth, n_cond, 6, d)

    fg = final_g.reshape(1, d)
    x_lat, x_ctx = x, ctx
    for l in range(depth):
        need_ctx = l < depth - 1
        mod_lat = mods[l, :bsz]
        mod_ctx = mods[l, bsz:bsz + 1]
        gb = mlstm_gate_b[l].reshape(N_DIR, 1, 2 * A_HEADS)
        lw = {
            "conv": jnp.stack([_pad_heads(mlstm_conv[l][:, :A_HEADS * A_HEAD_DIM], A_HEADS, A_HEAD_DIM, HEAD_PAD),
                               _pad_heads(mlstm_conv[l][:, A_HEADS * A_HEAD_DIM:], A_HEADS, A_HEAD_DIM, HEAD_PAD)]),
            "gate_b": _pad_to(gb, LANES),
            "w_gk2": _prep_w_gk2(gla_w_gk2[l]),
            "b_gk": _pad_heads(gla_b_gk[l], C_HEADS, C_HEAD_DK, CK_PAD).reshape(N_DIR, 1, CK_W),
            "sgu_g": sgu_norm_g[l].reshape(1, B_WIDTH),
            "sgu_w": sgu_w[l].astype(BF16),
            "sgu_bias": jnp.repeat(sgu_b[l].T, B_GROUP_DIM, axis=1),
        }
        w_in_l = _prep_w_in(w_in[l])
        w_out_l = _prep_w_out(w_out[l])
        na = _pad_heads(mlstm_norm_g[l], A_HEADS, A_HEAD_DIM, HEAD_PAD).reshape(1, A_PAD)
        nc = _pad_heads(jnp.tile(gla_norm_g[l], C_HEADS), C_HEADS, C_HEAD_DV, HEAD_PAD).reshape(1, CV_W)
        g1 = norm1_g[l].reshape(1, d)
        g2 = norm2_g[l].reshape(1, d)
        w_ffn_in_l = w_ffn_in[l].astype(BF16)
        w_ffn_out_l = w_ffn_out[l].astype(BF16)

        st_ctx = _proj_in_call(x_ctx, mod_ctx, g1, w_in_l, tm=_row_tile(ctx_len, PROJ_ROWS))
        st_lat = _proj_in_call(x_lat, mod_lat, g1, w_in_l, tm=_row_tile(seq, PROJ_ROWS))

        a0 = (jnp.zeros((bsz, A_HEADS, HEAD_PAD, 2 * HEAD_PAD), F32), jnp.zeros((bsz, SUBLANES, LANES), F32))
        c0 = jnp.zeros((bsz, C_HEADS, CK_PAD, HEAD_PAD), F32)
        lay_ctx = _SeqLayout(ctx_len, False)
        lay_lat = _SeqLayout(seq, l % 2 == 1)
        ha_c, hc_c, s_c, a_state, c_state = _mixer_scans(st_ctx, lw, (a0, a0), (c0, c0), lay_ctx)
        ha_l, hc_l, s_l, _, _ = _mixer_scans(st_lat, lw, a_state, c_state, lay_lat)

        x_lat = _mix_out_call(x_lat, mod_lat, ha_l[0], ha_l[1], hc_l[0], hc_l[1], st_lat["oa"], st_lat["gc"], s_l,
                              na, nc, w_out_l, tm=_row_tile(seq, MIX_ROWS))
        x_lat = _ffn_call(x_lat, mod_lat, g2, w_ffn_in_l, w_ffn_out_l, fg, not need_ctx,
                          tm=_row_tile(seq, FFN_ROWS), nk=FFN_SPLIT)
        if need_ctx:
            x_ctx = _mix_out_call(x_ctx, mod_ctx, ha_c[0], ha_c[1], hc_c[0], hc_c[1], st_ctx["oa"], st_ctx["gc"], s_c,
                                  na, nc, w_out_l, tm=_row_tile(ctx_len, MIX_ROWS))
            x_ctx = _ffn_call(x_ctx, mod_ctx, g2, w_ffn_in_l, w_ffn_out_l, fg, False,
                              tm=_row_tile(ctx_len, FFN_ROWS), nk=FFN_SPLIT)
    return x_lat
```

```python
import functools
import math

import jax
import jax.numpy as jnp
from jax import lax
from jax.experimental import pallas as pl
from jax.experimental.pallas import tpu as pltpu

F32 = jnp.float32
BF16 = jnp.bfloat16
HIGHEST = lax.Precision.HIGHEST

LANES = 128
SUBLANES = 8
VMEM_LIMIT_BYTES = 56 * 1024 * 1024

GRID_W = 64
RMS_EPS = 1e-6
A_HEADS = 4
A_HEAD_DIM = 96
B_GROUPS = 4
B_GROUP_DIM = 64
B_WIDTH = B_GROUPS * B_GROUP_DIM
C_HEADS = 4
C_HEAD_DK = 48
C_HEAD_DV = 96
C_GATE_RANK = 16
C_GATE_NORM = 16.0
N_DIR = 2

HEAD_PAD = LANES
A_PAD = A_HEADS * HEAD_PAD
CK_PAD = 64
CK_W = C_HEADS * CK_PAD
CV_W = C_HEADS * HEAD_PAD

SEQ_BLOCK = 128
SEQ_GROUP = SUBLANES
MLSTM_BATCH = 2
GLA_BATCH = 2
GK_COPIES = 3

PROJ_ROWS = 512
MIX_ROWS = 512
FFN_ROWS = 512
FFN_SPLIT = 1
ADA_COLS = 1536

_IN_LAYOUT = (("qa", A_PAD, 1), ("ka", A_PAD, 1), ("va", A_PAD, 1), ("oa", A_PAD, 1),
              ("ga", LANES, N_DIR),
              ("ub", B_WIDTH, 1), ("vb", B_WIDTH, 1),
              ("qc", CK_W, 1), ("kc", CK_W, 1), ("vc", CV_W, 1), ("gc", CV_W, 1),
              ("gk", LANES, N_DIR))
N_IN_PAD = sum(w * n for _, w, n in _IN_LAYOUT)
_BF16_STREAMS = ("oa", "gc")


def _cparams(n_axes):
    return pltpu.CompilerParams(dimension_semantics=("arbitrary",) * n_axes,
                                vmem_limit_bytes=VMEM_LIMIT_BYTES)


def _resident(shape, n_grid):
    zeros = (0,) * len(shape)
    maps = {2: lambda a, b: zeros, 3: lambda a, b, c: zeros}
    return pl.BlockSpec(shape, maps[n_grid], pipeline_mode=pl.Buffered(1))


def _row_tile(t_len, want):
    return min(want, t_len)


def _log_sigmoid(x):
    return jnp.minimum(x, 0.0) - jnp.log(1.0 + jnp.exp(-jnp.abs(x)))


def _silu(x):
    return x * jax.nn.sigmoid(x)


def _gelu_tanh(x):
    c = math.sqrt(2.0 / math.pi)
    return 0.5 * x * (1.0 + jnp.tanh(c * (x + 0.044715 * (x * x * x))))


def _rms_scale(x, n):
    return lax.rsqrt(jnp.sum(x * x, axis=-1, keepdims=True) * (1.0 / n) + RMS_EPS)


def _split_hi_lo(x):
    hi = x.astype(BF16)
    lo = (x - hi.astype(F32)).astype(BF16)
    return hi, lo


def _ada_kernel(s_ref, w_ref, b_ref, o_ref):
    s = _silu(s_ref[...])
    o_ref[0] = jnp.dot(s, w_ref[0], precision=HIGHEST, preferred_element_type=F32) + b_ref[0]


def _ada_call(cond, w_ada, b_ada):
    depth, d, n6 = w_ada.shape
    nb = cond.shape[0]
    tn = ADA_COLS
    return pl.pallas_call(
        _ada_kernel,
        grid=(depth, n6 // tn),
        in_specs=[pl.BlockSpec((nb, d), lambda l, j: (0, 0)),
                  pl.BlockSpec((1, d, tn), lambda l, j: (l, 0, j)),
                  pl.BlockSpec((1, 1, tn), lambda l, j: (l, 0, j))],
        out_specs=pl.BlockSpec((1, nb, tn), lambda l, j: (l, 0, j)),
        out_shape=jax.ShapeDtypeStruct((depth, nb, n6), F32),
        compiler_params=_cparams(2),
        name="ada_mod",
    )(cond, w_ada, b_ada.reshape(depth, 1, n6))


def _proj_in_kernel(x_ref, mod_ref, g_ref, w_ref, *out_refs):
    x = x_ref[0]
    y = x * _rms_scale(x, x.shape[-1]) * g_ref[...]
    h = (y * (1.0 + mod_ref[0, 1:2, :]) + mod_ref[0, 0:1, :]).astype(BF16)
    off = 0
    for (_, width, n_dir), o_ref in zip(_IN_LAYOUT, out_refs):
        for di in range(n_dir):
            res = jnp.dot(h, w_ref[:, off:off + width], preferred_element_type=F32)
            for j in range(width // LANES):
                piece = res[:, j * LANES:(j + 1) * LANES]
                if n_dir == 1:
                    o_ref[0, j] = piece.astype(o_ref.dtype)
                else:
                    o_ref[di, 0, j] = piece
            off += width


def _proj_in_call(x, mod, g, w, tm):
    bsz, t, d = x.shape
    mod_b = mod.shape[0]
    mod_map = (lambda b, i: (b, 0, 0)) if mod_b > 1 else (lambda b, i: (0, 0, 0))
    out_shapes, out_specs = [], []
    for name, wd, n_dir in _IN_LAYOUT:
        nch = wd // LANES
        if n_dir == 1:
            dtype = BF16 if name in _BF16_STREAMS else F32
            out_shapes.append(jax.ShapeDtypeStruct((bsz, nch, t, LANES), dtype))
            out_specs.append(pl.BlockSpec((1, nch, tm, LANES), lambda b, i: (b, 0, i, 0)))
        else:
            out_shapes.append(jax.ShapeDtypeStruct((n_dir, bsz, nch, t, LANES), F32))
            out_specs.append(pl.BlockSpec((n_dir, 1, nch, tm, LANES), lambda b, i: (0, b, 0, i, 0)))
    outs = pl.pallas_call(
        _proj_in_kernel,
        grid=(bsz, t // tm),
        in_specs=[pl.BlockSpec((1, tm, d), lambda b, i: (b, i, 0)),
                  pl.BlockSpec((1, 6, d), mod_map),
                  pl.BlockSpec((1, d), lambda b, i: (0, 0)),
                  _resident((d, N_IN_PAD), 2)],
        out_specs=out_specs,
        out_shape=out_shapes,
        compiler_params=_cparams(2),
        name="proj_in",
    )(x, mod, g, w)
    return dict(zip([n for n, _, _ in _IN_LAYOUT], outs))


class _SeqLayout:
    def __init__(self, t_len, column_major):
        self.cm = column_major
        self.t_len = t_len
        self.nblk = t_len // SEQ_BLOCK
        self.group = min(SEQ_GROUP, self.nblk)
        self.ngrp = self.nblk // self.group
        if column_major:
            assert t_len == SEQ_BLOCK * GRID_W and self.group == SEQ_GROUP

    def view(self, a):
        if not self.cm:
            return a
        return a.reshape(a.shape[:-2] + (SEQ_BLOCK, GRID_W, LANES))

    def unview(self, a):
        if not self.cm:
            return a
        return a.reshape(a.shape[:-3] + (self.t_len, LANES))

    def spec(self, c, lead_map, reverse=False, batch=1):
        def grp(g):
            return self.ngrp - 1 - g if reverse else g

        n_lead = len(lead_map(0, 0))
        lead_blk = (1,) * (n_lead - 1) + (batch,)
        nch = c // LANES
        if self.cm:
            return pl.BlockSpec(lead_blk + (nch, SEQ_BLOCK, self.group, LANES),
                                lambda b, g: lead_map(b, g) + (0, 0, grp(g), 0))
        return pl.BlockSpec(lead_blk + (nch, self.group * SEQ_BLOCK, LANES),
                            lambda b, g: lead_map(b, g) + (0, grp(g), 0))

    def halo_specs(self, c):
        nch = c // LANES
        if self.cm:
            shp = (1, nch, SUBLANES, self.group, LANES)
            prev = pl.BlockSpec(shp, lambda b, g: (b, 0, SEQ_BLOCK // SUBLANES - 1, jnp.maximum(g - 1, 0), 0))
            nxt = pl.BlockSpec(shp, lambda b, g: (b, 0, 0, jnp.minimum(g + 1, self.ngrp - 1), 0))
        else:
            per = self.group * SEQ_BLOCK // SUBLANES
            last = self.t_len // SUBLANES - 1
            shp = (1, nch, SUBLANES, LANES)
            prev = pl.BlockSpec(shp, lambda b, g: (b, 0, jnp.maximum(g * per - 1, 0), 0))
            nxt = pl.BlockSpec(shp, lambda b, g: (b, 0, jnp.minimum((g + 1) * per, last), 0))
        return prev, nxt

    @staticmethod
    def _cat(pieces):
        return pieces[0] if len(pieces) == 1 else jnp.concatenate(pieces, axis=1)

    def halo_prev_row(self, ref):
        r = SUBLANES - 1
        if self.cm:
            return self._cat([ref[0, j, r:r + 1, self.group - 1, :] for j in range(ref.shape[1])])
        return self._cat([ref[0, j, r:r + 1, :] for j in range(ref.shape[1])])

    def halo_next_row(self, ref):
        if self.cm:
            return self._cat([ref[0, j, 0:1, 0, :] for j in range(ref.shape[1])])
        return self._cat([ref[0, j, 0:1, :] for j in range(ref.shape[1])])

    def _piece(self, ref, lead, j, i):
        if self.cm:
            flat = ref.at[lead + (j,)].reshape(self.group * SEQ_BLOCK, LANES)
            return flat, (pl.ds(i, SEQ_BLOCK, stride=self.group), slice(None))
        start = i * SEQ_BLOCK
        if not isinstance(i, int):
            start = pl.multiple_of(start, SEQ_BLOCK)
        return ref, lead + (j, pl.ds(start, SEQ_BLOCK), slice(None))

    def load(self, ref, i, lead=(0,)):
        pieces = []
        for j in range(ref.shape[len(lead)]):
            r, idx = self._piece(ref, lead, j, i)
            pieces.append(r[idx])
        return self._cat(pieces)

    def store(self, ref, i, val, lead=(0,)):
        for j in range(ref.shape[len(lead)]):
            r, idx = self._piece(ref, lead, j, i)
            r[idx] = val[:, j * LANES:(j + 1) * LANES]

    def row(self, ref, i, r):
        if self.cm:
            return self._cat([ref[0, j, r:r + 1, i, :] for j in range(ref.shape[1])])
        t = i * SEQ_BLOCK + r
        return self._cat([ref[0, j, t:t + 1, :] for j in range(ref.shape[1])])


def _seq_prep_kernel(lay, q_ref, qp_ref, qn_ref, k_ref, kp_ref, kn_ref, u_ref, v_ref, cw_ref, ng_ref, w_ref,
                     bias_ref, qo_ref, ko_ref, s_ref):
    g = pl.program_id(1)
    L = SEQ_BLOCK
    row = lax.broadcasted_iota(jnp.int32, (L, 1), 0)
    has_prev = (g > 0).astype(F32)
    has_next = (g < lay.ngrp - 1).astype(F32)
    grp = lax.broadcasted_iota(jnp.int32, (1, B_WIDTH), 1) // B_GROUP_DIM

    for i in range(lay.group):
        for x_ref, xp_ref, xn_ref, o_ref, w, scale in ((q_ref, qp_ref, qn_ref, qo_ref, cw_ref[0], A_HEAD_DIM ** -0.5),
                                                       (k_ref, kp_ref, kn_ref, ko_ref, cw_ref[1], None)):
            x = lay.load(x_ref, i)
            prev_row = lay.row(x_ref, i - 1, L - 1) if i > 0 else lay.halo_prev_row(xp_ref) * has_prev
            next_row = lay.row(x_ref, i + 1, 0) if i < lay.group - 1 else lay.halo_next_row(xn_ref) * has_next
            x_dn = jnp.where(row == 0, prev_row, pltpu.roll(x, 1, 0))
            x_up = jnp.where(row == L - 1, next_row, pltpu.roll(x, L - 1, 0))
            y = _silu(w[0:1, :] * x_dn + w[1:2, :] * x + w[2:3, :] * x_up)
            lay.store(o_ref, i, y if scale is None else y * scale)

        u = _gelu_tanh(lay.load(u_ref, i))
        v = _gelu_tanh(lay.load(v_ref, i))
        v = v * _rms_scale(v, B_WIDTH) * ng_ref[...]
        mixed = bias_ref[...]
        for gi in range(B_GROUPS):
            vg = jnp.where(grp == gi, v, 0.0).astype(BF16)
            mixed = mixed + jnp.dot(w_ref[gi], vg, preferred_element_type=F32)
        lay.store(s_ref, i, u * mixed)


def _seq_prep_call(st, conv_w, norm_g, w_s, bias, lay):
    bsz = st["qa"].shape[0]
    lead = lambda b, g: (b,)
    tile = lambda c: lay.spec(c, lead)
    hp, hn = lay.halo_specs(A_PAD)
    full = lambda shp: pl.BlockSpec(shp, lambda b, g: (0,) * len(shp))
    qa, ka, ub, vb = (lay.view(st[n]) for n in ("qa", "ka", "ub", "vb"))
    qo, ko, s = pl.pallas_call(
        functools.partial(_seq_prep_kernel, lay),
        grid=(bsz, lay.ngrp),
        in_specs=[tile(A_PAD), hp, hn, tile(A_PAD), hp, hn, tile(B_WIDTH), tile(B_WIDTH),
                  full((2, 3, A_PAD)), full((1, B_WIDTH)), full((B_GROUPS, SEQ_BLOCK, SEQ_BLOCK)),
                  full((SEQ_BLOCK, B_WIDTH))],
        out_specs=[tile(A_PAD), tile(A_PAD), tile(B_WIDTH)],
        out_shape=[jax.ShapeDtypeStruct(qa.shape, F32), jax.ShapeDtypeStruct(ka.shape, F32),
                   jax.ShapeDtypeStruct(ub.shape, F32)],
        compiler_params=_cparams(2),
        name="seq_prep",
    )(qa, qa, qa, ka, ka, ka, ub, vb, conv_w, norm_g, w_s, bias)
    return lay.unview(qo), lay.unview(ko), lay.unview(s)


def _mlstm_kernel(lay, reverse, q_ref, k_ref, v_ref, g_ref, gb_ref, cmask_ref, spread_ref, rowsel_ref,
                  cinit_ref, minit_ref, *rest):
    other_ref = rest[0] if reverse else None
    h_ref, cfin_ref, mfin_ref, c_sc, m_sc = rest[1:] if reverse else rest
    gidx = pl.program_id(1)
    L = SEQ_BLOCK

    @pl.when(gidx == 0)
    def _():
        c_sc[...] = cinit_ref[...]
        m_sc[...] = minit_ref[...]

    glane = lax.broadcasted_iota(jnp.int32, (1, LANES), 1)
    ri = lax.broadcasted_iota(jnp.int32, (L, L), 0)
    ci = lax.broadcasted_iota(jnp.int32, (L, L), 1)
    past = (ci >= ri) if reverse else (ci <= ri)
    last = 0 if reverse else L - 1
    ones = jnp.ones((L, HEAD_PAD), F32)
    twice = lambda x: jnp.concatenate([x, x], axis=1)

    def chunk(ii, carry):
        i = lay.group - 1 - ii if reverse else ii
        rows = range(MLSTM_BATCH)
        q = [lay.load(q_ref, i, (bb,)).astype(BF16) for bb in rows]
        k = [lay.load(k_ref, i, (bb,)).astype(BF16) for bb in rows]
        v = [lay.load(v_ref, i, (bb,)) for bb in rows]
        g = [lay.load(g_ref, i, (0, bb)) + gb_ref[0] for bb in rows]
        lf = [_split_hi_lo(_log_sigmoid(g[bb])) for bb in rows]
        cs = [jnp.dot(cmask_ref[...], jnp.concatenate(lf[bb], axis=1), preferred_element_type=F32) for bb in rows]
        bc = [cs[bb][:, 0:LANES] + cs[bb][:, LANES:2 * LANES] for bb in rows]
        x = [jnp.concatenate(_split_hi_lo(jnp.where(glane < A_HEADS, g[bb], bc[bb])), axis=1) for bb in rows]
        spread = [jnp.dot(x[bb], spread_ref[...], preferred_element_type=F32) for bb in rows]
        c_rows = [lax.dot_general(rowsel_ref[...], x[bb], (((1,), (1,)), ((), ())), preferred_element_type=F32)
                  for bb in rows]

        units = [(bb, h) for bb in rows for h in range(A_HEADS)]
        each = lambda f: {u: f(*u) for u in units}
        sl = lambda h: slice(h * HEAD_PAD, (h + 1) * HEAD_PAD)
        b_rep = each(lambda bb, h: spread[bb][:, sl(h)])
        c_rep = each(lambda bb, h: spread[bb][:, sl(A_HEADS + h)])
        c_row = each(lambda bb, h: c_rows[bb][h:h + 1, :])
        m_prev = each(lambda bb, h: m_sc[bb, h:h + 1, :])
        c2 = each(lambda bb, h: c_sc[bb, h])
        v2 = each(lambda bb, h: jnp.concatenate([v[bb][:, sl(h)], ones], axis=1))
        u_rep = each(lambda bb, h: jnp.maximum(
            m_prev[bb, h], jnp.max(jnp.where(past, c_row[bb, h], -jnp.inf), axis=1, keepdims=True)))
        qk = each(lambda bb, h: lax.dot_general(q[bb][:, sl(h)], k[bb][:, sl(h)], (((1,), (1,)), ((), ())),
                                                preferred_element_type=F32))
        s = each(lambda bb, h: (qk[bb, h] * jnp.exp(jnp.where(past, c_row[bb, h] - u_rep[bb, h], -jnp.inf))
                                ).astype(BF16))
        inter = each(lambda bb, h: jnp.exp(m_prev[bb, h] - u_rep[bb, h]))
        carry_in = each(lambda bb, h: jnp.dot(q[bb][:, sl(h)], c2[bb, h].astype(BF16), preferred_element_type=F32))
        num2 = each(lambda bb, h: jnp.dot(s[bb, h], v2[bb, h].astype(BF16), preferred_element_type=F32)
                    + twice(inter[bb, h]) * carry_in[bb, h])
        inv = each(lambda bb, h: 1.0 / jnp.maximum(jnp.abs(num2[bb, h][:, HEAD_PAD:]),
                                                   jnp.exp(-(b_rep[bb, h] + u_rep[bb, h]))))
        for bb in rows:
            out = jnp.concatenate([num2[bb, h][:, :HEAD_PAD] * inv[bb, h] for h in range(A_HEADS)], axis=1)
            if other_ref is not None:
                out = out + lay.load(other_ref, i, (bb,))
            lay.store(h_ref, i, out, (bb,))

        b_last = each(lambda bb, h: b_rep[bb, h][last:last + 1, :])
        w_log = each(lambda bb, h: b_last[bb, h] + c_rep[bb, h])
        m_new = each(lambda bb, h: jnp.maximum(b_last[bb, h] + m_prev[bb, h],
                                               jnp.max(w_log[bb, h], axis=0, keepdims=True)))
        wv2 = each(lambda bb, h: (twice(jnp.exp(w_log[bb, h] - m_new[bb, h])) * v2[bb, h]).astype(BF16))
        for bb, h in units:
            upd = lax.dot_general(k[bb][:, sl(h)], wv2[bb, h], (((0,), (0,)), ((), ())), preferred_element_type=F32)
            decay = jnp.exp(b_last[bb, h] + m_prev[bb, h] - m_new[bb, h])
            c_sc[bb, h] = twice(decay) * c2[bb, h] + upd
            m_sc[bb, h:h + 1, :] = m_new[bb, h]
        return carry

    lax.fori_loop(0, lay.group, chunk, 0)

    @pl.when(gidx == lay.ngrp - 1)
    def _():
        cfin_ref[...] = c_sc[...]
        mfin_ref[...] = m_sc[...]


def _mlstm_gate_matrices():
    lane = jnp.arange(LANES)
    blocks = []
    for h in range(A_HEADS):
        blocks.append(jnp.broadcast_to((lane == A_HEADS + h).astype(F32)[:, None], (LANES, LANES)))
    for h in range(A_HEADS):
        col = (lane == h).astype(F32) - (lane == A_HEADS + h).astype(F32)
        blocks.append(jnp.broadcast_to(col[:, None], (LANES, LANES)))
    half = jnp.concatenate(blocks, axis=1)
    spread = jnp.concatenate([half, half], axis=0).astype(BF16)
    sel = jnp.stack([(lane == h).astype(F32) - (lane == A_HEADS + h).astype(F32) for h in range(A_HEADS)]
                    + [jnp.zeros((LANES,), F32)] * (2 * SUBLANES - A_HEADS))
    rowsel = jnp.concatenate([sel, sel], axis=1).astype(BF16)
    return spread, rowsel


def _mlstm_call(qconv, kconv, st, gate_b, cinit, minit, lay, reverse, other=None):
    bsz = qconv.shape[0]
    di = 1 if reverse else 0
    nb = MLSTM_BATCH
    tile = lambda c: lay.spec(c, lambda b, g: (b,), reverse, batch=nb)
    full = lambda shp: pl.BlockSpec(shp, lambda b, g: (0,) * len(shp))
    c_spec = pl.BlockSpec((nb, A_HEADS, HEAD_PAD, 2 * HEAD_PAD), lambda b, g: (b, 0, 0, 0))
    m_spec = pl.BlockSpec((nb, SUBLANES, LANES), lambda b, g: (b, 0, 0))
    qa, ka, va, ga = lay.view(qconv), lay.view(kconv), lay.view(st["va"]), lay.view(st["ga"])
    cmask, _ = _scan_masks(reverse)
    spread, rowsel = _mlstm_gate_matrices()
    h, cfin, mfin = pl.pallas_call(
        functools.partial(_mlstm_kernel, lay, reverse),
        grid=(bsz // nb, lay.ngrp),
        in_specs=[tile(A_PAD), tile(A_PAD), tile(A_PAD),
                  lay.spec(LANES, lambda b, g: (di, b), reverse, batch=nb),
                  pl.BlockSpec((1, 1, LANES), lambda b, g: (di, 0, 0)),
                  full(cmask.shape), full(spread.shape), full(rowsel.shape),
                  c_spec, m_spec] + ([tile(A_PAD)] if reverse else []),
        out_specs=[tile(A_PAD), c_spec, m_spec],
        out_shape=[jax.ShapeDtypeStruct(qa.shape, F32),
                   jax.ShapeDtypeStruct(cinit.shape, F32),
                   jax.ShapeDtypeStruct(minit.shape, F32)],
        scratch_shapes=[pltpu.VMEM((nb, A_HEADS, HEAD_PAD, 2 * HEAD_PAD), F32),
                        pltpu.VMEM((nb, SUBLANES, LANES), F32)],
        compiler_params=_cparams(2),
        name="mlstm_scan_bwd" if reverse else "mlstm_scan_fwd",
    )(qa, ka, va, ga, gate_b, cmask, spread, rowsel, cinit, minit, *([lay.view(other)] if reverse else []))
    return lay.unview(h), cfin, mfin


def _gla_kernel(lay, reverse, q_ref, k_ref, v_ref, gk_ref, w2_ref, bgk_ref, cmask_ref, lmask_ref, sinit_ref,
                *rest):
    other_ref = rest[0] if reverse else None
    o_ref, sfin_ref, s_sc, sbd_sc = rest[1:] if reverse else rest
    gidx = pl.program_id(1)
    L = SEQ_BLOCK

    @pl.when(gidx == 0)
    def _():
        s_sc[...] = sinit_ref[...]
        sbd_sc[...] = jnp.zeros_like(sbd_sc)
        for bb in range(GLA_BATCH):
            for h in range(C_HEADS):
                sbd_sc[bb, h * CK_PAD:(h + 1) * CK_PAD, h * HEAD_PAD:(h + 1) * HEAD_PAD] = \
                    sinit_ref[bb, h].astype(BF16)

    glane = lax.broadcasted_iota(jnp.int32, (1, LANES), 1)
    row = lax.broadcasted_iota(jnp.int32, (L, 1), 0)
    hlane = glane // CK_PAD
    col = lambda x, h: x[:, (h // 2) * LANES:(h // 2 + 1) * LANES]
    last = 0 if reverse else L - 1
    ones = jnp.ones((L, HEAD_PAD), BF16)

    def rows_from(x, idx_of_row_block, rows_per_block):
        n = L // rows_per_block
        return jnp.concatenate([jnp.broadcast_to(x[idx_of_row_block(i):idx_of_row_block(i) + 1],
                                                 (rows_per_block, x.shape[1])) for i in range(n)], axis=0)

    def chunk(ii, carry):
        i = lay.group - 1 - ii if reverse else ii
        rows = range(GLA_BATCH)
        heads = range(C_HEADS)
        g = [lay.load(gk_ref, i, (0, bb)) for bb in rows]
        g_hi = [g[bb].astype(BF16).astype(F32) for bb in rows]
        g_split = [jnp.where(glane < 2 * C_GATE_RANK, g_hi[bb], g[bb] - g_hi[bb]).astype(BF16) for bb in rows]
        la = [_log_sigmoid(jnp.dot(g_split[bb], w2_ref[0], preferred_element_type=F32) + bgk_ref[0])
              * (1.0 / C_GATE_NORM) for bb in rows]
        la_split = [_split_hi_lo(la[bb]) for bb in rows]
        cs = [jnp.dot(cmask_ref[...], jnp.concatenate(la_split[bb], axis=1), preferred_element_type=F32)
              for bb in rows]
        bs = [cs[bb][:, 0:CK_W] + cs[bb][:, CK_W:2 * CK_W] for bb in rows]
        qs = [lay.load(q_ref, i, (bb,)) * (C_HEAD_DK ** -0.5) for bb in rows]
        kk = [lay.load(k_ref, i, (bb,)) for bb in rows]
        vb = [lay.load(v_ref, i, (bb,)).astype(BF16) for bb in rows]
        k_head = [[jnp.where(hlane == h % 2, col(kk[bb], h), 0.0) for h in heads] for bb in rows]
        att = [[None] * C_HEADS for _ in rows]

        def add_level(bb, q_t, k_mul, level, k_rows=None):
            q_b = q_t.astype(BF16)
            for h in heads:
                k_t = k_head[bb][h] if k_mul is None else k_head[bb][h] * col(k_mul, h)
                if k_rows is not None:
                    k_t = jnp.where(k_rows, k_t, 0.0)
                a = lax.dot_general(col(q_b, h), k_t.astype(BF16), (((1,), (1,)), ((), ())),
                                    preferred_element_type=F32)
                if level is not None:
                    a = a * lmask_ref[level]
                att[bb][h] = a if att[bb][h] is None else att[bb][h] + a

        def level_factors(bb, m):
            bnd = m if reverse else m - 1
            if m >= SUBLANES:
                zero = jnp.zeros((m, CK_W), F32)
                q_parts, k_parts = [], []
                for base in range(0, L, 2 * m):
                    pref = bs[bb][base + bnd:base + bnd + 1]
                    lo, hi = slice(base, base + m), slice(base + m, base + 2 * m)
                    q_half, k_half = (lo, hi) if reverse else (hi, lo)
                    q_piece = qs[bb][q_half] * jnp.exp(bs[bb][q_half] - pref)
                    k_piece = jnp.exp(pref - bs[bb][k_half])
                    q_parts += [q_piece, zero] if reverse else [zero, q_piece]
                    k_parts += [zero, k_piece] if reverse else [k_piece, zero]
                return jnp.concatenate(q_parts, axis=0), jnp.concatenate(k_parts, axis=0)
            pos = row % (2 * m)
            is_q = (pos < m) if reverse else (pos >= m)
            if m == 1:
                return jnp.where(is_q, qs[bb] * jnp.exp(la[bb]), 0.0), None
            if 2 * m >= SUBLANES:
                pref = rows_from(bs[bb], lambda r: r * 2 * m + bnd, 2 * m)
            else:
                p0 = rows_from(bs[bb], lambda r: r * SUBLANES + bnd, SUBLANES)
                p1 = rows_from(bs[bb], lambda r: r * SUBLANES + 2 * m + bnd, SUBLANES)
                pref = jnp.where(row % SUBLANES < 2 * m, p0, p1)
            return (qs[bb] * jnp.exp(jnp.where(is_q, bs[bb] - pref, -jnp.inf)),
                    jnp.exp(jnp.where(is_q, -jnp.inf, pref - bs[bb])))

        level = 0
        m = L // 2
        while m >= 1:
            key_rows = (row % 2 == (1 if reverse else 0)) if m == 1 else None
            for bb in rows:
                q_t, k_mul = level_factors(bb, m)
                add_level(bb, q_t, k_mul, None if m == L // 2 else level, key_rows)
            level += 1
            m //= 2
        for bb in rows:
            add_level(bb, qs[bb], None, level)

        o_inter = [jnp.dot((qs[bb] * jnp.exp(bs[bb])).astype(BF16), sbd_sc[bb], preferred_element_type=F32)
                   for bb in rows]
        for bb in rows:
            o_heads = []
            for h in heads:
                cols = slice(h * HEAD_PAD, (h + 1) * HEAD_PAD)
                o_heads.append(o_inter[bb][:, cols] + jnp.dot(att[bb][h].astype(BF16), vb[bb][:, cols],
                                                              preferred_element_type=F32))
            out = jnp.concatenate(o_heads, axis=1)
            if other_ref is not None:
                out = out + lay.load(other_ref, i, (bb,))
            lay.store(o_ref, i, out, (bb,))

        tdot = lambda a: lax.dot_general(a, ones, (((0,), (0,)), ((), ())), preferred_element_type=F32)
        for bb in rows:
            btot = bs[bb][last:last + 1]
            ke_t = (kk[bb] * jnp.exp(btot - bs[bb])).T.astype(BF16)
            dec_col = jnp.exp(tdot(la_split[bb][0]) + tdot(la_split[bb][1]))
            for h in heads:
                krows = slice(h * CK_PAD, (h + 1) * CK_PAD)
                cols = slice(h * HEAD_PAD, (h + 1) * HEAD_PAD)
                upd = jnp.dot(ke_t[krows], vb[bb][:, cols], preferred_element_type=F32)
                s_new = dec_col[krows] * s_sc[bb, h] + upd
                s_sc[bb, h] = s_new
                sbd_sc[bb, krows, cols] = s_new.astype(BF16)
        return carry

    lax.fori_loop(0, lay.group, chunk, 0)

    @pl.when(gidx == lay.ngrp - 1)
    def _():
        sfin_ref[...] = s_sc[...]


def _scan_masks(reverse):
    t = jnp.arange(SEQ_BLOCK)[:, None]
    u = jnp.arange(SEQ_BLOCK)[None, :]
    cmask = ((u >= t) if reverse else (u <= t)).astype(BF16)
    sizes = []
    m = SEQ_BLOCK
    while m >= 1:
        sizes.append(m)
        m //= 2
    lmask = jnp.stack([(t // sz) == (u // sz) for sz in sizes]).astype(F32)
    return cmask, lmask


def _gla_call(st, w2s, b_gk, sinit, lay, reverse, other=None):
    bsz = st["qc"].shape[0]
    di = 1 if reverse else 0
    nb = GLA_BATCH
    tile = lambda c: lay.spec(c, lambda b, g: (b,), reverse, batch=nb)
    full = lambda shp: pl.BlockSpec(shp, lambda b, g: (0,) * len(shp))
    state_spec = pl.BlockSpec((nb, C_HEADS, CK_PAD, HEAD_PAD), lambda b, g: (b, 0, 0, 0))
    qc, kc, vc, gk = (lay.view(st[n]) for n in ("qc", "kc", "vc", "gk"))
    cmask, lmask = _scan_masks(reverse)
    o, sfin = pl.pallas_call(
        functools.partial(_gla_kernel, lay, reverse),
        grid=(bsz // nb, lay.ngrp),
        in_specs=[tile(CK_W), tile(CK_W), tile(CV_W),
                  lay.spec(LANES, lambda b, g: (di, b), reverse, batch=nb),
                  pl.BlockSpec((1, LANES, CK_W), lambda b, g: (di, 0, 0)),
                  pl.BlockSpec((1, 1, CK_W), lambda b, g: (di, 0, 0)),
                  full(cmask.shape), full(lmask.shape), state_spec] + ([tile(CV_W)] if reverse else []),
        out_specs=[tile(CV_W), state_spec],
        out_shape=[jax.ShapeDtypeStruct(vc.shape, F32),
                   jax.ShapeDtypeStruct(sinit.shape, F32)],
        scratch_shapes=[pltpu.VMEM((nb, C_HEADS, CK_PAD, HEAD_PAD), F32),
                        pltpu.VMEM((nb, CK_W, CV_W), BF16)],
        compiler_params=_cparams(2),
        name="gla_scan_bwd" if reverse else "gla_scan_fwd",
    )(qc, kc, vc, gk, w2s, b_gk, cmask, lmask, sinit, *([lay.view(other)] if reverse else []))
    return lay.unview(o), sfin


def _mix_out_kernel(x_ref, mod_ref, ha_ref, hc_ref, oa_ref, gc_ref, s_ref, na_ref, nc_ref, w_ref, o_ref):
    def head_norm(xh, gain):
        return xh * _rms_scale(xh, A_HEAD_DIM) * gain

    acc = None
    for j in range(B_WIDTH // LANES):
        r0 = A_PAD + j * LANES
        part = jnp.dot(s_ref[0, j].astype(BF16), w_ref[r0:r0 + LANES, :], preferred_element_type=F32)
        acc = part if acc is None else acc + part
    for h in range(A_HEADS):
        sl = slice(h * HEAD_PAD, (h + 1) * HEAD_PAD)
        a = head_norm(ha_ref[0, h], na_ref[:, sl])
        c = head_norm(hc_ref[0, h], nc_ref[:, sl])
        ah = (a * jax.nn.sigmoid(oa_ref[0, h].astype(F32))).astype(BF16)
        ch = (c * _silu(gc_ref[0, h].astype(F32))).astype(BF16)
        acc = acc + jnp.dot(ah, w_ref[h * HEAD_PAD:(h + 1) * HEAD_PAD, :], preferred_element_type=F32)
        c0 = A_PAD + B_WIDTH + h * HEAD_PAD
        acc = acc + jnp.dot(ch, w_ref[c0:c0 + HEAD_PAD, :], preferred_element_type=F32)
    o_ref[0] = x_ref[0] + mod_ref[0, 2:3, :] * acc


def _mix_out_call(x, mod, ha, hc, oa, gc, s, na, nc, w, tm):
    bsz, t, d = x.shape
    mod_map = (lambda b, i: (b, 0, 0)) if mod.shape[0] > 1 else (lambda b, i: (0, 0, 0))
    row = lambda c: pl.BlockSpec((1, tm, c), lambda b, i: (b, i, 0))
    chunked = lambda c: pl.BlockSpec((1, c // LANES, tm, LANES), lambda b, i: (b, 0, i, 0))
    full = lambda shp: pl.BlockSpec(shp, lambda b, i: (0,) * len(shp))
    return pl.pallas_call(
        _mix_out_kernel,
        grid=(bsz, t // tm),
        in_specs=[row(d), pl.BlockSpec((1, 6, d), mod_map), chunked(A_PAD), chunked(CV_W),
                  chunked(A_PAD), chunked(CV_W), chunked(B_WIDTH),
                  full((1, A_PAD)), full((1, CV_W)), _resident(w.shape, 2)],
        out_specs=row(d),
        out_shape=jax.ShapeDtypeStruct(x.shape, F32),
        compiler_params=_cparams(2),
        name="mix_out",
    )(x, mod, ha, hc, oa, gc, s, na, nc, w)


def _ffn_kernel(nk, final_norm, x_ref, mod_ref, g_ref, wg_ref, wu_ref, wo_ref, fg_ref, o_ref, h_sc, acc_sc):
    kk = pl.program_id(2)

    @pl.when(kk == 0)
    def _():
        x = x_ref[0]
        y = x * _rms_scale(x, x.shape[-1]) * g_ref[...]
        h_sc[...] = (y * (1.0 + mod_ref[0, 4:5, :]) + mod_ref[0, 3:4, :]).astype(BF16)
        acc_sc[...] = jnp.zeros_like(acc_sc)

    h = h_sc[...]
    gate = jnp.dot(h, wg_ref[...], preferred_element_type=F32)
    up = jnp.dot(h, wu_ref[...], preferred_element_type=F32)
    act = (_silu(gate) * up).astype(BF16)
    acc_sc[...] += jnp.dot(act, wo_ref[...], preferred_element_type=F32)

    @pl.when(kk == nk - 1)
    def _():
        y = x_ref[0] + mod_ref[0, 5:6, :] * acc_sc[...]
        if final_norm:
            y = y * _rms_scale(y, y.shape[-1]) * fg_ref[...]
        o_ref[0] = y


def _ffn_call(x, mod, g, w_in, w_out, final_g, final_norm, tm, nk):
    bsz, t, d = x.shape
    d_ff = w_out.shape[0]
    tk = d_ff // nk
    wmode = {"pipeline_mode": pl.Buffered(1)} if nk == 1 else {}
    mod_map = (lambda b, i, k: (b, 0, 0)) if mod.shape[0] > 1 else (lambda b, i, k: (0, 0, 0))
    return pl.pallas_call(
        functools.partial(_ffn_kernel, nk, final_norm),
        grid=(bsz, t // tm, nk),
        in_specs=[pl.BlockSpec((1, tm, d), lambda b, i, k: (b, i, 0)),
                  pl.BlockSpec((1, 6, d), mod_map),
                  pl.BlockSpec((1, d), lambda b, i, k: (0, 0)),
                  pl.BlockSpec((d, tk), lambda b, i, k: (0, k), **wmode),
                  pl.BlockSpec((d, tk), lambda b, i, k: (0, nk + k), **wmode),
                  pl.BlockSpec((tk, d), lambda b, i, k: (k, 0), **wmode),
                  pl.BlockSpec((1, d), lambda b, i, k: (0, 0))],
        out_specs=pl.BlockSpec((1, tm, d), lambda b, i, k: (b, i, 0)),
        out_shape=jax.ShapeDtypeStruct(x.shape, F32),
        scratch_shapes=[pltpu.VMEM((tm, d), BF16), pltpu.VMEM((tm, d), F32)],
        compiler_params=_cparams(3),
        name="ffn",
    )(x, mod, g, w_in, w_in, w_out, final_g)


def _pad_heads(w, n_heads, dim, pad, axis=-1):
    w = jnp.moveaxis(w, axis, -1)
    lead = w.shape[:-1]
    w = w.reshape(lead + (n_heads, dim))
    w = jnp.pad(w, [(0, 0)] * len(lead) + [(0, 0), (0, pad - dim)])
    return jnp.moveaxis(w.reshape(lead + (n_heads * pad,)), -1, axis)


def _pad_to(w, width):
    return jnp.pad(w, [(0, 0)] * (w.ndim - 1) + [(0, width - w.shape[-1])])


def _prep_w_in(w_in):
    a_w = A_HEADS * A_HEAD_DIM
    ck = C_HEADS * C_HEAD_DK
    cv = C_HEADS * C_HEAD_DV
    sizes = (a_w, a_w, a_w, a_w, N_DIR * 2 * A_HEADS, B_WIDTH, B_WIDTH, ck, ck, cv, cv, N_DIR * C_GATE_RANK)
    pts = [sum(sizes[:i + 1]) for i in range(len(sizes) - 1)]
    qa, ka, va, oa, ga, ub, vb, qc, kc, vc, gc, gkc = jnp.split(w_in, pts, axis=-1)
    pa = lambda w: _pad_heads(w, A_HEADS, A_HEAD_DIM, HEAD_PAD)
    pk = lambda w: _pad_heads(w, C_HEADS, C_HEAD_DK, CK_PAD)
    pv = lambda w: _pad_heads(w, C_HEADS, C_HEAD_DV, HEAD_PAD)
    ng = 2 * A_HEADS
    gk_dir = lambda w: _pad_to(jnp.tile(w, (1, GK_COPIES)), LANES)
    cols = [pa(qa), pa(ka), pa(va), pa(oa),
            _pad_to(ga[:, :ng], LANES), _pad_to(ga[:, ng:], LANES),
            ub, vb, pk(qc), pk(kc), pv(vc), pv(gc),
            gk_dir(gkc[:, :C_GATE_RANK]), gk_dir(gkc[:, C_GATE_RANK:])]
    return jnp.concatenate(cols, axis=-1).astype(BF16)


def _prep_w_gk2(w_gk2):
    w = _pad_heads(w_gk2, C_HEADS, C_HEAD_DK, CK_PAD)
    hi = w.astype(BF16)
    lo = (w - hi.astype(F32)).astype(BF16)
    stacked = jnp.concatenate([hi, lo, hi], axis=1)
    return jnp.pad(stacked, ((0, 0), (0, LANES - GK_COPIES * C_GATE_RANK), (0, 0)))


def _prep_w_out(w_out):
    a_w = A_HEADS * A_HEAD_DIM
    wa, wb, wc = w_out[:a_w], w_out[a_w:a_w + B_WIDTH], w_out[a_w + B_WIDTH:]
    wa = _pad_heads(wa, A_HEADS, A_HEAD_DIM, HEAD_PAD, axis=0)
    wc = _pad_heads(wc, C_HEADS, C_HEAD_DV, HEAD_PAD, axis=0)
    return jnp.concatenate([wa, wb, wc], axis=0).astype(BF16)


def _mixer_scans(st, lw, a_state, c_state, lay):
    qconv, kconv, s = _seq_prep_call(st, lw["conv"], lw["sgu_g"], lw["sgu_w"], lw["sgu_bias"], lay)
    ha, hc, a_fin, c_fin = None, None, [], []
    for di, reverse in enumerate((False, True)):
        ha, cfin, mfin = _mlstm_call(qconv, kconv, st, lw["gate_b"], a_state[di][0], a_state[di][1], lay, reverse, ha)
        a_fin.append((cfin, mfin))
        hc, sfin = _gla_call(st, lw["w_gk2"], lw["b_gk"], c_state[di], lay, reverse, hc)
        c_fin.append(sfin)
    return ha, hc, s, a_fin, c_fin


def kernel(x, c, ctx, c_ctx, norm1_g, norm2_g, w_ada, b_ada, w_in, mlstm_conv, mlstm_gate_b,
           mlstm_norm_g, gla_w_gk2, gla_b_gk, gla_norm_g, sgu_norm_g, sgu_w, sgu_b, w_out,
           w_ffn_in, w_ffn_out, final_g):
    bsz, seq, d = x.shape
    ctx_len = ctx.shape[1]
    depth = w_in.shape[0]
    n_cond = 2 * SUBLANES
    cond = jnp.zeros((n_cond, d), F32).at[:bsz].set(c).at[bsz].set(c_ctx)
    mods = _ada_call(cond, w_ada, b_ada).reshape(depth, n_cond, 6, d)

    fg = final_g.reshape(1, d)
    x_lat, x_ctx = x, ctx
    for l in range(depth):
        need_ctx = l < depth - 1
        mod_lat = mods[l, :bsz]
        mod_ctx = mods[l, bsz:bsz + 1]
        gb = mlstm_gate_b[l].reshape(N_DIR, 1, 2 * A_HEADS)
        lw = {
            "conv": jnp.stack([_pad_heads(mlstm_conv[l][:, :A_HEADS * A_HEAD_DIM], A_HEADS, A_HEAD_DIM, HEAD_PAD),
                               _pad_heads(mlstm_conv[l][:, A_HEADS * A_HEAD_DIM:], A_HEADS, A_HEAD_DIM, HEAD_PAD)]),
            "gate_b": _pad_to(gb, LANES),
            "w_gk2": _prep_w_gk2(gla_w_gk2[l]),
            "b_gk": _pad_heads(gla_b_gk[l], C_HEADS, C_HEAD_DK, CK_PAD).reshape(N_DIR, 1, CK_W),
            "sgu_g": sgu_norm_g[l].reshape(1, B_WIDTH),
            "sgu_w": sgu_w[l].astype(BF16),
            "sgu_bias": jnp.repeat(sgu_b[l].T, B_GROUP_DIM, axis=1),
        }
        w_in_l = _prep_w_in(w_in[l])
        w_out_l = _prep_w_out(w_out[l])
        na = _pad_heads(mlstm_norm_g[l], A_HEADS, A_HEAD_DIM, HEAD_PAD).reshape(1, A_PAD)
        nc = _pad_heads(jnp.tile(gla_norm_g[l], C_HEADS), C_HEADS, C_HEAD_DV, HEAD_PAD).reshape(1, CV_W)
        g1 = norm1_g[l].reshape(1, d)
        g2 = norm2_g[l].reshape(1, d)
        w_ffn_in_l = w_ffn_in[l].astype(BF16)
        w_ffn_out_l = w_ffn_out[l].astype(BF16)

        st_ctx = _proj_in_call(x_ctx, mod_ctx, g1, w_in_l, tm=_row_tile(ctx_len, PROJ_ROWS))
        st_lat = _proj_in_call(x_lat, mod_lat, g1, w_in_l, tm=_row_tile(seq, PROJ_ROWS))

        a0 = (jnp.zeros((bsz, A_HEADS, HEAD_PAD, 2 * HEAD_PAD), F32), jnp.zeros((bsz, SUBLANES, LANES), F32))
        c0 = jnp.zeros((bsz, C_HEADS, CK_PAD, HEAD_PAD), F32)
        lay_ctx = _SeqLayout(ctx_len, False)
        lay_lat = _SeqLayout(seq, l % 2 == 1)
        ha_c, hc_c, s_c, a_state, c_state = _mixer_scans(st_ctx, lw, (a0, a0), (c0, c0), lay_ctx)
        ha_l, hc_l, s_l, _, _ = _mixer_scans(st_lat, lw, a_state, c_state, lay_lat)

        x_lat = _mix_out_call(x_lat, mod_lat, ha_l, hc_l, st_lat["oa"], st_lat["gc"], s_l,
                              na, nc, w_out_l, tm=_row_tile(seq, MIX_ROWS))
        x_lat = _ffn_call(x_lat, mod_lat, g2, w_ffn_in_l, w_ffn_out_l, fg, not need_ctx,
                          tm=_row_tile(seq, FFN_ROWS), nk=FFN_SPLIT)
        if need_ctx:
            x_ctx = _mix_out_call(x_ctx, mod_ctx, ha_c, hc_c, st_ctx["oa"], st_ctx["gc"], s_c,
                                  na, nc, w_out_l, tm=_row_tile(ctx_len, MIX_ROWS))
            x_ctx = _ffn_call(x_ctx, mod_ctx, g2, w_ffn_in_l, w_ffn_out_l, fg, False,
                              tm=_row_tile(ctx_len, FFN_ROWS), nk=FFN_SPLIT)
    return x_lat
```

```python
import functools
import math

import jax
import jax.numpy as jnp
from jax import lax
from jax.experimental import pallas as pl
from jax.experimental.pallas import tpu as pltpu

F32 = jnp.float32
BF16 = jnp.bfloat16
HIGHEST = lax.Precision.HIGHEST

LANES = 128
SUBLANES = 8
VMEM_LIMIT_BYTES = 56 * 1024 * 1024

GRID_W = 64
RMS_EPS = 1e-6
A_HEADS = 4
A_HEAD_DIM = 96
B_GROUPS = 4
B_GROUP_DIM = 64
B_WIDTH = B_GROUPS * B_GROUP_DIM
C_HEADS = 4
C_HEAD_DK = 48
C_HEAD_DV = 96
C_GATE_RANK = 16
C_GATE_NORM = 16.0
N_DIR = 2

HEAD_PAD = LANES
A_PAD = A_HEADS * HEAD_PAD
CK_PAD = 64
CK_W = C_HEADS * CK_PAD
CV_W = C_HEADS * HEAD_PAD

SEQ_BLOCK = 128
SEQ_GROUP = SUBLANES
MLSTM_BATCH = 2
GLA_BATCH = 2
GK_COPIES = 3
GA_OFF = lambda d: 2 * A_HEADS * d
GK_OFF = lambda d: N_DIR * 2 * A_HEADS + GK_COPIES * C_GATE_RANK * d

PROJ_ROWS = 512
MIX_ROWS = 512
FFN_ROWS = 512
FFN_SPLIT = 1
ADA_COLS = 1536

_IN_LAYOUT = (("qa", A_PAD), ("ka", A_PAD), ("va", A_PAD), ("oa", A_PAD), ("gt", LANES),
              ("ub", B_WIDTH), ("vb", B_WIDTH), ("qc", CK_W), ("kc", CK_W), ("vc", CV_W), ("gc", CV_W))
N_IN_PAD = sum(w for _, w in _IN_LAYOUT)
_BF16_STREAMS = ("oa", "gc")


def _cparams(n_axes):
    return pltpu.CompilerParams(dimension_semantics=("arbitrary",) * n_axes,
                                vmem_limit_bytes=VMEM_LIMIT_BYTES)


def _resident(shape, n_grid):
    zeros = (0,) * len(shape)
    maps = {2: lambda a, b: zeros, 3: lambda a, b, c: zeros}
    return pl.BlockSpec(shape, maps[n_grid], pipeline_mode=pl.Buffered(1))


def _row_tile(t_len, want):
    return min(want, t_len)


def _log_sigmoid(x):
    return jnp.minimum(x, 0.0) - jnp.log(1.0 + jnp.exp(-jnp.abs(x)))


def _silu(x):
    return x * jax.nn.sigmoid(x)


def _gelu_tanh(x):
    c = math.sqrt(2.0 / math.pi)
    return 0.5 * x * (1.0 + jnp.tanh(c * (x + 0.044715 * (x * x * x))))


def _rms_scale(x, n):
    return lax.rsqrt(jnp.sum(x * x, axis=-1, keepdims=True) * (1.0 / n) + RMS_EPS)


def _split_hi_lo(x):
    hi = x.astype(BF16)
    lo = (x - hi.astype(F32)).astype(BF16)
    return hi, lo


def _ada_kernel(s_ref, w_ref, b_ref, o_ref):
    s = _silu(s_ref[...])
    o_ref[0] = jnp.dot(s, w_ref[0], precision=HIGHEST, preferred_element_type=F32) + b_ref[0]


def _ada_call(cond, w_ada, b_ada):
    depth, d, n6 = w_ada.shape
    nb = cond.shape[0]
    tn = ADA_COLS
    return pl.pallas_call(
        _ada_kernel,
        grid=(depth, n6 // tn),
        in_specs=[pl.BlockSpec((nb, d), lambda l, j: (0, 0)),
                  pl.BlockSpec((1, d, tn), lambda l, j: (l, 0, j)),
                  pl.BlockSpec((1, 1, tn), lambda l, j: (l, 0, j))],
        out_specs=pl.BlockSpec((1, nb, tn), lambda l, j: (l, 0, j)),
        out_shape=jax.ShapeDtypeStruct((depth, nb, n6), F32),
        compiler_params=_cparams(2),
        name="ada_mod",
    )(cond, w_ada, b_ada.reshape(depth, 1, n6))


def _proj_in_kernel(x_ref, mod_ref, g_ref, w_ref, *out_refs):
    x = x_ref[0]
    y = x * _rms_scale(x, x.shape[-1]) * g_ref[...]
    h = (y * (1.0 + mod_ref[0, 1:2, :]) + mod_ref[0, 0:1, :]).astype(BF16)
    off = 0
    for (_, width), o_ref in zip(_IN_LAYOUT, out_refs):
        res = jnp.dot(h, w_ref[:, off:off + width], preferred_element_type=F32)
        for j in range(width // LANES):
            o_ref[0, j] = res[:, j * LANES:(j + 1) * LANES].astype(o_ref.dtype)
        off += width


def _proj_in_call(x, mod, g, w, tm):
    bsz, t, d = x.shape
    mod_b = mod.shape[0]
    mod_map = (lambda b, i: (b, 0, 0)) if mod_b > 1 else (lambda b, i: (0, 0, 0))
    out_shapes, out_specs = [], []
    for name, wd in _IN_LAYOUT:
        nch = wd // LANES
        out_shapes.append(jax.ShapeDtypeStruct((bsz, nch, t, LANES), BF16 if name in _BF16_STREAMS else F32))
        out_specs.append(pl.BlockSpec((1, nch, tm, LANES), lambda b, i: (b, 0, i, 0)))
    outs = pl.pallas_call(
        _proj_in_kernel,
        grid=(bsz, t // tm),
        in_specs=[pl.BlockSpec((1, tm, d), lambda b, i: (b, i, 0)),
                  pl.BlockSpec((1, 6, d), mod_map),
                  pl.BlockSpec((1, d), lambda b, i: (0, 0)),
                  _resident((d, N_IN_PAD), 2)],
        out_specs=out_specs,
        out_shape=out_shapes,
        compiler_params=_cparams(2),
        name="proj_in",
    )(x, mod, g, w)
    return dict(zip([n for n, _ in _IN_LAYOUT], outs))


class _SeqLayout:
    def __init__(self, t_len, column_major):
        self.cm = column_major
        self.t_len = t_len
        self.nblk = t_len // SEQ_BLOCK
        self.group = min(SEQ_GROUP, self.nblk)
        self.ngrp = self.nblk // self.group
        if column_major:
            assert t_len == SEQ_BLOCK * GRID_W and self.group == SEQ_GROUP

    def view(self, a):
        if not self.cm:
            return a
        return a.reshape(a.shape[:-2] + (SEQ_BLOCK, GRID_W, LANES))

    def unview(self, a):
        if not self.cm:
            return a
        return a.reshape(a.shape[:-3] + (self.t_len, LANES))

    def spec(self, c, lead_map, reverse=False, batch=1):
        def grp(g):
            return self.ngrp - 1 - g if reverse else g

        n_lead = len(lead_map(0, 0))
        lead_blk = (1,) * (n_lead - 1) + (batch,)
        nch = c // LANES
        if self.cm:
            return pl.BlockSpec(lead_blk + (nch, SEQ_BLOCK, self.group, LANES),
                                lambda b, g: lead_map(b, g) + (0, 0, grp(g), 0))
        return pl.BlockSpec(lead_blk + (nch, self.group * SEQ_BLOCK, LANES),
                            lambda b, g: lead_map(b, g) + (0, grp(g), 0))

    def halo_specs(self, c):
        nch = c // LANES
        if self.cm:
            shp = (1, nch, SUBLANES, self.group, LANES)
            prev = pl.BlockSpec(shp, lambda b, g: (b, 0, SEQ_BLOCK // SUBLANES - 1, jnp.maximum(g - 1, 0), 0))
            nxt = pl.BlockSpec(shp, lambda b, g: (b, 0, 0, jnp.minimum(g + 1, self.ngrp - 1), 0))
        else:
            per = self.group * SEQ_BLOCK // SUBLANES
            last = self.t_len // SUBLANES - 1
            shp = (1, nch, SUBLANES, LANES)
            prev = pl.BlockSpec(shp, lambda b, g: (b, 0, jnp.maximum(g * per - 1, 0), 0))
            nxt = pl.BlockSpec(shp, lambda b, g: (b, 0, jnp.minimum((g + 1) * per, last), 0))
        return prev, nxt

    @staticmethod
    def _cat(pieces):
        return pieces[0] if len(pieces) == 1 else jnp.concatenate(pieces, axis=1)

    def halo_prev_row(self, ref):
        r = SUBLANES - 1
        if self.cm:
            return self._cat([ref[0, j, r:r + 1, self.group - 1, :] for j in range(ref.shape[1])])
        return self._cat([ref[0, j, r:r + 1, :] for j in range(ref.shape[1])])

    def halo_next_row(self, ref):
        if self.cm:
            return self._cat([ref[0, j, 0:1, 0, :] for j in range(ref.shape[1])])
        return self._cat([ref[0, j, 0:1, :] for j in range(ref.shape[1])])

    def _piece(self, ref, lead, j, i):
        if self.cm:
            flat = ref.at[lead + (j,)].reshape(self.group * SEQ_BLOCK, LANES)
            return flat, (pl.ds(i, SEQ_BLOCK, stride=self.group), slice(None))
        start = i * SEQ_BLOCK
        if not isinstance(i, int):
            start = pl.multiple_of(start, SEQ_BLOCK)
        return ref, lead + (j, pl.ds(start, SEQ_BLOCK), slice(None))

    def load(self, ref, i, lead=(0,)):
        pieces = []
        for j in range(ref.shape[len(lead)]):
            r, idx = self._piece(ref, lead, j, i)
            pieces.append(r[idx])
        return self._cat(pieces)

    def store(self, ref, i, val, lead=(0,)):
        for j in range(ref.shape[len(lead)]):
            r, idx = self._piece(ref, lead, j, i)
            r[idx] = val[:, j * LANES:(j + 1) * LANES]

    def row(self, ref, i, r):
        if self.cm:
            return self._cat([ref[0, j, r:r + 1, i, :] for j in range(ref.shape[1])])
        t = i * SEQ_BLOCK + r
        return self._cat([ref[0, j, t:t + 1, :] for j in range(ref.shape[1])])


def _seq_prep_kernel(lay, q_ref, qp_ref, qn_ref, k_ref, kp_ref, kn_ref, u_ref, v_ref, cw_ref, ng_ref, w_ref,
                     bias_ref, qo_ref, ko_ref, s_ref):
    g = pl.program_id(1)
    L = SEQ_BLOCK
    row = lax.broadcasted_iota(jnp.int32, (L, 1), 0)
    has_prev = (g > 0).astype(F32)
    has_next = (g < lay.ngrp - 1).astype(F32)
    grp = lax.broadcasted_iota(jnp.int32, (1, B_WIDTH), 1) // B_GROUP_DIM

    for i in range(lay.group):
        for x_ref, xp_ref, xn_ref, o_ref, w, scale in ((q_ref, qp_ref, qn_ref, qo_ref, cw_ref[0], A_HEAD_DIM ** -0.5),
                                                       (k_ref, kp_ref, kn_ref, ko_ref, cw_ref[1], None)):
            x = lay.load(x_ref, i)
            prev_row = lay.row(x_ref, i - 1, L - 1) if i > 0 else lay.halo_prev_row(xp_ref) * has_prev
            next_row = lay.row(x_ref, i + 1, 0) if i < lay.group - 1 else lay.halo_next_row(xn_ref) * has_next
            x_dn = jnp.where(row == 0, prev_row, pltpu.roll(x, 1, 0))
            x_up = jnp.where(row == L - 1, next_row, pltpu.roll(x, L - 1, 0))
            y = _silu(w[0:1, :] * x_dn + w[1:2, :] * x + w[2:3, :] * x_up)
            lay.store(o_ref, i, y if scale is None else y * scale)

        u = _gelu_tanh(lay.load(u_ref, i))
        v = _gelu_tanh(lay.load(v_ref, i))
        v = v * _rms_scale(v, B_WIDTH) * ng_ref[...]
        mixed = bias_ref[...]
        for gi in range(B_GROUPS):
            vg = jnp.where(grp == gi, v, 0.0).astype(BF16)
            mixed = mixed + jnp.dot(w_ref[gi], vg, preferred_element_type=F32)
        lay.store(s_ref, i, u * mixed)


def _seq_prep_call(st, conv_w, norm_g, w_s, bias, lay):
    bsz = st["qa"].shape[0]
    lead = lambda b, g: (b,)
    tile = lambda c: lay.spec(c, lead)
    hp, hn = lay.halo_specs(A_PAD)
    full = lambda shp: pl.BlockSpec(shp, lambda b, g: (0,) * len(shp))
    qa, ka, ub, vb = (lay.view(st[n]) for n in ("qa", "ka", "ub", "vb"))
    qo, ko, s = pl.pallas_call(
        functools.partial(_seq_prep_kernel, lay),
        grid=(bsz, lay.ngrp),
        in_specs=[tile(A_PAD), hp, hn, tile(A_PAD), hp, hn, tile(B_WIDTH), tile(B_WIDTH),
                  full((2, 3, A_PAD)), full((1, B_WIDTH)), full((B_GROUPS, SEQ_BLOCK, SEQ_BLOCK)),
                  full((SEQ_BLOCK, B_WIDTH))],
        out_specs=[tile(A_PAD), tile(A_PAD), tile(B_WIDTH)],
        out_shape=[jax.ShapeDtypeStruct(qa.shape, F32), jax.ShapeDtypeStruct(ka.shape, F32),
                   jax.ShapeDtypeStruct(ub.shape, F32)],
        compiler_params=_cparams(2),
        name="seq_prep",
    )(qa, qa, qa, ka, ka, ka, ub, vb, conv_w, norm_g, w_s, bias)
    return lay.unview(qo), lay.unview(ko), lay.unview(s)


def _mlstm_kernel(lay, reverse, q_ref, k_ref, v_ref, g_ref, gb_ref, cmask_ref, spread_ref, rowsel_ref,
                  cinit_ref, minit_ref, *rest):
    other_ref = rest[0] if reverse else None
    h_ref, cfin_ref, mfin_ref, c_sc, m_sc = rest[1:] if reverse else rest
    gidx = pl.program_id(1)
    L = SEQ_BLOCK

    @pl.when(gidx == 0)
    def _():
        c_sc[...] = cinit_ref[...]
        m_sc[...] = minit_ref[...]

    glane = lax.broadcasted_iota(jnp.int32, (1, LANES), 1)
    off = GA_OFF(1 if reverse else 0)
    is_i = (glane >= off) & (glane < off + A_HEADS)
    ri = lax.broadcasted_iota(jnp.int32, (L, L), 0)
    ci = lax.broadcasted_iota(jnp.int32, (L, L), 1)
    past = (ci >= ri) if reverse else (ci <= ri)
    last = 0 if reverse else L - 1
    ones = jnp.ones((L, HEAD_PAD), F32)
    twice = lambda x: jnp.concatenate([x, x], axis=1)

    def chunk(ii, carry):
        i = lay.group - 1 - ii if reverse else ii
        rows = range(MLSTM_BATCH)
        q = [lay.load(q_ref, i, (bb,)).astype(BF16) for bb in rows]
        k = [lay.load(k_ref, i, (bb,)).astype(BF16) for bb in rows]
        v = [lay.load(v_ref, i, (bb,)) for bb in rows]
        g = [lay.load(g_ref, i, (bb,)) + gb_ref[0] for bb in rows]
        lf = [_split_hi_lo(_log_sigmoid(g[bb])) for bb in rows]
        cs = [jnp.dot(cmask_ref[...], jnp.concatenate(lf[bb], axis=1), preferred_element_type=F32) for bb in rows]
        bc = [cs[bb][:, 0:LANES] + cs[bb][:, LANES:2 * LANES] for bb in rows]
        x = [jnp.concatenate(_split_hi_lo(jnp.where(is_i, g[bb], bc[bb])), axis=1) for bb in rows]
        spread = [jnp.dot(x[bb], spread_ref[...], preferred_element_type=F32) for bb in rows]
        c_rows = [lax.dot_general(rowsel_ref[...], x[bb], (((1,), (1,)), ((), ())), preferred_element_type=F32)
                  for bb in rows]

        units = [(bb, h) for bb in rows for h in range(A_HEADS)]
        each = lambda f: {u: f(*u) for u in units}
        sl = lambda h: slice(h * HEAD_PAD, (h + 1) * HEAD_PAD)
        b_rep = each(lambda bb, h: spread[bb][:, sl(h)])
        c_rep = each(lambda bb, h: spread[bb][:, sl(A_HEADS + h)])
        c_row = each(lambda bb, h: c_rows[bb][h:h + 1, :])
        m_prev = each(lambda bb, h: m_sc[bb, h:h + 1, :])
        c2 = each(lambda bb, h: c_sc[bb, h])
        v2 = each(lambda bb, h: jnp.concatenate([v[bb][:, sl(h)], ones], axis=1))
        u_rep = each(lambda bb, h: jnp.maximum(
            m_prev[bb, h], jnp.max(jnp.where(past, c_row[bb, h], -jnp.inf), axis=1, keepdims=True)))
        qk = each(lambda bb, h: lax.dot_general(q[bb][:, sl(h)], k[bb][:, sl(h)], (((1,), (1,)), ((), ())),
                                                preferred_element_type=F32))
        s = each(lambda bb, h: (qk[bb, h] * jnp.exp(jnp.where(past, c_row[bb, h] - u_rep[bb, h], -jnp.inf))
                                ).astype(BF16))
        inter = each(lambda bb, h: jnp.exp(m_prev[bb, h] - u_rep[bb, h]))
        carry_in = each(lambda bb, h: jnp.dot(q[bb][:, sl(h)], c2[bb, h].astype(BF16), preferred_element_type=F32))
        num2 = each(lambda bb, h: jnp.dot(s[bb, h], v2[bb, h].astype(BF16), preferred_element_type=F32)
                    + twice(inter[bb, h]) * carry_in[bb, h])
        inv = each(lambda bb, h: 1.0 / jnp.maximum(jnp.abs(num2[bb, h][:, HEAD_PAD:]),
                                                   jnp.exp(-(b_rep[bb, h] + u_rep[bb, h]))))
        for bb in rows:
            out = jnp.concatenate([num2[bb, h][:, :HEAD_PAD] * inv[bb, h] for h in range(A_HEADS)], axis=1)
            if other_ref is not None:
                out = out + lay.load(other_ref, i, (bb,))
            lay.store(h_ref, i, out, (bb,))

        b_last = each(lambda bb, h: b_rep[bb, h][last:last + 1, :])
        w_log = each(lambda bb, h: b_last[bb, h] + c_rep[bb, h])
        m_new = each(lambda bb, h: jnp.maximum(b_last[bb, h] + m_prev[bb, h],
                                               jnp.max(w_log[bb, h], axis=0, keepdims=True)))
        wv2 = each(lambda bb, h: (twice(jnp.exp(w_log[bb, h] - m_new[bb, h])) * v2[bb, h]).astype(BF16))
        for bb, h in units:
            upd = lax.dot_general(k[bb][:, sl(h)], wv2[bb, h], (((0,), (0,)), ((), ())), preferred_element_type=F32)
            decay = jnp.exp(b_last[bb, h] + m_prev[bb, h] - m_new[bb, h])
            c_sc[bb, h] = twice(decay) * c2[bb, h] + upd
            m_sc[bb, h:h + 1, :] = m_new[bb, h]
        return carry

    lax.fori_loop(0, lay.group, chunk, 0)

    @pl.when(gidx == lay.ngrp - 1)
    def _():
        cfin_ref[...] = c_sc[...]
        mfin_ref[...] = m_sc[...]


def _mlstm_gate_matrices(off):
    lane = jnp.arange(LANES) - off
    blocks = []
    for h in range(A_HEADS):
        blocks.append(jnp.broadcast_to((lane == A_HEADS + h).astype(F32)[:, None], (LANES, LANES)))
    for h in range(A_HEADS):
        col = (lane == h).astype(F32) - (lane == A_HEADS + h).astype(F32)
        blocks.append(jnp.broadcast_to(col[:, None], (LANES, LANES)))
    half = jnp.concatenate(blocks, axis=1)
    spread = jnp.concatenate([half, half], axis=0).astype(BF16)
    sel = jnp.stack([(lane == h).astype(F32) - (lane == A_HEADS + h).astype(F32) for h in range(A_HEADS)]
                    + [jnp.zeros((LANES,), F32)] * (2 * SUBLANES - A_HEADS))
    rowsel = jnp.concatenate([sel, sel], axis=1).astype(BF16)
    return spread, rowsel


def _mlstm_call(qconv, kconv, st, gate_b, cinit, minit, lay, reverse, other=None):
    bsz = qconv.shape[0]
    di = 1 if reverse else 0
    nb = MLSTM_BATCH
    tile = lambda c: lay.spec(c, lambda b, g: (b,), reverse, batch=nb)
    full = lambda shp: pl.BlockSpec(shp, lambda b, g: (0,) * len(shp))
    c_spec = pl.BlockSpec((nb, A_HEADS, HEAD_PAD, 2 * HEAD_PAD), lambda b, g: (b, 0, 0, 0))
    m_spec = pl.BlockSpec((nb, SUBLANES, LANES), lambda b, g: (b, 0, 0))
    qa, ka, va, ga = lay.view(qconv), lay.view(kconv), lay.view(st["va"]), lay.view(st["gt"])
    cmask, _ = _scan_masks(reverse)
    spread, rowsel = _mlstm_gate_matrices(GA_OFF(di))
    h, cfin, mfin = pl.pallas_call(
        functools.partial(_mlstm_kernel, lay, reverse),
        grid=(bsz // nb, lay.ngrp),
        in_specs=[tile(A_PAD), tile(A_PAD), tile(A_PAD), tile(LANES),
                  pl.BlockSpec((1, 1, LANES), lambda b, g: (di, 0, 0)),
                  full(cmask.shape), full(spread.shape), full(rowsel.shape),
                  c_spec, m_spec] + ([tile(A_PAD)] if reverse else []),
        out_specs=[tile(A_PAD), c_spec, m_spec],
        out_shape=[jax.ShapeDtypeStruct(qa.shape, F32),
                   jax.ShapeDtypeStruct(cinit.shape, F32),
                   jax.ShapeDtypeStruct(minit.shape, F32)],
        scratch_shapes=[pltpu.VMEM((nb, A_HEADS, HEAD_PAD, 2 * HEAD_PAD), F32),
                        pltpu.VMEM((nb, SUBLANES, LANES), F32)],
        compiler_params=_cparams(2),
        name="mlstm_scan_bwd" if reverse else "mlstm_scan_fwd",
    )(qa, ka, va, ga, gate_b, cmask, spread, rowsel, cinit, minit, *([lay.view(other)] if reverse else []))
    return lay.unview(h), cfin, mfin


def _gla_kernel(lay, reverse, q_ref, k_ref, v_ref, gk_ref, w2_ref, bgk_ref, cmask_ref, lmask_ref, sinit_ref,
                *rest):
    other_ref = rest[0] if reverse else None
    o_ref, sfin_ref, s_sc, sbd_sc = rest[1:] if reverse else rest
    gidx = pl.program_id(1)
    L = SEQ_BLOCK

    @pl.when(gidx == 0)
    def _():
        s_sc[...] = sinit_ref[...]
        sbd_sc[...] = jnp.zeros_like(sbd_sc)
        for bb in range(GLA_BATCH):
            for h in range(C_HEADS):
                sbd_sc[bb, h * CK_PAD:(h + 1) * CK_PAD, h * HEAD_PAD:(h + 1) * HEAD_PAD] = \
                    sinit_ref[bb, h].astype(BF16)

    glane = lax.broadcasted_iota(jnp.int32, (1, LANES), 1)
    gk_lo = GK_OFF(1 if reverse else 0) + 2 * C_GATE_RANK
    row = lax.broadcasted_iota(jnp.int32, (L, 1), 0)
    hlane = glane // CK_PAD
    col = lambda x, h: x[:, (h // 2) * LANES:(h // 2 + 1) * LANES]
    last = 0 if reverse else L - 1
    ones = jnp.ones((L, HEAD_PAD), BF16)

    def rows_from(x, idx_of_row_block, rows_per_block):
        n = L // rows_per_block
        return jnp.concatenate([jnp.broadcast_to(x[idx_of_row_block(i):idx_of_row_block(i) + 1],
                                                 (rows_per_block, x.shape[1])) for i in range(n)], axis=0)

    def chunk(ii, carry):
        i = lay.group - 1 - ii if reverse else ii
        rows = range(GLA_BATCH)
        heads = range(C_HEADS)
        g = [lay.load(gk_ref, i, (bb,)) for bb in rows]
        g_hi = [g[bb].astype(BF16).astype(F32) for bb in rows]
        g_split = [jnp.where(glane < gk_lo, g_hi[bb], g[bb] - g_hi[bb]).astype(BF16) for bb in rows]
        la = [_log_sigmoid(jnp.dot(g_split[bb], w2_ref[0], preferred_element_type=F32) + bgk_ref[0])
              * (1.0 / C_GATE_NORM) for bb in rows]
        la_split = [_split_hi_lo(la[bb]) for bb in rows]
        cs = [jnp.dot(cmask_ref[...], jnp.concatenate(la_split[bb], axis=1), preferred_element_type=F32)
              for bb in rows]
        bs = [cs[bb][:, 0:CK_W] + cs[bb][:, CK_W:2 * CK_W] for bb in rows]
        qs = [lay.load(q_ref, i, (bb,)) * (C_HEAD_DK ** -0.5) for bb in rows]
        kk = [lay.load(k_ref, i, (bb,)) for bb in rows]
        vb = [lay.load(v_ref, i, (bb,)).astype(BF16) for bb in rows]
        k_head = [[jnp.where(hlane == h % 2, col(kk[bb], h), 0.0) for h in heads] for bb in rows]
        att = [[None] * C_HEADS for _ in rows]

        def add_level(bb, q_t, k_mul, level, k_rows=None):
            q_b = q_t.astype(BF16)
            for h in heads:
                k_t = k_head[bb][h] if k_mul is None else k_head[bb][h] * col(k_mul, h)
                if k_rows is not None:
                    k_t = jnp.where(k_rows, k_t, 0.0)
                a = lax.dot_general(col(q_b, h), k_t.astype(BF16), (((1,), (1,)), ((), ())),
                                    preferred_element_type=F32)
                if level is not None:
                    a = a * lmask_ref[level]
                att[bb][h] = a if att[bb][h] is None else att[bb][h] + a

        def level_factors(bb, m):
            bnd = m if reverse else m - 1
            if m >= SUBLANES:
                zero = jnp.zeros((m, CK_W), F32)
                q_parts, k_parts = [], []
                for base in range(0, L, 2 * m):
                    pref = bs[bb][base + bnd:base + bnd + 1]
                    lo, hi = slice(base, base + m), slice(base + m, base + 2 * m)
                    q_half, k_half = (lo, hi) if reverse else (hi, lo)
                    q_piece = qs[bb][q_half] * jnp.exp(bs[bb][q_half] - pref)
                    k_piece = jnp.exp(pref - bs[bb][k_half])
                    q_parts += [q_piece, zero] if reverse else [zero, q_piece]
                    k_parts += [zero, k_piece] if reverse else [k_piece, zero]
                return jnp.concatenate(q_parts, axis=0), jnp.concatenate(k_parts, axis=0)
            pos = row % (2 * m)
            is_q = (pos < m) if reverse else (pos >= m)
            if m == 1:
                return jnp.where(is_q, qs[bb] * jnp.exp(la[bb]), 0.0), None
            if 2 * m >= SUBLANES:
                pref = rows_from(bs[bb], lambda r: r * 2 * m + bnd, 2 * m)
            else:
                p0 = rows_from(bs[bb], lambda r: r * SUBLANES + bnd, SUBLANES)
                p1 = rows_from(bs[bb], lambda r: r * SUBLANES + 2 * m + bnd, SUBLANES)
                pref = jnp.where(row % SUBLANES < 2 * m, p0, p1)
            return (qs[bb] * jnp.exp(jnp.where(is_q, bs[bb] - pref, -jnp.inf)),
                    jnp.exp(jnp.where(is_q, -jnp.inf, pref - bs[bb])))

        level = 0
        m = L // 2
        while m >= 1:
            key_rows = (row % 2 == (1 if reverse else 0)) if m == 1 else None
            for bb in rows:
                q_t, k_mul = level_factors(bb, m)
                add_level(bb, q_t, k_mul, None if m == L // 2 else level, key_rows)
            level += 1
            m //= 2
        for bb in rows:
            add_level(bb, qs[bb], None, level)

        o_inter = [jnp.dot((qs[bb] * jnp.exp(bs[bb])).astype(BF16), sbd_sc[bb], preferred_element_type=F32)
                   for bb in rows]
        for bb in rows:
            o_heads = []
            for h in heads:
                cols = slice(h * HEAD_PAD, (h + 1) * HEAD_PAD)
                o_heads.append(o_inter[bb][:, cols] + jnp.dot(att[bb][h].astype(BF16), vb[bb][:, cols],
                                                              preferred_element_type=F32))
            out = jnp.concatenate(o_heads, axis=1)
            if other_ref is not None:
                out = out + lay.load(other_ref, i, (bb,))
            lay.store(o_ref, i, out, (bb,))

        tdot = lambda a: lax.dot_general(a, ones, (((0,), (0,)), ((), ())), preferred_element_type=F32)
        for bb in rows:
            btot = bs[bb][last:last + 1]
            ke_t = (kk[bb] * jnp.exp(btot - bs[bb])).T.astype(BF16)
            dec_col = jnp.exp(tdot(la_split[bb][0]) + tdot(la_split[bb][1]))
            for h in heads:
                krows = slice(h * CK_PAD, (h + 1) * CK_PAD)
                cols = slice(h * HEAD_PAD, (h + 1) * HEAD_PAD)
                upd = jnp.dot(ke_t[krows], vb[bb][:, cols], preferred_element_type=F32)
                s_new = dec_col[krows] * s_sc[bb, h] + upd
                s_sc[bb, h] = s_new
                sbd_sc[bb, krows, cols] = s_new.astype(BF16)
        return carry

    lax.fori_loop(0, lay.group, chunk, 0)

    @pl.when(gidx == lay.ngrp - 1)
    def _():
        sfin_ref[...] = s_sc[...]


def _scan_masks(reverse):
    t = jnp.arange(SEQ_BLOCK)[:, None]
    u = jnp.arange(SEQ_BLOCK)[None, :]
    cmask = ((u >= t) if reverse else (u <= t)).astype(BF16)
    sizes = []
    m = SEQ_BLOCK
    while m >= 1:
        sizes.append(m)
        m //= 2
    lmask = jnp.stack([(t // sz) == (u // sz) for sz in sizes]).astype(F32)
    return cmask, lmask


def _gla_call(st, w2s, b_gk, sinit, lay, reverse, other=None):
    bsz = st["qc"].shape[0]
    di = 1 if reverse else 0
    nb = GLA_BATCH
    tile = lambda c: lay.spec(c, lambda b, g: (b,), reverse, batch=nb)
    full = lambda shp: pl.BlockSpec(shp, lambda b, g: (0,) * len(shp))
    state_spec = pl.BlockSpec((nb, C_HEADS, CK_PAD, HEAD_PAD), lambda b, g: (b, 0, 0, 0))
    qc, kc, vc, gk = (lay.view(st[n]) for n in ("qc", "kc", "vc", "gt"))
    cmask, lmask = _scan_masks(reverse)
    o, sfin = pl.pallas_call(
        functools.partial(_gla_kernel, lay, reverse),
        grid=(bsz // nb, lay.ngrp),
        in_specs=[tile(CK_W), tile(CK_W), tile(CV_W), tile(LANES),
                  pl.BlockSpec((1, LANES, CK_W), lambda b, g: (di, 0, 0)),
                  pl.BlockSpec((1, 1, CK_W), lambda b, g: (di, 0, 0)),
                  full(cmask.shape), full(lmask.shape), state_spec] + ([tile(CV_W)] if reverse else []),
        out_specs=[tile(CV_W), state_spec],
        out_shape=[jax.ShapeDtypeStruct(vc.shape, F32),
                   jax.ShapeDtypeStruct(sinit.shape, F32)],
        scratch_shapes=[pltpu.VMEM((nb, C_HEADS, CK_PAD, HEAD_PAD), F32),
                        pltpu.VMEM((nb, CK_W, CV_W), BF16)],
        compiler_params=_cparams(2),
        name="gla_scan_bwd" if reverse else "gla_scan_fwd",
    )(qc, kc, vc, gk, w2s, b_gk, cmask, lmask, sinit, *([lay.view(other)] if reverse else []))
    return lay.unview(o), sfin


def _mix_out_kernel(x_ref, mod_ref, ha_ref, hc_ref, oa_ref, gc_ref, s_ref, na_ref, nc_ref, w_ref, o_ref):
    def head_norm(xh, gain):
        return xh * _rms_scale(xh, A_HEAD_DIM) * gain

    acc = None
    for j in range(B_WIDTH // LANES):
        r0 = A_PAD + j * LANES
        part = jnp.dot(s_ref[0, j].astype(BF16), w_ref[r0:r0 + LANES, :], preferred_element_type=F32)
        acc = part if acc is None else acc + part
    for h in range(A_HEADS):
        sl = slice(h * HEAD_PAD, (h + 1) * HEAD_PAD)
        a = head_norm(ha_ref[0, h], na_ref[:, sl])
        c = head_norm(hc_ref[0, h], nc_ref[:, sl])
        ah = (a * jax.nn.sigmoid(oa_ref[0, h].astype(F32))).astype(BF16)
        ch = (c * _silu(gc_ref[0, h].astype(F32))).astype(BF16)
        acc = acc + jnp.dot(ah, w_ref[h * HEAD_PAD:(h + 1) * HEAD_PAD, :], preferred_element_type=F32)
        c0 = A_PAD + B_WIDTH + h * HEAD_PAD
        acc = acc + jnp.dot(ch, w_ref[c0:c0 + HEAD_PAD, :], preferred_element_type=F32)
    o_ref[0] = x_ref[0] + mod_ref[0, 2:3, :] * acc


def _mix_out_call(x, mod, ha, hc, oa, gc, s, na, nc, w, tm):
    bsz, t, d = x.shape
    mod_map = (lambda b, i: (b, 0, 0)) if mod.shape[0] > 1 else (lambda b, i: (0, 0, 0))
    row = lambda c: pl.BlockSpec((1, tm, c), lambda b, i: (b, i, 0))
    chunked = lambda c: pl.BlockSpec((1, c // LANES, tm, LANES), lambda b, i: (b, 0, i, 0))
    full = lambda shp: pl.BlockSpec(shp, lambda b, i: (0,) * len(shp))
    return pl.pallas_call(
        _mix_out_kernel,
        grid=(bsz, t // tm),
        in_specs=[row(d), pl.BlockSpec((1, 6, d), mod_map), chunked(A_PAD), chunked(CV_W),
                  chunked(A_PAD), chunked(CV_W), chunked(B_WIDTH),
                  full((1, A_PAD)), full((1, CV_W)), _resident(w.shape, 2)],
        out_specs=row(d),
        out_shape=jax.ShapeDtypeStruct(x.shape, F32),
        compiler_params=_cparams(2),
        name="mix_out",
    )(x, mod, ha, hc, oa, gc, s, na, nc, w)


def _ffn_kernel(nk, final_norm, x_ref, mod_ref, g_ref, wg_ref, wu_ref, wo_ref, fg_ref, o_ref, h_sc, acc_sc):
    kk = pl.program_id(2)

    @pl.when(kk == 0)
    def _():
        x = x_ref[0]
        y = x * _rms_scale(x, x.shape[-1]) * g_ref[...]
        h_sc[...] = (y * (1.0 + mod_ref[0, 4:5, :]) + mod_ref[0, 3:4, :]).astype(BF16)
        acc_sc[...] = jnp.zeros_like(acc_sc)

    h = h_sc[...]
    gate = jnp.dot(h, wg_ref[...], preferred_element_type=F32)
    up = jnp.dot(h, wu_ref[...], preferred_element_type=F32)
    act = (_silu(gate) * up).astype(BF16)
    acc_sc[...] += jnp.dot(act, wo_ref[...], preferred_element_type=F32)

    @pl.when(kk == nk - 1)
    def _():
        y = x_ref[0] + mod_ref[0, 5:6, :] * acc_sc[...]
        if final_norm:
            y = y * _rms_scale(y, y.shape[-1]) * fg_ref[...]
        o_ref[0] = y


def _ffn_call(x, mod, g, w_in, w_out, final_g, final_norm, tm, nk):
    bsz, t, d = x.shape
    d_ff = w_out.shape[0]
    tk = d_ff // nk
    wmode = {"pipeline_mode": pl.Buffered(1)} if nk == 1 else {}
    mod_map = (lambda b, i, k: (b, 0, 0)) if mod.shape[0] > 1 else (lambda b, i, k: (0, 0, 0))
    return pl.pallas_call(
        functools.partial(_ffn_kernel, nk, final_norm),
        grid=(bsz, t // tm, nk),
        in_specs=[pl.BlockSpec((1, tm, d), lambda b, i, k: (b, i, 0)),
                  pl.BlockSpec((1, 6, d), mod_map),
                  pl.BlockSpec((1, d), lambda b, i, k: (0, 0)),
                  pl.BlockSpec((d, tk), lambda b, i, k: (0, k), **wmode),
                  pl.BlockSpec((d, tk), lambda b, i, k: (0, nk + k), **wmode),
                  pl.BlockSpec((tk, d), lambda b, i, k: (k, 0), **wmode),
                  pl.BlockSpec((1, d), lambda b, i, k: (0, 0))],
        out_specs=pl.BlockSpec((1, tm, d), lambda b, i, k: (b, i, 0)),
        out_shape=jax.ShapeDtypeStruct(x.shape, F32),
        scratch_shapes=[pltpu.VMEM((tm, d), BF16), pltpu.VMEM((tm, d), F32)],
        compiler_params=_cparams(3),
        name="ffn",
    )(x, mod, g, w_in, w_in, w_out, final_g)


def _pad_heads(w, n_heads, dim, pad, axis=-1):
    w = jnp.moveaxis(w, axis, -1)
    lead = w.shape[:-1]
    w = w.reshape(lead + (n_heads, dim))
    w = jnp.pad(w, [(0, 0)] * len(lead) + [(0, 0), (0, pad - dim)])
    return jnp.moveaxis(w.reshape(lead + (n_heads * pad,)), -1, axis)


def _pad_to(w, width):
    return jnp.pad(w, [(0, 0)] * (w.ndim - 1) + [(0, width - w.shape[-1])])


def _prep_w_in(w_in):
    a_w = A_HEADS * A_HEAD_DIM
    ck = C_HEADS * C_HEAD_DK
    cv = C_HEADS * C_HEAD_DV
    sizes = (a_w, a_w, a_w, a_w, N_DIR * 2 * A_HEADS, B_WIDTH, B_WIDTH, ck, ck, cv, cv, N_DIR * C_GATE_RANK)
    pts = [sum(sizes[:i + 1]) for i in range(len(sizes) - 1)]
    qa, ka, va, oa, ga, ub, vb, qc, kc, vc, gc, gkc = jnp.split(w_in, pts, axis=-1)
    pa = lambda w: _pad_heads(w, A_HEADS, A_HEAD_DIM, HEAD_PAD)
    pk = lambda w: _pad_heads(w, C_HEADS, C_HEAD_DK, CK_PAD)
    pv = lambda w: _pad_heads(w, C_HEADS, C_HEAD_DV, HEAD_PAD)
    gates = _pad_to(jnp.concatenate([ga, jnp.tile(gkc[:, :C_GATE_RANK], (1, GK_COPIES)),
                                     jnp.tile(gkc[:, C_GATE_RANK:], (1, GK_COPIES))], axis=1), LANES)
    cols = [pa(qa), pa(ka), pa(va), pa(oa), gates, ub, vb, pk(qc), pk(kc), pv(vc), pv(gc)]
    return jnp.concatenate(cols, axis=-1).astype(BF16)


def _prep_w_gk2(w_gk2):
    w = _pad_heads(w_gk2, C_HEADS, C_HEAD_DK, CK_PAD)
    hi = w.astype(BF16)
    lo = (w - hi.astype(F32)).astype(BF16)
    stacked = jnp.concatenate([hi, lo, hi], axis=1)
    return jnp.stack([jnp.pad(stacked[d], ((GK_OFF(d), LANES - GK_OFF(d) - GK_COPIES * C_GATE_RANK), (0, 0)))
                      for d in range(N_DIR)])


def _prep_w_out(w_out):
    a_w = A_HEADS * A_HEAD_DIM
    wa, wb, wc = w_out[:a_w], w_out[a_w:a_w + B_WIDTH], w_out[a_w + B_WIDTH:]
    wa = _pad_heads(wa, A_HEADS, A_HEAD_DIM, HEAD_PAD, axis=0)
    wc = _pad_heads(wc, C_HEADS, C_HEAD_DV, HEAD_PAD, axis=0)
    return jnp.concatenate([wa, wb, wc], axis=0).astype(BF16)


def _mixer_scans(st, lw, a_state, c_state, lay):
    qconv, kconv, s = _seq_prep_call(st, lw["conv"], lw["sgu_g"], lw["sgu_w"], lw["sgu_bias"], lay)
    ha, hc, a_fin, c_fin = None, None, [], []
    for di, reverse in enumerate((False, True)):
        ha, cfin, mfin = _mlstm_call(qconv, kconv, st, lw["gate_b"], a_state[di][0], a_state[di][1], lay, reverse, ha)
        a_fin.append((cfin, mfin))
        hc, sfin = _gla_call(st, lw["w_gk2"], lw["b_gk"], c_state[di], lay, reverse, hc)
        c_fin.append(sfin)
    return ha, hc, s, a_fin, c_fin


def kernel(x, c, ctx, c_ctx, norm1_g, norm2_g, w_ada, b_ada, w_in, mlstm_conv, mlstm_gate_b,
           mlstm_norm_g, gla_w_gk2, gla_b_gk, gla_norm_g, sgu_norm_g, sgu_w, sgu_b, w_out,
           w_ffn_in, w_ffn_out, final_g):
    bsz, seq, d = x.shape
    ctx_len = ctx.shape[1]
    depth = w_in.shape[0]
    n_cond = 2 * SUBLANES
    cond = jnp.zeros((n_cond, d), F32).at[:bsz].set(c).at[bsz].set(c_ctx)
    mods = _ada_call(cond, w_ada, b_ada).reshape(depth, n_cond, 6, d)

    fg = final_g.reshape(1, d)
    x_lat, x_ctx = x, ctx
    for l in range(depth):
        need_ctx = l < depth - 1
        mod_lat = mods[l, :bsz]
        mod_ctx = mods[l, bsz:bsz + 1]
        gb = mlstm_gate_b[l].reshape(N_DIR, 1, 2 * A_HEADS)
        lw = {
            "conv": jnp.stack([_pad_heads(mlstm_conv[l][:, :A_HEADS * A_HEAD_DIM], A_HEADS, A_HEAD_DIM, HEAD_PAD),
                               _pad_heads(mlstm_conv[l][:, A_HEADS * A_HEAD_DIM:], A_HEADS, A_HEAD_DIM, HEAD_PAD)]),
            "gate_b": jnp.stack([jnp.pad(gb[d], ((0, 0), (GA_OFF(d), LANES - GA_OFF(d) - 2 * A_HEADS)))
                                 for d in range(N_DIR)]),
            "w_gk2": _prep_w_gk2(gla_w_gk2[l]),
            "b_gk": _pad_heads(gla_b_gk[l], C_HEADS, C_HEAD_DK, CK_PAD).reshape(N_DIR, 1, CK_W),
            "sgu_g": sgu_norm_g[l].reshape(1, B_WIDTH),
            "sgu_w": sgu_w[l].astype(BF16),
            "sgu_bias": jnp.repeat(sgu_b[l].T, B_GROUP_DIM, axis=1),
        }
        w_in_l = _prep_w_in(w_in[l])
        w_out_l = _prep_w_out(w_out[l])
        na = _pad_heads(mlstm_norm_g[l], A_HEADS, A_HEAD_DIM, HEAD_PAD).reshape(1, A_PAD)
        nc = _pad_heads(jnp.tile(gla_norm_g[l], C_HEADS), C_HEADS, C_HEAD_DV, HEAD_PAD).reshape(1, CV_W)
        g1 = norm1_g[l].reshape(1, d)
        g2 = norm2_g[l].reshape(1, d)
        w_ffn_in_l = w_ffn_in[l].astype(BF16)
        w_ffn_out_l = w_ffn_out[l].astype(BF16)

        st_ctx = _proj_in_call(x_ctx, mod_ctx, g1, w_in_l, tm=_row_tile(ctx_len, PROJ_ROWS))
        st_lat = _proj_in_call(x_lat, mod_lat, g1, w_in_l, tm=_row_tile(seq, PROJ_ROWS))

        a0 = (jnp.zeros((bsz, A_HEADS, HEAD_PAD, 2 * HEAD_PAD), F32), jnp.zeros((bsz, SUBLANES, LANES), F32))
        c0 = jnp.zeros((bsz, C_HEADS, CK_PAD, HEAD_PAD), F32)
        lay_ctx = _SeqLayout(ctx_len, False)
        lay_lat = _SeqLayout(seq, l % 2 == 1)
        ha_c, hc_c, s_c, a_state, c_state = _mixer_scans(st_ctx, lw, (a0, a0), (c0, c0), lay_ctx)
        ha_l, hc_l, s_l, _, _ = _mixer_scans(st_lat, lw, a_state, c_state, lay_lat)

        x_lat = _mix_out_call(x_lat, mod_lat, ha_l, hc_l, st_lat["oa"], st_lat["gc"], s_l,
                              na, nc, w_out_l, tm=_row_tile(seq, MIX_ROWS))
        x_lat = _ffn_call(x_lat, mod_lat, g2, w_ffn_in_l, w_ffn_out_l, fg, not need_ctx,
                          tm=_row_tile(seq, FFN_ROWS), nk=FFN_SPLIT)
        if need_ctx:
            x_ctx = _mix_out_call(x_ctx, mod_ctx, ha_c, hc_c, st_ctx["oa"], st_ctx["gc"], s_c,
                                  na, nc, w_out_l, tm=_row_tile(ctx_len, MIX_ROWS))
            x_ctx = _ffn_call(x_ctx, mod_ctx, g2, w_ffn_in_l, w_ffn_out_l, fg, False,
                              tm=_row_tile(ctx_len, FFN_ROWS), nk=FFN_SPLIT)
    return x_lat
```

```python
import functools
import math

import jax
import jax.numpy as jnp
from jax import lax
from jax.experimental import pallas as pl
from jax.experimental.pallas import tpu as pltpu

F32 = jnp.float32
BF16 = jnp.bfloat16
HIGHEST = lax.Precision.HIGHEST

LANES = 128
SUBLANES = 8
VMEM_LIMIT_BYTES = 56 * 1024 * 1024

GRID_W = 64
RMS_EPS = 1e-6
A_HEADS = 4
A_HEAD_DIM = 96
B_GROUPS = 4
B_GROUP_DIM = 64
B_WIDTH = B_GROUPS * B_GROUP_DIM
C_HEADS = 4
C_HEAD_DK = 48
C_HEAD_DV = 96
C_GATE_RANK = 16
C_GATE_NORM = 16.0
N_DIR = 2

HEAD_PAD = LANES
A_PAD = A_HEADS * HEAD_PAD
CK_PAD = 64
CK_W = C_HEADS * CK_PAD
CV_W = C_HEADS * HEAD_PAD

SEQ_BLOCK = 128
SEQ_GROUP = SUBLANES
MLSTM_BATCH = 2
GLA_BATCH = 2
GK_COPIES = 3
GA_OFF = lambda d: 2 * A_HEADS * d
GK_OFF = lambda d: N_DIR * 2 * A_HEADS + GK_COPIES * C_GATE_RANK * d

PROJ_ROWS = 512
TAIL_ROWS = 512
ADA_COLS = 1536

_IN_LAYOUT = (("qa", A_PAD), ("ka", A_PAD), ("va", A_PAD), ("oa", A_PAD), ("gt", LANES),
              ("ub", B_WIDTH), ("vb", B_WIDTH), ("qc", CK_W), ("kc", CK_W), ("vc", CV_W), ("gc", CV_W))
N_IN_PAD = sum(w for _, w in _IN_LAYOUT)
_BF16_STREAMS = ("oa", "gc")


def _cparams(n_axes):
    return pltpu.CompilerParams(dimension_semantics=("arbitrary",) * n_axes,
                                vmem_limit_bytes=VMEM_LIMIT_BYTES)


def _resident(shape, n_grid):
    zeros = (0,) * len(shape)
    maps = {2: lambda a, b: zeros, 3: lambda a, b, c: zeros}
    return pl.BlockSpec(shape, maps[n_grid], pipeline_mode=pl.Buffered(1))


def _row_tile(t_len, want):
    return min(want, t_len)


def _log_sigmoid(x):
    return jnp.minimum(x, 0.0) - jnp.log(1.0 + jnp.exp(-jnp.abs(x)))


def _silu(x):
    return x * jax.nn.sigmoid(x)


def _gelu_tanh(x):
    c = math.sqrt(2.0 / math.pi)
    return 0.5 * x * (1.0 + jnp.tanh(c * (x + 0.044715 * (x * x * x))))


def _rms_scale(x, n):
    return lax.rsqrt(jnp.sum(x * x, axis=-1, keepdims=True) * (1.0 / n) + RMS_EPS)


def _split_hi_lo(x):
    hi = x.astype(BF16)
    lo = (x - hi.astype(F32)).astype(BF16)
    return hi, lo


def _ada_kernel(s_ref, w_ref, b_ref, o_ref):
    s = _silu(s_ref[...])
    o_ref[0] = jnp.dot(s, w_ref[0], precision=HIGHEST, preferred_element_type=F32) + b_ref[0]


def _ada_call(cond, w_ada, b_ada):
    depth, d, n6 = w_ada.shape
    nb = cond.shape[0]
    tn = ADA_COLS
    return pl.pallas_call(
        _ada_kernel,
        grid=(depth, n6 // tn),
        in_specs=[pl.BlockSpec((nb, d), lambda l, j: (0, 0)),
                  pl.BlockSpec((1, d, tn), lambda l, j: (l, 0, j)),
                  pl.BlockSpec((1, 1, tn), lambda l, j: (l, 0, j))],
        out_specs=pl.BlockSpec((1, nb, tn), lambda l, j: (l, 0, j)),
        out_shape=jax.ShapeDtypeStruct((depth, nb, n6), F32),
        compiler_params=_cparams(2),
        name="ada_mod",
    )(cond, w_ada, b_ada.reshape(depth, 1, n6))


def _proj_in_kernel(x_ref, mod_ref, g_ref, w_ref, *out_refs):
    x = x_ref[0]
    y = x * _rms_scale(x, x.shape[-1]) * g_ref[...]
    h = (y * (1.0 + mod_ref[0, 1:2, :]) + mod_ref[0, 0:1, :]).astype(BF16)
    off = 0
    for (_, width), o_ref in zip(_IN_LAYOUT, out_refs):
        res = jnp.dot(h, w_ref[:, off:off + width], preferred_element_type=F32)
        for j in range(width // LANES):
            o_ref[0, j] = res[:, j * LANES:(j + 1) * LANES].astype(o_ref.dtype)
        off += width


def _proj_in_call(x, mod, g, w, tm):
    bsz, t, d = x.shape
    mod_b = mod.shape[0]
    mod_map = (lambda b, i: (b, 0, 0)) if mod_b > 1 else (lambda b, i: (0, 0, 0))
    out_shapes, out_specs = [], []
    for name, wd in _IN_LAYOUT:
        nch = wd // LANES
        out_shapes.append(jax.ShapeDtypeStruct((bsz, nch, t, LANES), BF16 if name in _BF16_STREAMS else F32))
        out_specs.append(pl.BlockSpec((1, nch, tm, LANES), lambda b, i: (b, 0, i, 0)))
    outs = pl.pallas_call(
        _proj_in_kernel,
        grid=(bsz, t // tm),
        in_specs=[pl.BlockSpec((1, tm, d), lambda b, i: (b, i, 0)),
                  pl.BlockSpec((1, 6, d), mod_map),
                  pl.BlockSpec((1, d), lambda b, i: (0, 0)),
                  _resident((d, N_IN_PAD), 2)],
        out_specs=out_specs,
        out_shape=out_shapes,
        compiler_params=_cparams(2),
        name="proj_in",
    )(x, mod, g, w)
    return dict(zip([n for n, _ in _IN_LAYOUT], outs))


class _SeqLayout:
    def __init__(self, t_len, column_major):
        self.cm = column_major
        self.t_len = t_len
        self.nblk = t_len // SEQ_BLOCK
        self.group = min(SEQ_GROUP, self.nblk)
        self.ngrp = self.nblk // self.group
        if column_major:
            assert t_len == SEQ_BLOCK * GRID_W and self.group == SEQ_GROUP

    def view(self, a):
        if not self.cm:
            return a
        return a.reshape(a.shape[:-2] + (SEQ_BLOCK, GRID_W, LANES))

    def unview(self, a):
        if not self.cm:
            return a
        return a.reshape(a.shape[:-3] + (self.t_len, LANES))

    def spec(self, c, lead_map, reverse=False, batch=1):
        def grp(g):
            return self.ngrp - 1 - g if reverse else g

        n_lead = len(lead_map(0, 0))
        lead_blk = (1,) * (n_lead - 1) + (batch,)
        nch = c // LANES
        if self.cm:
            return pl.BlockSpec(lead_blk + (nch, SEQ_BLOCK, self.group, LANES),
                                lambda b, g: lead_map(b, g) + (0, 0, grp(g), 0))
        return pl.BlockSpec(lead_blk + (nch, self.group * SEQ_BLOCK, LANES),
                            lambda b, g: lead_map(b, g) + (0, grp(g), 0))

    def halo_specs(self, c):
        nch = c // LANES
        if self.cm:
            shp = (1, nch, SUBLANES, self.group, LANES)
            prev = pl.BlockSpec(shp, lambda b, g: (b, 0, SEQ_BLOCK // SUBLANES - 1, jnp.maximum(g - 1, 0), 0))
            nxt = pl.BlockSpec(shp, lambda b, g: (b, 0, 0, jnp.minimum(g + 1, self.ngrp - 1), 0))
        else:
            per = self.group * SEQ_BLOCK // SUBLANES
            last = self.t_len // SUBLANES - 1
            shp = (1, nch, SUBLANES, LANES)
            prev = pl.BlockSpec(shp, lambda b, g: (b, 0, jnp.maximum(g * per - 1, 0), 0))
            nxt = pl.BlockSpec(shp, lambda b, g: (b, 0, jnp.minimum((g + 1) * per, last), 0))
        return prev, nxt

    @staticmethod
    def _cat(pieces):
        return pieces[0] if len(pieces) == 1 else jnp.concatenate(pieces, axis=1)

    def halo_prev_row(self, ref):
        r = SUBLANES - 1
        if self.cm:
            return self._cat([ref[0, j, r:r + 1, self.group - 1, :] for j in range(ref.shape[1])])
        return self._cat([ref[0, j, r:r + 1, :] for j in range(ref.shape[1])])

    def halo_next_row(self, ref):
        if self.cm:
            return self._cat([ref[0, j, 0:1, 0, :] for j in range(ref.shape[1])])
        return self._cat([ref[0, j, 0:1, :] for j in range(ref.shape[1])])

    def _piece(self, ref, lead, j, i):
        if self.cm:
            flat = ref.at[lead + (j,)].reshape(self.group * SEQ_BLOCK, LANES)
            return flat, (pl.ds(i, SEQ_BLOCK, stride=self.group), slice(None))
        start = i * SEQ_BLOCK
        if not isinstance(i, int):
            start = pl.multiple_of(start, SEQ_BLOCK)
        return ref, lead + (j, pl.ds(start, SEQ_BLOCK), slice(None))

    def load(self, ref, i, lead=(0,)):
        pieces = []
        for j in range(ref.shape[len(lead)]):
            r, idx = self._piece(ref, lead, j, i)
            pieces.append(r[idx])
        return self._cat(pieces)

    def store(self, ref, i, val, lead=(0,)):
        for j in range(ref.shape[len(lead)]):
            r, idx = self._piece(ref, lead, j, i)
            r[idx] = val[:, j * LANES:(j + 1) * LANES]

    def row(self, ref, i, r):
        if self.cm:
            return self._cat([ref[0, j, r:r + 1, i, :] for j in range(ref.shape[1])])
        t = i * SEQ_BLOCK + r
        return self._cat([ref[0, j, t:t + 1, :] for j in range(ref.shape[1])])


def _seq_prep_kernel(lay, q_ref, qp_ref, qn_ref, k_ref, kp_ref, kn_ref, u_ref, v_ref, cw_ref, ng_ref, w_ref,
                     bias_ref, qo_ref, ko_ref, s_ref):
    g = pl.program_id(1)
    L = SEQ_BLOCK
    row = lax.broadcasted_iota(jnp.int32, (L, 1), 0)
    has_prev = (g > 0).astype(F32)
    has_next = (g < lay.ngrp - 1).astype(F32)
    grp = lax.broadcasted_iota(jnp.int32, (1, B_WIDTH), 1) // B_GROUP_DIM

    for i in range(lay.group):
        for x_ref, xp_ref, xn_ref, o_ref, w, scale in ((q_ref, qp_ref, qn_ref, qo_ref, cw_ref[0], A_HEAD_DIM ** -0.5),
                                                       (k_ref, kp_ref, kn_ref, ko_ref, cw_ref[1], None)):
            x = lay.load(x_ref, i)
            prev_row = lay.row(x_ref, i - 1, L - 1) if i > 0 else lay.halo_prev_row(xp_ref) * has_prev
            next_row = lay.row(x_ref, i + 1, 0) if i < lay.group - 1 else lay.halo_next_row(xn_ref) * has_next
            x_dn = jnp.where(row == 0, prev_row, pltpu.roll(x, 1, 0))
            x_up = jnp.where(row == L - 1, next_row, pltpu.roll(x, L - 1, 0))
            y = _silu(w[0:1, :] * x_dn + w[1:2, :] * x + w[2:3, :] * x_up)
            lay.store(o_ref, i, y if scale is None else y * scale)

        u = _gelu_tanh(lay.load(u_ref, i))
        v = _gelu_tanh(lay.load(v_ref, i))
        v = v * _rms_scale(v, B_WIDTH) * ng_ref[...]
        mixed = bias_ref[...]
        for gi in range(B_GROUPS):
            vg = jnp.where(grp == gi, v, 0.0).astype(BF16)
            mixed = mixed + jnp.dot(w_ref[gi], vg, preferred_element_type=F32)
        lay.store(s_ref, i, u * mixed)


def _seq_prep_call(st, conv_w, norm_g, w_s, bias, lay):
    bsz = st["qa"].shape[0]
    lead = lambda b, g: (b,)
    tile = lambda c: lay.spec(c, lead)
    hp, hn = lay.halo_specs(A_PAD)
    full = lambda shp: pl.BlockSpec(shp, lambda b, g: (0,) * len(shp))
    qa, ka, ub, vb = (lay.view(st[n]) for n in ("qa", "ka", "ub", "vb"))
    qo, ko, s = pl.pallas_call(
        functools.partial(_seq_prep_kernel, lay),
        grid=(bsz, lay.ngrp),
        in_specs=[tile(A_PAD), hp, hn, tile(A_PAD), hp, hn, tile(B_WIDTH), tile(B_WIDTH),
                  full((2, 3, A_PAD)), full((1, B_WIDTH)), full((B_GROUPS, SEQ_BLOCK, SEQ_BLOCK)),
                  full((SEQ_BLOCK, B_WIDTH))],
        out_specs=[tile(A_PAD), tile(A_PAD), tile(B_WIDTH)],
        out_shape=[jax.ShapeDtypeStruct(qa.shape, F32), jax.ShapeDtypeStruct(ka.shape, F32),
                   jax.ShapeDtypeStruct(ub.shape, F32)],
        compiler_params=_cparams(2),
        name="seq_prep",
    )(qa, qa, qa, ka, ka, ka, ub, vb, conv_w, norm_g, w_s, bias)
    return lay.unview(qo), lay.unview(ko), lay.unview(s)


def _mlstm_kernel(lay, reverse, q_ref, k_ref, v_ref, g_ref, gb_ref, cmask_ref, spread_ref, rowsel_ref,
                  cinit_ref, minit_ref, *rest):
    other_ref = rest[0] if reverse else None
    h_ref, cfin_ref, mfin_ref, c_sc, m_sc = rest[1:] if reverse else rest
    gidx = pl.program_id(1)
    L = SEQ_BLOCK

    @pl.when(gidx == 0)
    def _():
        c_sc[...] = cinit_ref[...]
        m_sc[...] = minit_ref[...]

    glane = lax.broadcasted_iota(jnp.int32, (1, LANES), 1)
    off = GA_OFF(1 if reverse else 0)
    is_i = (glane >= off) & (glane < off + A_HEADS)
    ri = lax.broadcasted_iota(jnp.int32, (L, L), 0)
    ci = lax.broadcasted_iota(jnp.int32, (L, L), 1)
    past = (ci >= ri) if reverse else (ci <= ri)
    last = 0 if reverse else L - 1
    ones = jnp.ones((L, HEAD_PAD), F32)
    twice = lambda x: jnp.concatenate([x, x], axis=1)

    def chunk(ii, carry):
        i = lay.group - 1 - ii if reverse else ii
        rows = range(MLSTM_BATCH)
        q = [lay.load(q_ref, i, (bb,)).astype(BF16) for bb in rows]
        k = [lay.load(k_ref, i, (bb,)).astype(BF16) for bb in rows]
        v = [lay.load(v_ref, i, (bb,)) for bb in rows]
        g = [lay.load(g_ref, i, (bb,)) + gb_ref[0] for bb in rows]
        lf = [_split_hi_lo(_log_sigmoid(g[bb])) for bb in rows]
        cs = [jnp.dot(cmask_ref[...], jnp.concatenate(lf[bb], axis=1), preferred_element_type=F32) for bb in rows]
        bc = [cs[bb][:, 0:LANES] + cs[bb][:, LANES:2 * LANES] for bb in rows]
        x = [jnp.concatenate(_split_hi_lo(jnp.where(is_i, g[bb], bc[bb])), axis=1) for bb in rows]
        spread = [jnp.dot(x[bb], spread_ref[...], preferred_element_type=F32) for bb in rows]
        c_rows = [lax.dot_general(rowsel_ref[...], x[bb], (((1,), (1,)), ((), ())), preferred_element_type=F32)
                  for bb in rows]

        units = [(bb, h) for bb in rows for h in range(A_HEADS)]
        each = lambda f: {u: f(*u) for u in units}
        sl = lambda h: slice(h * HEAD_PAD, (h + 1) * HEAD_PAD)
        b_rep = each(lambda bb, h: spread[bb][:, sl(h)])
        c_rep = each(lambda bb, h: spread[bb][:, sl(A_HEADS + h)])
        c_row = each(lambda bb, h: c_rows[bb][h:h + 1, :])
        m_prev = each(lambda bb, h: m_sc[bb, h:h + 1, :])
        c2 = each(lambda bb, h: c_sc[bb, h])
        v2 = each(lambda bb, h: jnp.concatenate([v[bb][:, sl(h)], ones], axis=1))
        u_rep = each(lambda bb, h: jnp.maximum(
            m_prev[bb, h], jnp.max(jnp.where(past, c_row[bb, h], -jnp.inf), axis=1, keepdims=True)))
        qk = each(lambda bb, h: lax.dot_general(q[bb][:, sl(h)], k[bb][:, sl(h)], (((1,), (1,)), ((), ())),
                                                preferred_element_type=F32))
        s = each(lambda bb, h: (qk[bb, h] * jnp.exp(jnp.where(past, c_row[bb, h] - u_rep[bb, h], -jnp.inf))
                                ).astype(BF16))
        inter = each(lambda bb, h: jnp.exp(m_prev[bb, h] - u_rep[bb, h]))
        carry_in = each(lambda bb, h: jnp.dot(q[bb][:, sl(h)], c2[bb, h].astype(BF16), preferred_element_type=F32))
        num2 = each(lambda bb, h: jnp.dot(s[bb, h], v2[bb, h].astype(BF16), preferred_element_type=F32)
                    + twice(inter[bb, h]) * carry_in[bb, h])
        inv = each(lambda bb, h: 1.0 / jnp.maximum(jnp.abs(num2[bb, h][:, HEAD_PAD:]),
                                                   jnp.exp(-(b_rep[bb, h] + u_rep[bb, h]))))
        for bb in rows:
            out = jnp.concatenate([num2[bb, h][:, :HEAD_PAD] * inv[bb, h] for h in range(A_HEADS)], axis=1)
            if other_ref is not None:
                out = out + lay.load(other_ref, i, (bb,))
            lay.store(h_ref, i, out, (bb,))

        b_last = each(lambda bb, h: b_rep[bb, h][last:last + 1, :])
        w_log = each(lambda bb, h: b_last[bb, h] + c_rep[bb, h])
        m_new = each(lambda bb, h: jnp.maximum(b_last[bb, h] + m_prev[bb, h],
                                               jnp.max(w_log[bb, h], axis=0, keepdims=True)))
        wv2 = each(lambda bb, h: (twice(jnp.exp(w_log[bb, h] - m_new[bb, h])) * v2[bb, h]).astype(BF16))
        for bb, h in units:
            upd = lax.dot_general(k[bb][:, sl(h)], wv2[bb, h], (((0,), (0,)), ((), ())), preferred_element_type=F32)
            decay = jnp.exp(b_last[bb, h] + m_prev[bb, h] - m_new[bb, h])
            c_sc[bb, h] = twice(decay) * c2[bb, h] + upd
            m_sc[bb, h:h + 1, :] = m_new[bb, h]
        return carry

    lax.fori_loop(0, lay.group, chunk, 0)

    @pl.when(gidx == lay.ngrp - 1)
    def _():
        cfin_ref[...] = c_sc[...]
        mfin_ref[...] = m_sc[...]


def _mlstm_gate_matrices(off):
    lane = jnp.arange(LANES) - off
    blocks = []
    for h in range(A_HEADS):
        blocks.append(jnp.broadcast_to((lane == A_HEADS + h).astype(F32)[:, None], (LANES, LANES)))
    for h in range(A_HEADS):
        col = (lane == h).astype(F32) - (lane == A_HEADS + h).astype(F32)
        blocks.append(jnp.broadcast_to(col[:, None], (LANES, LANES)))
    half = jnp.concatenate(blocks, axis=1)
    spread = jnp.concatenate([half, half], axis=0).astype(BF16)
    sel = jnp.stack([(lane == h).astype(F32) - (lane == A_HEADS + h).astype(F32) for h in range(A_HEADS)]
                    + [jnp.zeros((LANES,), F32)] * (2 * SUBLANES - A_HEADS))
    rowsel = jnp.concatenate([sel, sel], axis=1).astype(BF16)
    return spread, rowsel


def _mlstm_call(qconv, kconv, st, gate_b, cinit, minit, lay, reverse, other=None):
    bsz = qconv.shape[0]
    di = 1 if reverse else 0
    nb = MLSTM_BATCH
    tile = lambda c: lay.spec(c, lambda b, g: (b,), reverse, batch=nb)
    full = lambda shp: pl.BlockSpec(shp, lambda b, g: (0,) * len(shp))
    c_spec = pl.BlockSpec((nb, A_HEADS, HEAD_PAD, 2 * HEAD_PAD), lambda b, g: (b, 0, 0, 0))
    m_spec = pl.BlockSpec((nb, SUBLANES, LANES), lambda b, g: (b, 0, 0))
    qa, ka, va, ga = lay.view(qconv), lay.view(kconv), lay.view(st["va"]), lay.view(st["gt"])
    cmask, _ = _scan_masks(reverse)
    spread, rowsel = _mlstm_gate_matrices(GA_OFF(di))
    h, cfin, mfin = pl.pallas_call(
        functools.partial(_mlstm_kernel, lay, reverse),
        grid=(bsz // nb, lay.ngrp),
        in_specs=[tile(A_PAD), tile(A_PAD), tile(A_PAD), tile(LANES),
                  pl.BlockSpec((1, 1, LANES), lambda b, g: (di, 0, 0)),
                  full(cmask.shape), full(spread.shape), full(rowsel.shape),
                  c_spec, m_spec] + ([tile(A_PAD)] if reverse else []),
        out_specs=[tile(A_PAD), c_spec, m_spec],
        out_shape=[jax.ShapeDtypeStruct(qa.shape, F32),
                   jax.ShapeDtypeStruct(cinit.shape, F32),
                   jax.ShapeDtypeStruct(minit.shape, F32)],
        scratch_shapes=[pltpu.VMEM((nb, A_HEADS, HEAD_PAD, 2 * HEAD_PAD), F32),
                        pltpu.VMEM((nb, SUBLANES, LANES), F32)],
        compiler_params=_cparams(2),
        name="mlstm_scan_bwd" if reverse else "mlstm_scan_fwd",
    )(qa, ka, va, ga, gate_b, cmask, spread, rowsel, cinit, minit, *([lay.view(other)] if reverse else []))
    return lay.unview(h), cfin, mfin


def _gla_kernel(lay, reverse, q_ref, k_ref, v_ref, gk_ref, w2_ref, bgk_ref, cmask_ref, lmask_ref, sinit_ref,
                *rest):
    other_ref = rest[0] if reverse else None
    o_ref, sfin_ref, s_sc, sbd_sc = rest[1:] if reverse else rest
    gidx = pl.program_id(1)
    L = SEQ_BLOCK

    @pl.when(gidx == 0)
    def _():
        s_sc[...] = sinit_ref[...]
        sbd_sc[...] = jnp.zeros_like(sbd_sc)
        for bb in range(GLA_BATCH):
            for h in range(C_HEADS):
                sbd_sc[bb, h * CK_PAD:(h + 1) * CK_PAD, h * HEAD_PAD:(h + 1) * HEAD_PAD] = \
                    sinit_ref[bb, h].astype(BF16)

    glane = lax.broadcasted_iota(jnp.int32, (1, LANES), 1)
    gk_lo = GK_OFF(1 if reverse else 0) + 2 * C_GATE_RANK
    row = lax.broadcasted_iota(jnp.int32, (L, 1), 0)
    hlane = glane // CK_PAD
    col = lambda x, h: x[:, (h // 2) * LANES:(h // 2 + 1) * LANES]
    last = 0 if reverse else L - 1
    ones = jnp.ones((L, HEAD_PAD), BF16)

    def rows_from(x, idx_of_row_block, rows_per_block):
        n = L // rows_per_block
        return jnp.concatenate([jnp.broadcast_to(x[idx_of_row_block(i):idx_of_row_block(i) + 1],
                                                 (rows_per_block, x.shape[1])) for i in range(n)], axis=0)

    def chunk(ii, carry):
        i = lay.group - 1 - ii if reverse else ii
        rows = range(GLA_BATCH)
        heads = range(C_HEADS)
        g = [lay.load(gk_ref, i, (bb,)) for bb in rows]
        g_hi = [g[bb].astype(BF16).astype(F32) for bb in rows]
        g_split = [jnp.where(glane < gk_lo, g_hi[bb], g[bb] - g_hi[bb]).astype(BF16) for bb in rows]
        la = [_log_sigmoid(jnp.dot(g_split[bb], w2_ref[0], preferred_element_type=F32) + bgk_ref[0])
              * (1.0 / C_GATE_NORM) for bb in rows]
        la_split = [_split_hi_lo(la[bb]) for bb in rows]
        cs = [jnp.dot(cmask_ref[...], jnp.concatenate(la_split[bb], axis=1), preferred_element_type=F32)
              for bb in rows]
        bs = [cs[bb][:, 0:CK_W] + cs[bb][:, CK_W:2 * CK_W] for bb in rows]
        qs = [lay.load(q_ref, i, (bb,)) * (C_HEAD_DK ** -0.5) for bb in rows]
        kk = [lay.load(k_ref, i, (bb,)) for bb in rows]
        vb = [lay.load(v_ref, i, (bb,)).astype(BF16) for bb in rows]
        k_head = [[jnp.where(hlane == h % 2, col(kk[bb], h), 0.0) for h in heads] for bb in rows]
        att = [[None] * C_HEADS for _ in rows]

        def add_level(bb, q_t, k_mul, level, k_rows=None):
            q_b = q_t.astype(BF16)
            for h in heads:
                k_t = k_head[bb][h] if k_mul is None else k_head[bb][h] * col(k_mul, h)
                if k_rows is not None:
                    k_t = jnp.where(k_rows, k_t, 0.0)
                a = lax.dot_general(col(q_b, h), k_t.astype(BF16), (((1,), (1,)), ((), ())),
                                    preferred_element_type=F32)
                if level is not None:
                    a = a * lmask_ref[level]
                att[bb][h] = a if att[bb][h] is None else att[bb][h] + a

        def level_factors(bb, m):
            bnd = m if reverse else m - 1
            if m >= SUBLANES:
                zero = jnp.zeros((m, CK_W), F32)
                q_parts, k_parts = [], []
                for base in range(0, L, 2 * m):
                    pref = bs[bb][base + bnd:base + bnd + 1]
                    lo, hi = slice(base, base + m), slice(base + m, base + 2 * m)
                    q_half, k_half = (lo, hi) if reverse else (hi, lo)
                    q_piece = qs[bb][q_half] * jnp.exp(bs[bb][q_half] - pref)
                    k_piece = jnp.exp(pref - bs[bb][k_half])
                    q_parts += [q_piece, zero] if reverse else [zero, q_piece]
                    k_parts += [zero, k_piece] if reverse else [k_piece, zero]
                return jnp.concatenate(q_parts, axis=0), jnp.concatenate(k_parts, axis=0)
            pos = row % (2 * m)
            is_q = (pos < m) if reverse else (pos >= m)
            if m == 1:
                return jnp.where(is_q, qs[bb] * jnp.exp(la[bb]), 0.0), None
            if 2 * m >= SUBLANES:
                pref = rows_from(bs[bb], lambda r: r * 2 * m + bnd, 2 * m)
            else:
                p0 = rows_from(bs[bb], lambda r: r * SUBLANES + bnd, SUBLANES)
                p1 = rows_from(bs[bb], lambda r: r * SUBLANES + 2 * m + bnd, SUBLANES)
                pref = jnp.where(row % SUBLANES < 2 * m, p0, p1)
            return (qs[bb] * jnp.exp(jnp.where(is_q, bs[bb] - pref, -jnp.inf)),
                    jnp.exp(jnp.where(is_q, -jnp.inf, pref - bs[bb])))

        level = 0
        m = L // 2
        while m >= 1:
            key_rows = (row % 2 == (1 if reverse else 0)) if m == 1 else None
            for bb in rows:
                q_t, k_mul = level_factors(bb, m)
                add_level(bb, q_t, k_mul, None if m == L // 2 else level, key_rows)
            level += 1
            m //= 2
        for bb in rows:
            add_level(bb, qs[bb], None, level)

        o_inter = [jnp.dot((qs[bb] * jnp.exp(bs[bb])).astype(BF16), sbd_sc[bb], preferred_element_type=F32)
                   for bb in rows]
        for bb in rows:
            o_heads = []
            for h in heads:
                cols = slice(h * HEAD_PAD, (h + 1) * HEAD_PAD)
                o_heads.append(o_inter[bb][:, cols] + jnp.dot(att[bb][h].astype(BF16), vb[bb][:, cols],
                                                              preferred_element_type=F32))
            out = jnp.concatenate(o_heads, axis=1)
            if other_ref is not None:
                out = out + lay.load(other_ref, i, (bb,))
            lay.store(o_ref, i, out, (bb,))

        tdot = lambda a: lax.dot_general(a, ones, (((0,), (0,)), ((), ())), preferred_element_type=F32)
        for bb in rows:
            btot = bs[bb][last:last + 1]
            ke_t = (kk[bb] * jnp.exp(btot - bs[bb])).T.astype(BF16)
            dec_col = jnp.exp(tdot(la_split[bb][0]) + tdot(la_split[bb][1]))
            for h in heads:
                krows = slice(h * CK_PAD, (h + 1) * CK_PAD)
                cols = slice(h * HEAD_PAD, (h + 1) * HEAD_PAD)
                upd = jnp.dot(ke_t[krows], vb[bb][:, cols], preferred_element_type=F32)
                s_new = dec_col[krows] * s_sc[bb, h] + upd
                s_sc[bb, h] = s_new
                sbd_sc[bb, krows, cols] = s_new.astype(BF16)
        return carry

    lax.fori_loop(0, lay.group, chunk, 0)

    @pl.when(gidx == lay.ngrp - 1)
    def _():
        sfin_ref[...] = s_sc[...]


def _scan_masks(reverse):
    t = jnp.arange(SEQ_BLOCK)[:, None]
    u = jnp.arange(SEQ_BLOCK)[None, :]
    cmask = ((u >= t) if reverse else (u <= t)).astype(BF16)
    sizes = []
    m = SEQ_BLOCK
    while m >= 1:
        sizes.append(m)
        m //= 2
    lmask = jnp.stack([(t // sz) == (u // sz) for sz in sizes]).astype(F32)
    return cmask, lmask


def _gla_call(st, w2s, b_gk, sinit, lay, reverse, other=None):
    bsz = st["qc"].shape[0]
    di = 1 if reverse else 0
    nb = GLA_BATCH
    tile = lambda c: lay.spec(c, lambda b, g: (b,), reverse, batch=nb)
    full = lambda shp: pl.BlockSpec(shp, lambda b, g: (0,) * len(shp))
    state_spec = pl.BlockSpec((nb, C_HEADS, CK_PAD, HEAD_PAD), lambda b, g: (b, 0, 0, 0))
    qc, kc, vc, gk = (lay.view(st[n]) for n in ("qc", "kc", "vc", "gt"))
    cmask, lmask = _scan_masks(reverse)
    o, sfin = pl.pallas_call(
        functools.partial(_gla_kernel, lay, reverse),
        grid=(bsz // nb, lay.ngrp),
        in_specs=[tile(CK_W), tile(CK_W), tile(CV_W), tile(LANES),
                  pl.BlockSpec((1, LANES, CK_W), lambda b, g: (di, 0, 0)),
                  pl.BlockSpec((1, 1, CK_W), lambda b, g: (di, 0, 0)),
                  full(cmask.shape), full(lmask.shape), state_spec] + ([tile(CV_W)] if reverse else []),
        out_specs=[tile(CV_W), state_spec],
        out_shape=[jax.ShapeDtypeStruct(vc.shape, F32),
                   jax.ShapeDtypeStruct(sinit.shape, F32)],
        scratch_shapes=[pltpu.VMEM((nb, C_HEADS, CK_PAD, HEAD_PAD), F32),
                        pltpu.VMEM((nb, CK_W, CV_W), BF16)],
        compiler_params=_cparams(2),
        name="gla_scan_bwd" if reverse else "gla_scan_fwd",
    )(qc, kc, vc, gk, w2s, b_gk, cmask, lmask, sinit, *([lay.view(other)] if reverse else []))
    return lay.unview(o), sfin


def _mix_ffn_kernel(final_norm, x_ref, mod_ref, ha_ref, hc_ref, oa_ref, gc_ref, s_ref, na_ref, nc_ref, wmix_ref,
                    g2_ref, wg_ref, wu_ref, wd_ref, fg_ref, o_ref):
    def head_norm(xh, gain):
        return xh * _rms_scale(xh, A_HEAD_DIM) * gain

    acc = None
    for j in range(B_WIDTH // LANES):
        r0 = A_PAD + j * LANES
        part = jnp.dot(s_ref[0, j].astype(BF16), wmix_ref[r0:r0 + LANES, :], preferred_element_type=F32)
        acc = part if acc is None else acc + part
    for h in range(A_HEADS):
        sl = slice(h * HEAD_PAD, (h + 1) * HEAD_PAD)
        a = head_norm(ha_ref[0, h], na_ref[:, sl])
        c = head_norm(hc_ref[0, h], nc_ref[:, sl])
        ah = (a * jax.nn.sigmoid(oa_ref[0, h].astype(F32))).astype(BF16)
        ch = (c * _silu(gc_ref[0, h].astype(F32))).astype(BF16)
        acc = acc + jnp.dot(ah, wmix_ref[h * HEAD_PAD:(h + 1) * HEAD_PAD, :], preferred_element_type=F32)
        c0 = A_PAD + B_WIDTH + h * HEAD_PAD
        acc = acc + jnp.dot(ch, wmix_ref[c0:c0 + HEAD_PAD, :], preferred_element_type=F32)
    x1 = x_ref[0] + mod_ref[0, 2:3, :] * acc

    y = x1 * _rms_scale(x1, x1.shape[-1]) * g2_ref[...]
    hmod = (y * (1.0 + mod_ref[0, 4:5, :]) + mod_ref[0, 3:4, :]).astype(BF16)
    gate = jnp.dot(hmod, wg_ref[...], preferred_element_type=F32)
    up = jnp.dot(hmod, wu_ref[...], preferred_element_type=F32)
    act = (_silu(gate) * up).astype(BF16)
    x2 = x1 + mod_ref[0, 5:6, :] * jnp.dot(act, wd_ref[...], preferred_element_type=F32)
    if final_norm:
        x2 = x2 * _rms_scale(x2, x2.shape[-1]) * fg_ref[...]
    o_ref[0] = x2


def _mix_ffn_call(x, mod, ha, hc, oa, gc, s, na, nc, w_mix, g2, w_ffn_in, w_ffn_out, final_g, final_norm, tm):
    bsz, t, d = x.shape
    d_ff = w_ffn_out.shape[0]
    mod_map = (lambda b, i: (b, 0, 0)) if mod.shape[0] > 1 else (lambda b, i: (0, 0, 0))
    row = lambda c: pl.BlockSpec((1, tm, c), lambda b, i: (b, i, 0))
    chunked = lambda c: pl.BlockSpec((1, c // LANES, tm, LANES), lambda b, i: (b, 0, i, 0))
    full = lambda shp: pl.BlockSpec(shp, lambda b, i: (0,) * len(shp))
    half = lambda j: pl.BlockSpec((d, d_ff), lambda b, i: (0, j), pipeline_mode=pl.Buffered(1))
    return pl.pallas_call(
        functools.partial(_mix_ffn_kernel, final_norm),
        grid=(bsz, t // tm),
        in_specs=[row(d), pl.BlockSpec((1, 6, d), mod_map), chunked(A_PAD), chunked(CV_W),
                  chunked(A_PAD), chunked(CV_W), chunked(B_WIDTH),
                  full((1, A_PAD)), full((1, CV_W)), _resident(w_mix.shape, 2),
                  full((1, d)), half(0), half(1), _resident(w_ffn_out.shape, 2), full((1, d))],
        out_specs=row(d),
        out_shape=jax.ShapeDtypeStruct(x.shape, F32),
        compiler_params=_cparams(2),
        name="mix_ffn",
    )(x, mod, ha, hc, oa, gc, s, na, nc, w_mix, g2, w_ffn_in, w_ffn_in, w_ffn_out, final_g)


def _pad_heads(w, n_heads, dim, pad, axis=-1):
    w = jnp.moveaxis(w, axis, -1)
    lead = w.shape[:-1]
    w = w.reshape(lead + (n_heads, dim))
    w = jnp.pad(w, [(0, 0)] * len(lead) + [(0, 0), (0, pad - dim)])
    return jnp.moveaxis(w.reshape(lead + (n_heads * pad,)), -1, axis)


def _pad_to(w, width):
    return jnp.pad(w, [(0, 0)] * (w.ndim - 1) + [(0, width - w.shape[-1])])


def _prep_w_in(w_in):
    a_w = A_HEADS * A_HEAD_DIM
    ck = C_HEADS * C_HEAD_DK
    cv = C_HEADS * C_HEAD_DV
    sizes = (a_w, a_w, a_w, a_w, N_DIR * 2 * A_HEADS, B_WIDTH, B_WIDTH, ck, ck, cv, cv, N_DIR * C_GATE_RANK)
    pts = [sum(sizes[:i + 1]) for i in range(len(sizes) - 1)]
    qa, ka, va, oa, ga, ub, vb, qc, kc, vc, gc, gkc = jnp.split(w_in, pts, axis=-1)
    pa = lambda w: _pad_heads(w, A_HEADS, A_HEAD_DIM, HEAD_PAD)
    pk = lambda w: _pad_heads(w, C_HEADS, C_HEAD_DK, CK_PAD)
    pv = lambda w: _pad_heads(w, C_HEADS, C_HEAD_DV, HEAD_PAD)
    gates = _pad_to(jnp.concatenate([ga, jnp.tile(gkc[:, :C_GATE_RANK], (1, GK_COPIES)),
                                     jnp.tile(gkc[:, C_GATE_RANK:], (1, GK_COPIES))], axis=1), LANES)
    cols = [pa(qa), pa(ka), pa(va), pa(oa), gates, ub, vb, pk(qc), pk(kc), pv(vc), pv(gc)]
    return jnp.concatenate(cols, axis=-1).astype(BF16)


def _prep_w_gk2(w_gk2):
    w = _pad_heads(w_gk2, C_HEADS, C_HEAD_DK, CK_PAD)
    hi = w.astype(BF16)
    lo = (w - hi.astype(F32)).astype(BF16)
    stacked = jnp.concatenate([hi, lo, hi], axis=1)
    return jnp.stack([jnp.pad(stacked[d], ((GK_OFF(d), LANES - GK_OFF(d) - GK_COPIES * C_GATE_RANK), (0, 0)))
                      for d in range(N_DIR)])


def _prep_w_out(w_out):
    a_w = A_HEADS * A_HEAD_DIM
    wa, wb, wc = w_out[:a_w], w_out[a_w:a_w + B_WIDTH], w_out[a_w + B_WIDTH:]
    wa = _pad_heads(wa, A_HEADS, A_HEAD_DIM, HEAD_PAD, axis=0)
    wc = _pad_heads(wc, C_HEADS, C_HEAD_DV, HEAD_PAD, axis=0)
    return jnp.concatenate([wa, wb, wc], axis=0).astype(BF16)


def _mixer_scans(st, lw, a_state, c_state, lay):
    qconv, kconv, s = _seq_prep_call(st, lw["conv"], lw["sgu_g"], lw["sgu_w"], lw["sgu_bias"], lay)
    ha, hc, a_fin, c_fin = None, None, [], []
    for di, reverse in enumerate((False, True)):
        ha, cfin, mfin = _mlstm_call(qconv, kconv, st, lw["gate_b"], a_state[di][0], a_state[di][1], lay, reverse, ha)
        a_fin.append((cfin, mfin))
        hc, sfin = _gla_call(st, lw["w_gk2"], lw["b_gk"], c_state[di], lay, reverse, hc)
        c_fin.append(sfin)
    return ha, hc, s, a_fin, c_fin


def kernel(x, c, ctx, c_ctx, norm1_g, norm2_g, w_ada, b_ada, w_in, mlstm_conv, mlstm_gate_b,
           mlstm_norm_g, gla_w_gk2, gla_b_gk, gla_norm_g, sgu_norm_g, sgu_w, sgu_b, w_out,
           w_ffn_in, w_ffn_out, final_g):
    bsz, seq, d = x.shape
    ctx_len = ctx.shape[1]
    depth = w_in.shape[0]
    n_cond = 2 * SUBLANES
    cond = jnp.zeros((n_cond, d), F32).at[:bsz].set(c).at[bsz].set(c_ctx)
    mods = _ada_call(cond, w_ada, b_ada).reshape(depth, n_cond, 6, d)

    fg = final_g.reshape(1, d)
    x_lat, x_ctx = x, ctx
    for l in range(depth):
        need_ctx = l < depth - 1
        mod_lat = mods[l, :bsz]
        mod_ctx = mods[l, bsz:bsz + 1]
        gb = mlstm_gate_b[l].reshape(N_DIR, 1, 2 * A_HEADS)
        lw = {
            "conv": jnp.stack([_pad_heads(mlstm_conv[l][:, :A_HEADS * A_HEAD_DIM], A_HEADS, A_HEAD_DIM, HEAD_PAD),
                               _pad_heads(mlstm_conv[l][:, A_HEADS * A_HEAD_DIM:], A_HEADS, A_HEAD_DIM, HEAD_PAD)]),
            "gate_b": jnp.stack([jnp.pad(gb[d], ((0, 0), (GA_OFF(d), LANES - GA_OFF(d) - 2 * A_HEADS)))
                                 for d in range(N_DIR)]),
            "w_gk2": _prep_w_gk2(gla_w_gk2[l]),
            "b_gk": _pad_heads(gla_b_gk[l], C_HEADS, C_HEAD_DK, CK_PAD).reshape(N_DIR, 1, CK_W),
            "sgu_g": sgu_norm_g[l].reshape(1, B_WIDTH),
            "sgu_w": sgu_w[l].astype(BF16),
            "sgu_bias": jnp.repeat(sgu_b[l].T, B_GROUP_DIM, axis=1),
        }
        w_in_l = _prep_w_in(w_in[l])
        w_out_l = _prep_w_out(w_out[l])
        na = _pad_heads(mlstm_norm_g[l], A_HEADS, A_HEAD_DIM, HEAD_PAD).reshape(1, A_PAD)
        nc = _pad_heads(jnp.tile(gla_norm_g[l], C_HEADS), C_HEADS, C_HEAD_DV, HEAD_PAD).reshape(1, CV_W)
        g1 = norm1_g[l].reshape(1, d)
        g2 = norm2_g[l].reshape(1, d)
        w_ffn_in_l = w_ffn_in[l].astype(BF16)
        w_ffn_out_l = w_ffn_out[l].astype(BF16)

        st_ctx = _proj_in_call(x_ctx, mod_ctx, g1, w_in_l, tm=_row_tile(ctx_len, PROJ_ROWS))
        st_lat = _proj_in_call(x_lat, mod_lat, g1, w_in_l, tm=_row_tile(seq, PROJ_ROWS))

        a0 = (jnp.zeros((bsz, A_HEADS, HEAD_PAD, 2 * HEAD_PAD), F32), jnp.zeros((bsz, SUBLANES, LANES), F32))
        c0 = jnp.zeros((bsz, C_HEADS, CK_PAD, HEAD_PAD), F32)
        lay_ctx = _SeqLayout(ctx_len, False)
        lay_lat = _SeqLayout(seq, l % 2 == 1)
        ha_c, hc_c, s_c, a_state, c_state = _mixer_scans(st_ctx, lw, (a0, a0), (c0, c0), lay_ctx)
        ha_l, hc_l, s_l, _, _ = _mixer_scans(st_lat, lw, a_state, c_state, lay_lat)

        x_lat = _mix_ffn_call(x_lat, mod_lat, ha_l, hc_l, st_lat["oa"], st_lat["gc"], s_l, na, nc, w_out_l,
                              g2, w_ffn_in_l, w_ffn_out_l, fg, not need_ctx, tm=_row_tile(seq, TAIL_ROWS))
        if need_ctx:
            x_ctx = _mix_ffn_call(x_ctx, mod_ctx, ha_c, hc_c, st_ctx["oa"], st_ctx["gc"], s_c, na, nc, w_out_l,
                                  g2, w_ffn_in_l, w_ffn_out_l, fg, False, tm=_row_tile(ctx_len, TAIL_ROWS))
    return x_lat
```

```python
import functools
import math

import jax
import jax.numpy as jnp
from jax import lax
from jax.experimental import pallas as pl
from jax.experimental.pallas import tpu as pltpu

F32 = jnp.float32
BF16 = jnp.bfloat16
HIGHEST = lax.Precision.HIGHEST

LANES = 128
SUBLANES = 8
VMEM_LIMIT_BYTES = 56 * 1024 * 1024

GRID_W = 64
RMS_EPS = 1e-6
A_HEADS = 4
A_HEAD_DIM = 96
B_GROUPS = 4
B_GROUP_DIM = 64
B_WIDTH = B_GROUPS * B_GROUP_DIM
C_HEADS = 4
C_HEAD_DK = 48
C_HEAD_DV = 96
C_GATE_RANK = 16
C_GATE_NORM = 16.0
N_DIR = 2

HEAD_PAD = LANES
A_PAD = A_HEADS * HEAD_PAD
CK_PAD = 64
CK_W = C_HEADS * CK_PAD
CV_W = C_HEADS * HEAD_PAD

SEQ_BLOCK = 128
SEQ_GROUP = SUBLANES
MLSTM_BATCH = 2
GLA_BATCH = 2
GK_COPIES = 3
GA_OFF = lambda d: 2 * A_HEADS * d
GK_OFF = lambda d: N_DIR * 2 * A_HEADS + GK_COPIES * C_GATE_RANK * d

PROJ_ROWS = 512
TAIL_ROWS = 512
ADA_COLS = 1536

_IN_LAYOUT = (("qa", A_PAD), ("ka", A_PAD), ("va", A_PAD), ("oa", A_PAD), ("gt", LANES),
              ("ub", B_WIDTH), ("vb", B_WIDTH), ("qc", CK_W), ("kc", CK_W), ("vc", CV_W), ("gc", CV_W))
N_IN_PAD = sum(w for _, w in _IN_LAYOUT)
_BF16_STREAMS = ("oa", "gc")


def _cparams(n_axes):
    return pltpu.CompilerParams(dimension_semantics=("arbitrary",) * n_axes,
                                vmem_limit_bytes=VMEM_LIMIT_BYTES)


def _resident(shape, n_grid):
    zeros = (0,) * len(shape)
    maps = {2: lambda a, b: zeros, 3: lambda a, b, c: zeros}
    return pl.BlockSpec(shape, maps[n_grid], pipeline_mode=pl.Buffered(1))


def _row_tile(t_len, want):
    return min(want, t_len)


def _log_sigmoid(x):
    return jnp.minimum(x, 0.0) - jnp.log(1.0 + jnp.exp(-jnp.abs(x)))


def _silu(x):
    return x * jax.nn.sigmoid(x)


def _gelu_tanh(x):
    c = math.sqrt(2.0 / math.pi)
    return 0.5 * x * (1.0 + jnp.tanh(c * (x + 0.044715 * (x * x * x))))


def _rms_scale(x, n):
    return lax.rsqrt(jnp.sum(x * x, axis=-1, keepdims=True) * (1.0 / n) + RMS_EPS)


def _split_hi_lo(x):
    hi = x.astype(BF16)
    lo = (x - hi.astype(F32)).astype(BF16)
    return hi, lo


def _ada_kernel(s_ref, w_ref, b_ref, o_ref):
    s = _silu(s_ref[...])
    o_ref[0] = jnp.dot(s, w_ref[0], precision=HIGHEST, preferred_element_type=F32) + b_ref[0]


def _ada_call(cond, w_ada, b_ada):
    depth, d, n6 = w_ada.shape
    nb = cond.shape[0]
    tn = ADA_COLS
    return pl.pallas_call(
        _ada_kernel,
        grid=(depth, n6 // tn),
        in_specs=[pl.BlockSpec((nb, d), lambda l, j: (0, 0)),
                  pl.BlockSpec((1, d, tn), lambda l, j: (l, 0, j)),
                  pl.BlockSpec((1, 1, tn), lambda l, j: (l, 0, j))],
        out_specs=pl.BlockSpec((1, nb, tn), lambda l, j: (l, 0, j)),
        out_shape=jax.ShapeDtypeStruct((depth, nb, n6), F32),
        compiler_params=_cparams(2),
        name="ada_mod",
    )(cond, w_ada, b_ada.reshape(depth, 1, n6))


def _proj_in_kernel(x_ref, mod_ref, g_ref, w_ref, *out_refs):
    x = x_ref[0]
    y = x * _rms_scale(x, x.shape[-1]) * g_ref[...]
    h = (y * (1.0 + mod_ref[0, 1:2, :]) + mod_ref[0, 0:1, :]).astype(BF16)
    off = 0
    for (_, width), o_ref in zip(_IN_LAYOUT, out_refs):
        res = jnp.dot(h, w_ref[:, off:off + width], preferred_element_type=F32)
        for j in range(width // LANES):
            o_ref[0, j] = res[:, j * LANES:(j + 1) * LANES].astype(o_ref.dtype)
        off += width


def _proj_in_call(x, mod, g, w, tm):
    bsz, t, d = x.shape
    mod_b = mod.shape[0]
    mod_map = (lambda b, i: (b, 0, 0)) if mod_b > 1 else (lambda b, i: (0, 0, 0))
    out_shapes, out_specs = [], []
    for name, wd in _IN_LAYOUT:
        nch = wd // LANES
        out_shapes.append(jax.ShapeDtypeStruct((bsz, nch, t, LANES), BF16 if name in _BF16_STREAMS else F32))
        out_specs.append(pl.BlockSpec((1, nch, tm, LANES), lambda b, i: (b, 0, i, 0)))
    outs = pl.pallas_call(
        _proj_in_kernel,
        grid=(bsz, t // tm),
        in_specs=[pl.BlockSpec((1, tm, d), lambda b, i: (b, i, 0)),
                  pl.BlockSpec((1, 6, d), mod_map),
                  pl.BlockSpec((1, d), lambda b, i: (0, 0)),
                  _resident((d, N_IN_PAD), 2)],
        out_specs=out_specs,
        out_shape=out_shapes,
        compiler_params=_cparams(2),
        name="proj_in",
    )(x, mod, g, w)
    return dict(zip([n for n, _ in _IN_LAYOUT], outs))


class _SeqLayout:
    def __init__(self, t_len, column_major):
        self.cm = column_major
        self.t_len = t_len
        self.nblk = t_len // SEQ_BLOCK
        self.group = min(SEQ_GROUP, self.nblk)
        self.ngrp = self.nblk // self.group
        if column_major:
            assert t_len == SEQ_BLOCK * GRID_W and self.group == SEQ_GROUP

    def view(self, a):
        if not self.cm:
            return a
        return a.reshape(a.shape[:-2] + (SEQ_BLOCK, GRID_W, LANES))

    def unview(self, a):
        if not self.cm:
            return a
        return a.reshape(a.shape[:-3] + (self.t_len, LANES))

    def spec(self, c, lead_map, reverse=False, batch=1):
        def grp(g):
            return self.ngrp - 1 - g if reverse else g

        n_lead = len(lead_map(0, 0))
        lead_blk = (1,) * (n_lead - 1) + (batch,)
        nch = c // LANES
        if self.cm:
            return pl.BlockSpec(lead_blk + (nch, SEQ_BLOCK, self.group, LANES),
                                lambda b, g: lead_map(b, g) + (0, 0, grp(g), 0))
        return pl.BlockSpec(lead_blk + (nch, self.group * SEQ_BLOCK, LANES),
                            lambda b, g: lead_map(b, g) + (0, grp(g), 0))

    def halo_specs(self, c):
        nch = c // LANES
        if self.cm:
            shp = (1, nch, SUBLANES, self.group, LANES)
            prev = pl.BlockSpec(shp, lambda b, g: (b, 0, SEQ_BLOCK // SUBLANES - 1, jnp.maximum(g - 1, 0), 0))
            nxt = pl.BlockSpec(shp, lambda b, g: (b, 0, 0, jnp.minimum(g + 1, self.ngrp - 1), 0))
        else:
            per = self.group * SEQ_BLOCK // SUBLANES
            last = self.t_len // SUBLANES - 1
            shp = (1, nch, SUBLANES, LANES)
            prev = pl.BlockSpec(shp, lambda b, g: (b, 0, jnp.maximum(g * per - 1, 0), 0))
            nxt = pl.BlockSpec(shp, lambda b, g: (b, 0, jnp.minimum((g + 1) * per, last), 0))
        return prev, nxt

    @staticmethod
    def _cat(pieces):
        return pieces[0] if len(pieces) == 1 else jnp.concatenate(pieces, axis=1)

    def halo_prev_row(self, ref):
        r = SUBLANES - 1
        if self.cm:
            return self._cat([ref[0, j, r:r + 1, self.group - 1, :] for j in range(ref.shape[1])])
        return self._cat([ref[0, j, r:r + 1, :] for j in range(ref.shape[1])])

    def halo_next_row(self, ref):
        if self.cm:
            return self._cat([ref[0, j, 0:1, 0, :] for j in range(ref.shape[1])])
        return self._cat([ref[0, j, 0:1, :] for j in range(ref.shape[1])])

    def _piece(self, ref, lead, j, i):
        if self.cm:
            flat = ref.at[lead + (j,)].reshape(self.group * SEQ_BLOCK, LANES)
            return flat, (pl.ds(i, SEQ_BLOCK, stride=self.group), slice(None))
        start = i * SEQ_BLOCK
        if not isinstance(i, int):
            start = pl.multiple_of(start, SEQ_BLOCK)
        return ref, lead + (j, pl.ds(start, SEQ_BLOCK), slice(None))

    def load(self, ref, i, lead=(0,)):
        pieces = []
        for j in range(ref.shape[len(lead)]):
            r, idx = self._piece(ref, lead, j, i)
            pieces.append(r[idx])
        return self._cat(pieces)

    def store(self, ref, i, val, lead=(0,)):
        for j in range(ref.shape[len(lead)]):
            r, idx = self._piece(ref, lead, j, i)
            r[idx] = val[:, j * LANES:(j + 1) * LANES]

    def row(self, ref, i, r):
        if self.cm:
            return self._cat([ref[0, j, r:r + 1, i, :] for j in range(ref.shape[1])])
        t = i * SEQ_BLOCK + r
        return self._cat([ref[0, j, t:t + 1, :] for j in range(ref.shape[1])])


def _seq_prep_kernel(lay, q_ref, qp_ref, qn_ref, k_ref, kp_ref, kn_ref, u_ref, v_ref, cw_ref, ng_ref, w_ref,
                     bias_ref, qo_ref, ko_ref, s_ref):
    g = pl.program_id(1)
    L = SEQ_BLOCK
    row = lax.broadcasted_iota(jnp.int32, (L, 1), 0)
    has_prev = (g > 0).astype(F32)
    has_next = (g < lay.ngrp - 1).astype(F32)
    grp = lax.broadcasted_iota(jnp.int32, (1, B_WIDTH), 1) // B_GROUP_DIM

    for i in range(lay.group):
        for x_ref, xp_ref, xn_ref, o_ref, w, scale in ((q_ref, qp_ref, qn_ref, qo_ref, cw_ref[0], A_HEAD_DIM ** -0.5),
                                                       (k_ref, kp_ref, kn_ref, ko_ref, cw_ref[1], None)):
            x = lay.load(x_ref, i)
            prev_row = lay.row(x_ref, i - 1, L - 1) if i > 0 else lay.halo_prev_row(xp_ref) * has_prev
            next_row = lay.row(x_ref, i + 1, 0) if i < lay.group - 1 else lay.halo_next_row(xn_ref) * has_next
            x_dn = jnp.where(row == 0, prev_row, pltpu.roll(x, 1, 0))
            x_up = jnp.where(row == L - 1, next_row, pltpu.roll(x, L - 1, 0))
            y = _silu(w[0:1, :] * x_dn + w[1:2, :] * x + w[2:3, :] * x_up)
            lay.store(o_ref, i, y if scale is None else y * scale)

        u = _gelu_tanh(lay.load(u_ref, i))
        v = _gelu_tanh(lay.load(v_ref, i))
        v = v * _rms_scale(v, B_WIDTH) * ng_ref[...]
        mixed = bias_ref[...]
        for gi in range(B_GROUPS):
            vg = jnp.where(grp == gi, v, 0.0).astype(BF16)
            mixed = mixed + jnp.dot(w_ref[gi], vg, preferred_element_type=F32)
        lay.store(s_ref, i, u * mixed)


def _seq_prep_call(st, conv_w, norm_g, w_s, bias, lay):
    bsz = st["qa"].shape[0]
    lead = lambda b, g: (b,)
    tile = lambda c: lay.spec(c, lead)
    hp, hn = lay.halo_specs(A_PAD)
    full = lambda shp: pl.BlockSpec(shp, lambda b, g: (0,) * len(shp))
    qa, ka, ub, vb = (lay.view(st[n]) for n in ("qa", "ka", "ub", "vb"))
    qo, ko, s = pl.pallas_call(
        functools.partial(_seq_prep_kernel, lay),
        grid=(bsz, lay.ngrp),
        in_specs=[tile(A_PAD), hp, hn, tile(A_PAD), hp, hn, tile(B_WIDTH), tile(B_WIDTH),
                  full((2, 3, A_PAD)), full((1, B_WIDTH)), full((B_GROUPS, SEQ_BLOCK, SEQ_BLOCK)),
                  full((SEQ_BLOCK, B_WIDTH))],
        out_specs=[tile(A_PAD), tile(A_PAD), tile(B_WIDTH)],
        out_shape=[jax.ShapeDtypeStruct(qa.shape, F32), jax.ShapeDtypeStruct(ka.shape, F32),
                   jax.ShapeDtypeStruct(ub.shape, F32)],
        compiler_params=_cparams(2),
        name="seq_prep",
    )(qa, qa, qa, ka, ka, ka, ub, vb, conv_w, norm_g, w_s, bias)
    return lay.unview(qo), lay.unview(ko), lay.unview(s)


def _mlstm_kernel(lay, reverse, q_ref, k_ref, v_ref, g_ref, gb_ref, cmask_ref, spread_ref, rowsel_ref,
                  cinit_ref, minit_ref, *rest):
    other_ref = rest[0] if reverse else None
    h_ref, cfin_ref, mfin_ref, c_sc, m_sc = rest[1:] if reverse else rest
    gidx = pl.program_id(1)
    L = SEQ_BLOCK

    @pl.when(gidx == 0)
    def _():
        c_sc[...] = cinit_ref[...]
        m_sc[...] = minit_ref[...]

    glane = lax.broadcasted_iota(jnp.int32, (1, LANES), 1)
    off = GA_OFF(1 if reverse else 0)
    is_i = (glane >= off) & (glane < off + A_HEADS)
    ri = lax.broadcasted_iota(jnp.int32, (L, L), 0)
    ci = lax.broadcasted_iota(jnp.int32, (L, L), 1)
    past = (ci >= ri) if reverse else (ci <= ri)
    last = 0 if reverse else L - 1
    ones = jnp.ones((L, HEAD_PAD), F32)
    twice = lambda x: jnp.concatenate([x, x], axis=1)

    def chunk(ii, carry):
        i = lay.group - 1 - ii if reverse else ii
        rows = range(MLSTM_BATCH)
        q = [lay.load(q_ref, i, (bb,)).astype(BF16) for bb in rows]
        k = [lay.load(k_ref, i, (bb,)).astype(BF16) for bb in rows]
        v = [lay.load(v_ref, i, (bb,)) for bb in rows]
        g = [lay.load(g_ref, i, (bb,)) + gb_ref[0] for bb in rows]
        lf = [_split_hi_lo(_log_sigmoid(g[bb])) for bb in rows]
        cs = [jnp.dot(cmask_ref[...], jnp.concatenate(lf[bb], axis=1), preferred_element_type=F32) for bb in rows]
        bc = [cs[bb][:, 0:LANES] + cs[bb][:, LANES:2 * LANES] for bb in rows]
        x = [jnp.concatenate(_split_hi_lo(jnp.where(is_i, g[bb], bc[bb])), axis=1) for bb in rows]
        spread = [jnp.dot(x[bb], spread_ref[...], preferred_element_type=F32) for bb in rows]
        c_rows = [lax.dot_general(rowsel_ref[...], x[bb], (((1,), (1,)), ((), ())), preferred_element_type=F32)
                  for bb in rows]

        units = [(bb, h) for bb in rows for h in range(A_HEADS)]
        each = lambda f: {u: f(*u) for u in units}
        sl = lambda h: slice(h * HEAD_PAD, (h + 1) * HEAD_PAD)
        b_rep = each(lambda bb, h: spread[bb][:, sl(h)])
        c_rep = each(lambda bb, h: spread[bb][:, sl(A_HEADS + h)])
        c_row = each(lambda bb, h: c_rows[bb][h:h + 1, :])
        m_prev = each(lambda bb, h: m_sc[bb, h:h + 1, :])
        c2 = each(lambda bb, h: c_sc[bb, h])
        v2 = each(lambda bb, h: jnp.concatenate([v[bb][:, sl(h)], ones], axis=1))
        u_rep = each(lambda bb, h: jnp.maximum(
            m_prev[bb, h], jnp.max(jnp.where(past, c_row[bb, h], -jnp.inf), axis=1, keepdims=True)))
        qk = each(lambda bb, h: lax.dot_general(q[bb][:, sl(h)], k[bb][:, sl(h)], (((1,), (1,)), ((), ())),
                                                preferred_element_type=F32))
        s = each(lambda bb, h: (qk[bb, h] * jnp.exp(jnp.where(past, c_row[bb, h] - u_rep[bb, h], -jnp.inf))
                                ).astype(BF16))
        inter = each(lambda bb, h: jnp.exp(m_prev[bb, h] - u_rep[bb, h]))
        carry_in = each(lambda bb, h: jnp.dot(q[bb][:, sl(h)], c2[bb, h].astype(BF16), preferred_element_type=F32))
        num2 = each(lambda bb, h: jnp.dot(s[bb, h], v2[bb, h].astype(BF16), preferred_element_type=F32)
                    + twice(inter[bb, h]) * carry_in[bb, h])
        inv = each(lambda bb, h: 1.0 / jnp.maximum(jnp.abs(num2[bb, h][:, HEAD_PAD:]),
                                                   jnp.exp(-(b_rep[bb, h] + u_rep[bb, h]))))
        for bb in rows:
            out = jnp.concatenate([num2[bb, h][:, :HEAD_PAD] * inv[bb, h] for h in range(A_HEADS)], axis=1)
            if other_ref is not None:
                out = out + lay.load(other_ref, i, (bb,))
            lay.store(h_ref, i, out, (bb,))

        b_last = each(lambda bb, h: b_rep[bb, h][last:last + 1, :])
        w_log = each(lambda bb, h: b_last[bb, h] + c_rep[bb, h])
        m_new = each(lambda bb, h: jnp.maximum(b_last[bb, h] + m_prev[bb, h],
                                               jnp.max(w_log[bb, h], axis=0, keepdims=True)))
        wv2 = each(lambda bb, h: (twice(jnp.exp(w_log[bb, h] - m_new[bb, h])) * v2[bb, h]).astype(BF16))
        for bb, h in units:
            upd = lax.dot_general(k[bb][:, sl(h)], wv2[bb, h], (((0,), (0,)), ((), ())), preferred_element_type=F32)
            decay = jnp.exp(b_last[bb, h] + m_prev[bb, h] - m_new[bb, h])
            c_sc[bb, h] = twice(decay) * c2[bb, h] + upd
            m_sc[bb, h:h + 1, :] = m_new[bb, h]
        return carry

    lax.fori_loop(0, lay.group, chunk, 0)

    @pl.when(gidx == lay.ngrp - 1)
    def _():
        cfin_ref[...] = c_sc[...]
        mfin_ref[...] = m_sc[...]


def _mlstm_gate_matrices(off):
    lane = jnp.arange(LANES) - off
    blocks = []
    for h in range(A_HEADS):
        blocks.append(jnp.broadcast_to((lane == A_HEADS + h).astype(F32)[:, None], (LANES, LANES)))
    for h in range(A_HEADS):
        col = (lane == h).astype(F32) - (lane == A_HEADS + h).astype(F32)
        blocks.append(jnp.broadcast_to(col[:, None], (LANES, LANES)))
    half = jnp.concatenate(blocks, axis=1)
    spread = jnp.concatenate([half, half], axis=0).astype(BF16)
    sel = jnp.stack([(lane == h).astype(F32) - (lane == A_HEADS + h).astype(F32) for h in range(A_HEADS)]
                    + [jnp.zeros((LANES,), F32)] * (2 * SUBLANES - A_HEADS))
    rowsel = jnp.concatenate([sel, sel], axis=1).astype(BF16)
    return spread, rowsel


def _mlstm_call(qconv, kconv, st, gate_b, cinit, minit, lay, reverse, other=None):
    bsz = qconv.shape[0]
    di = 1 if reverse else 0
    nb = MLSTM_BATCH
    tile = lambda c: lay.spec(c, lambda b, g: (b,), reverse, batch=nb)
    full = lambda shp: pl.BlockSpec(shp, lambda b, g: (0,) * len(shp))
    c_spec = pl.BlockSpec((nb, A_HEADS, HEAD_PAD, 2 * HEAD_PAD), lambda b, g: (b, 0, 0, 0))
    m_spec = pl.BlockSpec((nb, SUBLANES, LANES), lambda b, g: (b, 0, 0))
    qa, ka, va, ga = lay.view(qconv), lay.view(kconv), lay.view(st["va"]), lay.view(st["gt"])
    cmask, _ = _scan_masks(reverse)
    spread, rowsel = _mlstm_gate_matrices(GA_OFF(di))
    h, cfin, mfin = pl.pallas_call(
        functools.partial(_mlstm_kernel, lay, reverse),
        grid=(bsz // nb, lay.ngrp),
        in_specs=[tile(A_PAD), tile(A_PAD), tile(A_PAD), tile(LANES),
                  pl.BlockSpec((1, 1, LANES), lambda b, g: (di, 0, 0)),
                  full(cmask.shape), full(spread.shape), full(rowsel.shape),
                  c_spec, m_spec] + ([tile(A_PAD)] if reverse else []),
        out_specs=[tile(A_PAD), c_spec, m_spec],
        out_shape=[jax.ShapeDtypeStruct(qa.shape, F32),
                   jax.ShapeDtypeStruct(cinit.shape, F32),
                   jax.ShapeDtypeStruct(minit.shape, F32)],
        scratch_shapes=[pltpu.VMEM((nb, A_HEADS, HEAD_PAD, 2 * HEAD_PAD), F32),
                        pltpu.VMEM((nb, SUBLANES, LANES), F32)],
        compiler_params=_cparams(2),
        name="mlstm_scan_bwd" if reverse else "mlstm_scan_fwd",
    )(qa, ka, va, ga, gate_b, cmask, spread, rowsel, cinit, minit, *([lay.view(other)] if reverse else []))
    return lay.unview(h), cfin, mfin


def _gla_kernel(lay, reverse, q_ref, k_ref, v_ref, gk_ref, w2_ref, bgk_ref, cmask_ref, lmask_ref, sinit_ref,
                *rest):
    other_ref = rest[0] if reverse else None
    o_ref, sfin_ref, s_sc, sbd_sc = rest[1:] if reverse else rest
    gidx = pl.program_id(1)
    L = SEQ_BLOCK

    @pl.when(gidx == 0)
    def _():
        s_sc[...] = sinit_ref[...]
        sbd_sc[...] = jnp.zeros_like(sbd_sc)
        for bb in range(GLA_BATCH):
            for h in range(C_HEADS):
                sbd_sc[bb, h * CK_PAD:(h + 1) * CK_PAD, h * HEAD_PAD:(h + 1) * HEAD_PAD] = \
                    sinit_ref[bb, h].astype(BF16)

    glane = lax.broadcasted_iota(jnp.int32, (1, LANES), 1)
    gk_lo = GK_OFF(1 if reverse else 0) + 2 * C_GATE_RANK
    row = lax.broadcasted_iota(jnp.int32, (L, 1), 0)
    hlane = glane // CK_PAD
    col = lambda x, h: x[:, (h // 2) * LANES:(h // 2 + 1) * LANES]
    last = 0 if reverse else L - 1
    ones = jnp.ones((L, HEAD_PAD), BF16)

    def rows_from(x, idx_of_row_block, rows_per_block):
        n = L // rows_per_block
        return jnp.concatenate([jnp.broadcast_to(x[idx_of_row_block(i):idx_of_row_block(i) + 1],
                                                 (rows_per_block, x.shape[1])) for i in range(n)], axis=0)

    def chunk(ii, carry):
        i = lay.group - 1 - ii if reverse else ii
        rows = range(GLA_BATCH)
        heads = range(C_HEADS)
        g = [lay.load(gk_ref, i, (bb,)) for bb in rows]
        g_hi = [g[bb].astype(BF16).astype(F32) for bb in rows]
        g_split = [jnp.where(glane < gk_lo, g_hi[bb], g[bb] - g_hi[bb]).astype(BF16) for bb in rows]
        la = [_log_sigmoid(jnp.dot(g_split[bb], w2_ref[0], preferred_element_type=F32) + bgk_ref[0])
              * (1.0 / C_GATE_NORM) for bb in rows]
        la_split = [_split_hi_lo(la[bb]) for bb in rows]
        cs = [jnp.dot(cmask_ref[...], jnp.concatenate(la_split[bb], axis=1), preferred_element_type=F32)
              for bb in rows]
        bs = [cs[bb][:, 0:CK_W] + cs[bb][:, CK_W:2 * CK_W] for bb in rows]
        qs = [lay.load(q_ref, i, (bb,)) * (C_HEAD_DK ** -0.5) for bb in rows]
        kk = [lay.load(k_ref, i, (bb,)) for bb in rows]
        vb = [lay.load(v_ref, i, (bb,)).astype(BF16) for bb in rows]
        att = [[None] * C_HEADS for _ in rows]

        def add_level(bb, q_t, k_mul, level, k_rows=None):
            k_all = kk[bb] if k_mul is None else kk[bb] * k_mul
            if k_rows is not None:
                k_all = jnp.where(k_rows, k_all, 0.0)
            k_b = k_all.astype(BF16)
            for p in range(C_HEADS // 2):
                tile = slice(p * LANES, (p + 1) * LANES)
                q_pair = q_t[:, tile]
                q_stack = jnp.concatenate([jnp.where(hlane == 0, q_pair, 0.0), jnp.where(hlane == 1, q_pair, 0.0)],
                                          axis=0).astype(BF16)
                a2 = lax.dot_general(q_stack, k_b[:, tile], (((1,), (1,)), ((), ())), preferred_element_type=F32)
                for j in range(2):
                    h = 2 * p + j
                    a = a2[j * L:(j + 1) * L]
                    if level is not None:
                        a = a * lmask_ref[level]
                    att[bb][h] = a if att[bb][h] is None else att[bb][h] + a

        def level_factors(bb, m):
            bnd = m if reverse else m - 1
            if m >= SUBLANES:
                zero = jnp.zeros((m, CK_W), F32)
                q_parts, k_parts = [], []
                for base in range(0, L, 2 * m):
                    pref = bs[bb][base + bnd:base + bnd + 1]
                    lo, hi = slice(base, base + m), slice(base + m, base + 2 * m)
                    q_half, k_half = (lo, hi) if reverse else (hi, lo)
                    q_piece = qs[bb][q_half] * jnp.exp(bs[bb][q_half] - pref)
                    k_piece = jnp.exp(pref - bs[bb][k_half])
                    q_parts += [q_piece, zero] if reverse else [zero, q_piece]
                    k_parts += [zero, k_piece] if reverse else [k_piece, zero]
                return jnp.concatenate(q_parts, axis=0), jnp.concatenate(k_parts, axis=0)
            pos = row % (2 * m)
            is_q = (pos < m) if reverse else (pos >= m)
            if m == 1:
                return jnp.where(is_q, qs[bb] * jnp.exp(la[bb]), 0.0), None
            if 2 * m >= SUBLANES:
                pref = rows_from(bs[bb], lambda r: r * 2 * m + bnd, 2 * m)
            else:
                p0 = rows_from(bs[bb], lambda r: r * SUBLANES + bnd, SUBLANES)
                p1 = rows_from(bs[bb], lambda r: r * SUBLANES + 2 * m + bnd, SUBLANES)
                pref = jnp.where(row % SUBLANES < 2 * m, p0, p1)
            return (qs[bb] * jnp.exp(jnp.where(is_q, bs[bb] - pref, -jnp.inf)),
                    jnp.exp(jnp.where(is_q, -jnp.inf, pref - bs[bb])))

        level = 0
        m = L // 2
        while m >= 1:
            key_rows = (row % 2 == (1 if reverse else 0)) if m == 1 else None
            for bb in rows:
                q_t, k_mul = level_factors(bb, m)
                add_level(bb, q_t, k_mul, None if m == L // 2 else level, key_rows)
            level += 1
            m //= 2
        for bb in rows:
            add_level(bb, qs[bb], None, level)

        o_inter = [jnp.dot((qs[bb] * jnp.exp(bs[bb])).astype(BF16), sbd_sc[bb], preferred_element_type=F32)
                   for bb in rows]
        for bb in rows:
            o_heads = []
            for h in heads:
                cols = slice(h * HEAD_PAD, (h + 1) * HEAD_PAD)
                o_heads.append(o_inter[bb][:, cols] + jnp.dot(att[bb][h].astype(BF16), vb[bb][:, cols],
                                                              preferred_element_type=F32))
            out = jnp.concatenate(o_heads, axis=1)
            if other_ref is not None:
                out = out + lay.load(other_ref, i, (bb,))
            lay.store(o_ref, i, out, (bb,))

        tdot = lambda a: lax.dot_general(a, ones, (((0,), (0,)), ((), ())), preferred_element_type=F32)
        for bb in rows:
            btot = bs[bb][last:last + 1]
            ke_t = (kk[bb] * jnp.exp(btot - bs[bb])).T.astype(BF16)
            dec_col = jnp.exp(tdot(la_split[bb][0]) + tdot(la_split[bb][1]))
            for h in heads:
                krows = slice(h * CK_PAD, (h + 1) * CK_PAD)
                cols = slice(h * HEAD_PAD, (h + 1) * HEAD_PAD)
                upd = jnp.dot(ke_t[krows], vb[bb][:, cols], preferred_element_type=F32)
                s_new = dec_col[krows] * s_sc[bb, h] + upd
                s_sc[bb, h] = s_new
                sbd_sc[bb, krows, cols] = s_new.astype(BF16)
        return carry

    lax.fori_loop(0, lay.group, chunk, 0)

    @pl.when(gidx == lay.ngrp - 1)
    def _():
        sfin_ref[...] = s_sc[...]


def _scan_masks(reverse):
    t = jnp.arange(SEQ_BLOCK)[:, None]
    u = jnp.arange(SEQ_BLOCK)[None, :]
    cmask = ((u >= t) if reverse else (u <= t)).astype(BF16)
    sizes = []
    m = SEQ_BLOCK
    while m >= 1:
        sizes.append(m)
        m //= 2
    lmask = jnp.stack([(t // sz) == (u // sz) for sz in sizes]).astype(F32)
    return cmask, lmask


def _gla_call(st, w2s, b_gk, sinit, lay, reverse, other=None):
    bsz = st["qc"].shape[0]
    di = 1 if reverse else 0
    nb = GLA_BATCH
    tile = lambda c: lay.spec(c, lambda b, g: (b,), reverse, batch=nb)
    full = lambda shp: pl.BlockSpec(shp, lambda b, g: (0,) * len(shp))
    state_spec = pl.BlockSpec((nb, C_HEADS, CK_PAD, HEAD_PAD), lambda b, g: (b, 0, 0, 0))
    qc, kc, vc, gk = (lay.view(st[n]) for n in ("qc", "kc", "vc", "gt"))
    cmask, lmask = _scan_masks(reverse)
    o, sfin = pl.pallas_call(
        functools.partial(_gla_kernel, lay, reverse),
        grid=(bsz // nb, lay.ngrp),
        in_specs=[tile(CK_W), tile(CK_W), tile(CV_W), tile(LANES),
                  pl.BlockSpec((1, LANES, CK_W), lambda b, g: (di, 0, 0)),
                  pl.BlockSpec((1, 1, CK_W), lambda b, g: (di, 0, 0)),
                  full(cmask.shape), full(lmask.shape), state_spec] + ([tile(CV_W)] if reverse else []),
        out_specs=[tile(CV_W), state_spec],
        out_shape=[jax.ShapeDtypeStruct(vc.shape, F32),
                   jax.ShapeDtypeStruct(sinit.shape, F32)],
        scratch_shapes=[pltpu.VMEM((nb, C_HEADS, CK_PAD, HEAD_PAD), F32),
                        pltpu.VMEM((nb, CK_W, CV_W), BF16)],
        compiler_params=_cparams(2),
        name="gla_scan_bwd" if reverse else "gla_scan_fwd",
    )(qc, kc, vc, gk, w2s, b_gk, cmask, lmask, sinit, *([lay.view(other)] if reverse else []))
    return lay.unview(o), sfin


def _mix_ffn_kernel(final_norm, x_ref, mod_ref, ha_ref, hc_ref, oa_ref, gc_ref, s_ref, na_ref, nc_ref, wmix_ref,
                    g2_ref, wg_ref, wu_ref, wd_ref, fg_ref, o_ref):
    def head_norm(xh, gain):
        return xh * _rms_scale(xh, A_HEAD_DIM) * gain

    acc = None
    for j in range(B_WIDTH // LANES):
        r0 = A_PAD + j * LANES
        part = jnp.dot(s_ref[0, j].astype(BF16), wmix_ref[r0:r0 + LANES, :], preferred_element_type=F32)
        acc = part if acc is None else acc + part
    for h in range(A_HEADS):
        sl = slice(h * HEAD_PAD, (h + 1) * HEAD_PAD)
        a = head_norm(ha_ref[0, h], na_ref[:, sl])
        c = head_norm(hc_ref[0, h], nc_ref[:, sl])
        ah = (a * jax.nn.sigmoid(oa_ref[0, h].astype(F32))).astype(BF16)
        ch = (c * _silu(gc_ref[0, h].astype(F32))).astype(BF16)
        acc = acc + jnp.dot(ah, wmix_ref[h * HEAD_PAD:(h + 1) * HEAD_PAD, :], preferred_element_type=F32)
        c0 = A_PAD + B_WIDTH + h * HEAD_PAD
        acc = acc + jnp.dot(ch, wmix_ref[c0:c0 + HEAD_PAD, :], preferred_element_type=F32)
    x1 = x_ref[0] + mod_ref[0, 2:3, :] * acc

    y = x1 * _rms_scale(x1, x1.shape[-1]) * g2_ref[...]
    hmod = (y * (1.0 + mod_ref[0, 4:5, :]) + mod_ref[0, 3:4, :]).astype(BF16)
    gate = jnp.dot(hmod, wg_ref[...], preferred_element_type=F32)
    up = jnp.dot(hmod, wu_ref[...], preferred_element_type=F32)
    act = (_silu(gate) * up).astype(BF16)
    x2 = x1 + mod_ref[0, 5:6, :] * jnp.dot(act, wd_ref[...], preferred_element_type=F32)
    if final_norm:
        x2 = x2 * _rms_scale(x2, x2.shape[-1]) * fg_ref[...]
    o_ref[0] = x2


def _mix_ffn_call(x, mod, ha, hc, oa, gc, s, na, nc, w_mix, g2, w_ffn_in, w_ffn_out, final_g, final_norm, tm):
    bsz, t, d = x.shape
    d_ff = w_ffn_out.shape[0]
    mod_map = (lambda b, i: (b, 0, 0)) if mod.shape[0] > 1 else (lambda b, i: (0, 0, 0))
    row = lambda c: pl.BlockSpec((1, tm, c), lambda b, i: (b, i, 0))
    chunked = lambda c: pl.BlockSpec((1, c // LANES, tm, LANES), lambda b, i: (b, 0, i, 0))
    full = lambda shp: pl.BlockSpec(shp, lambda b, i: (0,) * len(shp))
    half = lambda j: pl.BlockSpec((d, d_ff), lambda b, i: (0, j), pipeline_mode=pl.Buffered(1))
    return pl.pallas_call(
        functools.partial(_mix_ffn_kernel, final_norm),
        grid=(bsz, t // tm),
        in_specs=[row(d), pl.BlockSpec((1, 6, d), mod_map), chunked(A_PAD), chunked(CV_W),
                  chunked(A_PAD), chunked(CV_W), chunked(B_WIDTH),
                  full((1, A_PAD)), full((1, CV_W)), _resident(w_mix.shape, 2),
                  full((1, d)), half(0), half(1), _resident(w_ffn_out.shape, 2), full((1, d))],
        out_specs=row(d),
        out_shape=jax.ShapeDtypeStruct(x.shape, F32),
        compiler_params=_cparams(2),
        name="mix_ffn",
    )(x, mod, ha, hc, oa, gc, s, na, nc, w_mix, g2, w_ffn_in, w_ffn_in, w_ffn_out, final_g)


def _pad_heads(w, n_heads, dim, pad, axis=-1):
    w = jnp.moveaxis(w, axis, -1)
    lead = w.shape[:-1]
    w = w.reshape(lead + (n_heads, dim))
    w = jnp.pad(w, [(0, 0)] * len(lead) + [(0, 0), (0, pad - dim)])
    return jnp.moveaxis(w.reshape(lead + (n_heads * pad,)), -1, axis)


def _pad_to(w, width):
    return jnp.pad(w, [(0, 0)] * (w.ndim - 1) + [(0, width - w.shape[-1])])


def _prep_w_in(w_in):
    a_w = A_HEADS * A_HEAD_DIM
    ck = C_HEADS * C_HEAD_DK
    cv = C_HEADS * C_HEAD_DV
    sizes = (a_w, a_w, a_w, a_w, N_DIR * 2 * A_HEADS, B_WIDTH, B_WIDTH, ck, ck, cv, cv, N_DIR * C_GATE_RANK)
    pts = [sum(sizes[:i + 1]) for i in range(len(sizes) - 1)]
    qa, ka, va, oa, ga, ub, vb, qc, kc, vc, gc, gkc = jnp.split(w_in, pts, axis=-1)
    pa = lambda w: _pad_heads(w, A_HEADS, A_HEAD_DIM, HEAD_PAD)
    pk = lambda w: _pad_heads(w, C_HEADS, C_HEAD_DK, CK_PAD)
    pv = lambda w: _pad_heads(w, C_HEADS, C_HEAD_DV, HEAD_PAD)
    gates = _pad_to(jnp.concatenate([ga, jnp.tile(gkc[:, :C_GATE_RANK], (1, GK_COPIES)),
                                     jnp.tile(gkc[:, C_GATE_RANK:], (1, GK_COPIES))], axis=1), LANES)
    cols = [pa(qa), pa(ka), pa(va), pa(oa), gates, ub, vb, pk(qc), pk(kc), pv(vc), pv(gc)]
    return jnp.concatenate(cols, axis=-1).astype(BF16)


def _prep_w_gk2(w_gk2):
    w = _pad_heads(w_gk2, C_HEADS, C_HEAD_DK, CK_PAD)
    hi = w.astype(BF16)
    lo = (w - hi.astype(F32)).astype(BF16)
    stacked = jnp.concatenate([hi, lo, hi], axis=1)
    return jnp.stack([jnp.pad(stacked[d], ((GK_OFF(d), LANES - GK_OFF(d) - GK_COPIES * C_GATE_RANK), (0, 0)))
                      for d in range(N_DIR)])


def _prep_w_out(w_out):
    a_w = A_HEADS * A_HEAD_DIM
    wa, wb, wc = w_out[:a_w], w_out[a_w:a_w + B_WIDTH], w_out[a_w + B_WIDTH:]
    wa = _pad_heads(wa, A_HEADS, A_HEAD_DIM, HEAD_PAD, axis=0)
    wc = _pad_heads(wc, C_HEADS, C_HEAD_DV, HEAD_PAD, axis=0)
    return jnp.concatenate([wa, wb, wc], axis=0).astype(BF16)


def _mixer_scans(st, lw, a_state, c_state, lay):
    qconv, kconv, s = _seq_prep_call(st, lw["conv"], lw["sgu_g"], lw["sgu_w"], lw["sgu_bias"], lay)
    ha, hc, a_fin, c_fin = None, None, [], []
    for di, reverse in enumerate((False, True)):
        ha, cfin, mfin = _mlstm_call(qconv, kconv, st, lw["gate_b"], a_state[di][0], a_state[di][1], lay, reverse, ha)
        a_fin.append((cfin, mfin))
        hc, sfin = _gla_call(st, lw["w_gk2"], lw["b_gk"], c_state[di], lay, reverse, hc)
        c_fin.append(sfin)
    return ha, hc, s, a_fin, c_fin


def kernel(x, c, ctx, c_ctx, norm1_g, norm2_g, w_ada, b_ada, w_in, mlstm_conv, mlstm_gate_b,
           mlstm_norm_g, gla_w_gk2, gla_b_gk, gla_norm_g, sgu_norm_g, sgu_w, sgu_b, w_out,
           w_ffn_in, w_ffn_out, final_g):
    bsz, seq, d = x.shape
    ctx_len = ctx.shape[1]
    depth = w_in.shape[0]
    n_cond = 2 * SUBLANES
    cond = jnp.zeros((n_cond, d), F32).at[:bsz].set(c).at[bsz].set(c_ctx)
    mods = _ada_call(cond, w_ada, b_ada).reshape(depth, n_cond, 6, d)

    fg = final_g.reshape(1, d)
    x_lat, x_ctx = x, ctx
    for l in range(depth):
        need_ctx = l < depth - 1
        mod_lat = mods[l, :bsz]
        mod_ctx = mods[l, bsz:bsz + 1]
        gb = mlstm_gate_b[l].reshape(N_DIR, 1, 2 * A_HEADS)
        lw = {
            "conv": jnp.stack([_pad_heads(mlstm_conv[l][:, :A_HEADS * A_HEAD_DIM], A_HEADS, A_HEAD_DIM, HEAD_PAD),
                               _pad_heads(mlstm_conv[l][:, A_HEADS * A_HEAD_DIM:], A_HEADS, A_HEAD_DIM, HEAD_PAD)]),
            "gate_b": jnp.stack([jnp.pad(gb[d], ((0, 0), (GA_OFF(d), LANES - GA_OFF(d) - 2 * A_HEADS)))
                                 for d in range(N_DIR)]),
            "w_gk2": _prep_w_gk2(gla_w_gk2[l]),
            "b_gk": _pad_heads(gla_b_gk[l], C_HEADS, C_HEAD_DK, CK_PAD).reshape(N_DIR, 1, CK_W),
            "sgu_g": sgu_norm_g[l].reshape(1, B_WIDTH),
            "sgu_w": sgu_w[l].astype(BF16),
            "sgu_bias": jnp.repeat(sgu_b[l].T, B_GROUP_DIM, axis=1),
        }
        w_in_l = _prep_w_in(w_in[l])
        w_out_l = _prep_w_out(w_out[l])
        na = _pad_heads(mlstm_norm_g[l], A_HEADS, A_HEAD_DIM, HEAD_PAD).reshape(1, A_PAD)
        nc = _pad_heads(jnp.tile(gla_norm_g[l], C_HEADS), C_HEADS, C_HEAD_DV, HEAD_PAD).reshape(1, CV_W)
        g1 = norm1_g[l].reshape(1, d)
        g2 = norm2_g[l].reshape(1, d)
        w_ffn_in_l = w_ffn_in[l].astype(BF16)
        w_ffn_out_l = w_ffn_out[l].astype(BF16)

        st_ctx = _proj_in_call(x_ctx, mod_ctx, g1, w_in_l, tm=_row_tile(ctx_len, PROJ_ROWS))
        st_lat = _proj_in_call(x_lat, mod_lat, g1, w_in_l, tm=_row_tile(seq, PROJ_ROWS))

        a0 = (jnp.zeros((bsz, A_HEADS, HEAD_PAD, 2 * HEAD_PAD), F32), jnp.zeros((bsz, SUBLANES, LANES), F32))
        c0 = jnp.zeros((bsz, C_HEADS, CK_PAD, HEAD_PAD), F32)
        lay_ctx = _SeqLayout(ctx_len, False)
        lay_lat = _SeqLayout(seq, l % 2 == 1)
        ha_c, hc_c, s_c, a_state, c_state = _mixer_scans(st_ctx, lw, (a0, a0), (c0, c0), lay_ctx)
        ha_l, hc_l, s_l, _, _ = _mixer_scans(st_lat, lw, a_state, c_state, lay_lat)

        x_lat = _mix_ffn_call(x_lat, mod_lat, ha_l, hc_l, st_lat["oa"], st_lat["gc"], s_l, na, nc, w_out_l,
                              g2, w_ffn_in_l, w_ffn_out_l, fg, not need_ctx, tm=_row_tile(seq, TAIL_ROWS))
        if need_ctx:
            x_ctx = _mix_ffn_call(x_ctx, mod_ctx, ha_c, hc_c, st_ctx["oa"], st_ctx["gc"], s_c, na, nc, w_out_l,
                                  g2, w_ffn_in_l, w_ffn_out_l, fg, False, tm=_row_tile(ctx_len, TAIL_ROWS))
    return x_lat
```

```python
import functools
import math

import jax
import jax.numpy as jnp
from jax import lax
from jax.experimental import pallas as pl
from jax.experimental.pallas import tpu as pltpu

F32 = jnp.float32
BF16 = jnp.bfloat16
HIGHEST = lax.Precision.HIGHEST

LANES = 128
SUBLANES = 8
VMEM_LIMIT_BYTES = 56 * 1024 * 1024

GRID_W = 64
RMS_EPS = 1e-6
A_HEADS = 4
A_HEAD_DIM = 96
B_GROUPS = 4
B_GROUP_DIM = 64
B_WIDTH = B_GROUPS * B_GROUP_DIM
C_HEADS = 4
C_HEAD_DK = 48
C_HEAD_DV = 96
C_GATE_RANK = 16
C_GATE_NORM = 16.0
N_DIR = 2

HEAD_PAD = LANES
A_PAD = A_HEADS * HEAD_PAD
CK_PAD = 64
CK_W = C_HEADS * CK_PAD
CV_W = C_HEADS * HEAD_PAD

SEQ_BLOCK = 128
SEQ_GROUP = SUBLANES
MLSTM_BATCH = 2
GLA_BATCH = 2
GK_COPIES = 3
GA_OFF = lambda d: 2 * A_HEADS * d
GK_OFF = lambda d: N_DIR * 2 * A_HEADS + GK_COPIES * C_GATE_RANK * d

PROJ_ROWS = 512
TAIL_ROWS = 512
ADA_COLS = 1536

_IN_LAYOUT = (("qa", A_PAD), ("ka", A_PAD), ("va", A_PAD), ("oa", A_PAD), ("gt", LANES),
              ("ub", B_WIDTH), ("vb", B_WIDTH), ("qc", CK_W), ("kc", CK_W), ("vc", CV_W), ("gc", CV_W))
N_IN_PAD = sum(w for _, w in _IN_LAYOUT)
_BF16_STREAMS = ("oa", "gc")


def _cparams(n_axes):
    return pltpu.CompilerParams(dimension_semantics=("arbitrary",) * n_axes,
                                vmem_limit_bytes=VMEM_LIMIT_BYTES)


def _resident(shape, n_grid):
    zeros = (0,) * len(shape)
    maps = {2: lambda a, b: zeros, 3: lambda a, b, c: zeros}
    return pl.BlockSpec(shape, maps[n_grid], pipeline_mode=pl.Buffered(1))


def _row_tile(t_len, want):
    return min(want, t_len)


def _log_sigmoid(x):
    return jnp.minimum(x, 0.0) - jnp.log(1.0 + jnp.exp(-jnp.abs(x)))


def _silu(x):
    return x * jax.nn.sigmoid(x)


def _gelu_tanh(x):
    c = math.sqrt(2.0 / math.pi)
    return 0.5 * x * (1.0 + jnp.tanh(c * (x + 0.044715 * (x * x * x))))


def _rms_scale(x, n):
    return lax.rsqrt(jnp.sum(x * x, axis=-1, keepdims=True) * (1.0 / n) + RMS_EPS)


def _split_hi_lo(x):
    hi = x.astype(BF16)
    lo = (x - hi.astype(F32)).astype(BF16)
    return hi, lo


def _ada_kernel(s_ref, w_ref, b_ref, o_ref):
    s = _silu(s_ref[...])
    o_ref[0] = jnp.dot(s, w_ref[0], precision=HIGHEST, preferred_element_type=F32) + b_ref[0]


def _ada_call(cond, w_ada, b_ada):
    depth, d, n6 = w_ada.shape
    nb = cond.shape[0]
    tn = ADA_COLS
    return pl.pallas_call(
        _ada_kernel,
        grid=(depth, n6 // tn),
        in_specs=[pl.BlockSpec((nb, d), lambda l, j: (0, 0)),
                  pl.BlockSpec((1, d, tn), lambda l, j: (l, 0, j)),
                  pl.BlockSpec((1, 1, tn), lambda l, j: (l, 0, j))],
        out_specs=pl.BlockSpec((1, nb, tn), lambda l, j: (l, 0, j)),
        out_shape=jax.ShapeDtypeStruct((depth, nb, n6), F32),
        compiler_params=_cparams(2),
        name="ada_mod",
    )(cond, w_ada, b_ada.reshape(depth, 1, n6))


def _proj_in_kernel(x_ref, mod_ref, g_ref, w_ref, *out_refs):
    x = x_ref[0]
    y = x * _rms_scale(x, x.shape[-1]) * g_ref[...]
    h = (y * (1.0 + mod_ref[0, 1:2, :]) + mod_ref[0, 0:1, :]).astype(BF16)
    off = 0
    for (_, width), o_ref in zip(_IN_LAYOUT, out_refs):
        res = jnp.dot(h, w_ref[:, off:off + width], preferred_element_type=F32)
        for j in range(width // LANES):
            o_ref[0, j] = res[:, j * LANES:(j + 1) * LANES].astype(o_ref.dtype)
        off += width


def _proj_in_call(x, mod, g, w, tm):
    bsz, t, d = x.shape
    mod_b = mod.shape[0]
    mod_map = (lambda b, i: (b, 0, 0)) if mod_b > 1 else (lambda b, i: (0, 0, 0))
    out_shapes, out_specs = [], []
    for name, wd in _IN_LAYOUT:
        nch = wd // LANES
        out_shapes.append(jax.ShapeDtypeStruct((bsz, nch, t, LANES), BF16 if name in _BF16_STREAMS else F32))
        out_specs.append(pl.BlockSpec((1, nch, tm, LANES), lambda b, i: (b, 0, i, 0)))
    outs = pl.pallas_call(
        _proj_in_kernel,
        grid=(bsz, t // tm),
        in_specs=[pl.BlockSpec((1, tm, d), lambda b, i: (b, i, 0)),
                  pl.BlockSpec((1, 6, d), mod_map),
                  pl.BlockSpec((1, d), lambda b, i: (0, 0)),
                  _resident((d, N_IN_PAD), 2)],
        out_specs=out_specs,
        out_shape=out_shapes,
        compiler_params=_cparams(2),
        name="proj_in",
    )(x, mod, g, w)
    return dict(zip([n for n, _ in _IN_LAYOUT], outs))


class _SeqLayout:
    def __init__(self, t_len, column_major):
        self.cm = column_major
        self.t_len = t_len
        self.nblk = t_len // SEQ_BLOCK
        self.group = min(SEQ_GROUP, self.nblk)
        self.ngrp = self.nblk // self.group
        if column_major:
            assert t_len == SEQ_BLOCK * GRID_W and self.group == SEQ_GROUP

    def view(self, a):
        if not self.cm:
            return a
        return a.reshape(a.shape[:-2] + (SEQ_BLOCK, GRID_W, LANES))

    def unview(self, a):
        if not self.cm:
            return a
        return a.reshape(a.shape[:-3] + (self.t_len, LANES))

    def spec(self, c, lead_map, reverse=False, batch=1):
        def grp(g):
            return self.ngrp - 1 - g if reverse else g

        n_lead = len(lead_map(0, 0))
        lead_blk = (1,) * (n_lead - 1) + (batch,)
        nch = c // LANES
        if self.cm:
            return pl.BlockSpec(lead_blk + (nch, SEQ_BLOCK, self.group, LANES),
                                lambda b, g: lead_map(b, g) + (0, 0, grp(g), 0))
        return pl.BlockSpec(lead_blk + (nch, self.group * SEQ_BLOCK, LANES),
                            lambda b, g: lead_map(b, g) + (0, grp(g), 0))

    def halo_specs(self, c):
        nch = c // LANES
        if self.cm:
            shp = (1, nch, SUBLANES, self.group, LANES)
            prev = pl.BlockSpec(shp, lambda b, g: (b, 0, SEQ_BLOCK // SUBLANES - 1, jnp.maximum(g - 1, 0), 0))
            nxt = pl.BlockSpec(shp, lambda b, g: (b, 0, 0, jnp.minimum(g + 1, self.ngrp - 1), 0))
        else:
            per = self.group * SEQ_BLOCK // SUBLANES
            last = self.t_len // SUBLANES - 1
            shp = (1, nch, SUBLANES, LANES)
            prev = pl.BlockSpec(shp, lambda b, g: (b, 0, jnp.maximum(g * per - 1, 0), 0))
            nxt = pl.BlockSpec(shp, lambda b, g: (b, 0, jnp.minimum((g + 1) * per, last), 0))
        return prev, nxt

    @staticmethod
    def _cat(pieces):
        return pieces[0] if len(pieces) == 1 else jnp.concatenate(pieces, axis=1)

    def halo_prev_row(self, ref):
        r = SUBLANES - 1
        if self.cm:
            return self._cat([ref[0, j, r:r + 1, self.group - 1, :] for j in range(ref.shape[1])])
        return self._cat([ref[0, j, r:r + 1, :] for j in range(ref.shape[1])])

    def halo_next_row(self, ref):
        if self.cm:
            return self._cat([ref[0, j, 0:1, 0, :] for j in range(ref.shape[1])])
        return self._cat([ref[0, j, 0:1, :] for j in range(ref.shape[1])])

    def _piece(self, ref, lead, j, i):
        if self.cm:
            flat = ref.at[lead + (j,)].reshape(self.group * SEQ_BLOCK, LANES)
            return flat, (pl.ds(i, SEQ_BLOCK, stride=self.group), slice(None))
        start = i * SEQ_BLOCK
        if not isinstance(i, int):
            start = pl.multiple_of(start, SEQ_BLOCK)
        return ref, lead + (j, pl.ds(start, SEQ_BLOCK), slice(None))

    def load(self, ref, i, lead=(0,)):
        pieces = []
        for j in range(ref.shape[len(lead)]):
            r, idx = self._piece(ref, lead, j, i)
            pieces.append(r[idx])
        return self._cat(pieces)

    def store(self, ref, i, val, lead=(0,)):
        for j in range(ref.shape[len(lead)]):
            r, idx = self._piece(ref, lead, j, i)
            r[idx] = val[:, j * LANES:(j + 1) * LANES]

    def row(self, ref, i, r):
        if self.cm:
            return self._cat([ref[0, j, r:r + 1, i, :] for j in range(ref.shape[1])])
        t = i * SEQ_BLOCK + r
        return self._cat([ref[0, j, t:t + 1, :] for j in range(ref.shape[1])])


def _seq_prep_kernel(lay, q_ref, qp_ref, qn_ref, k_ref, kp_ref, kn_ref, u_ref, v_ref, cw_ref, ng_ref, w_ref,
                     bias_ref, qo_ref, ko_ref, s_ref):
    g = pl.program_id(1)
    L = SEQ_BLOCK
    row = lax.broadcasted_iota(jnp.int32, (L, 1), 0)
    has_prev = (g > 0).astype(F32)
    has_next = (g < lay.ngrp - 1).astype(F32)
    grp = lax.broadcasted_iota(jnp.int32, (1, B_WIDTH), 1) // B_GROUP_DIM

    for i in range(lay.group):
        for x_ref, xp_ref, xn_ref, o_ref, w, scale in ((q_ref, qp_ref, qn_ref, qo_ref, cw_ref[0], A_HEAD_DIM ** -0.5),
                                                       (k_ref, kp_ref, kn_ref, ko_ref, cw_ref[1], None)):
            x = lay.load(x_ref, i)
            prev_row = lay.row(x_ref, i - 1, L - 1) if i > 0 else lay.halo_prev_row(xp_ref) * has_prev
            next_row = lay.row(x_ref, i + 1, 0) if i < lay.group - 1 else lay.halo_next_row(xn_ref) * has_next
            x_dn = jnp.where(row == 0, prev_row, pltpu.roll(x, 1, 0))
            x_up = jnp.where(row == L - 1, next_row, pltpu.roll(x, L - 1, 0))
            y = _silu(w[0:1, :] * x_dn + w[1:2, :] * x + w[2:3, :] * x_up)
            lay.store(o_ref, i, y if scale is None else y * scale)

        u = _gelu_tanh(lay.load(u_ref, i))
        v = _gelu_tanh(lay.load(v_ref, i))
        v = v * _rms_scale(v, B_WIDTH) * ng_ref[...]
        mixed = bias_ref[...]
        for gi in range(B_GROUPS):
            vg = jnp.where(grp == gi, v, 0.0).astype(BF16)
            mixed = mixed + jnp.dot(w_ref[gi], vg, preferred_element_type=F32)
        lay.store(s_ref, i, u * mixed)


def _seq_prep_call(st, conv_w, norm_g, w_s, bias, lay):
    bsz = st["qa"].shape[0]
    lead = lambda b, g: (b,)
    tile = lambda c: lay.spec(c, lead)
    hp, hn = lay.halo_specs(A_PAD)
    full = lambda shp: pl.BlockSpec(shp, lambda b, g: (0,) * len(shp))
    qa, ka, ub, vb = (lay.view(st[n]) for n in ("qa", "ka", "ub", "vb"))
    qo, ko, s = pl.pallas_call(
        functools.partial(_seq_prep_kernel, lay),
        grid=(bsz, lay.ngrp),
        in_specs=[tile(A_PAD), hp, hn, tile(A_PAD), hp, hn, tile(B_WIDTH), tile(B_WIDTH),
                  full((2, 3, A_PAD)), full((1, B_WIDTH)), full((B_GROUPS, SEQ_BLOCK, SEQ_BLOCK)),
                  full((SEQ_BLOCK, B_WIDTH))],
        out_specs=[tile(A_PAD), tile(A_PAD), tile(B_WIDTH)],
        out_shape=[jax.ShapeDtypeStruct(qa.shape, F32), jax.ShapeDtypeStruct(ka.shape, F32),
                   jax.ShapeDtypeStruct(ub.shape, F32)],
        compiler_params=_cparams(2),
        name="seq_prep",
    )(qa, qa, qa, ka, ka, ka, ub, vb, conv_w, norm_g, w_s, bias)
    return lay.unview(qo), lay.unview(ko), lay.unview(s)


def _mlstm_kernel(lay, reverse, q_ref, k_ref, v_ref, g_ref, gb_ref, cmask_ref, spread_ref, rowsel_ref,
                  cinit_ref, minit_ref, *rest):
    other_ref = rest[0] if reverse else None
    h_ref, cfin_ref, mfin_ref, c_sc, m_sc = rest[1:] if reverse else rest
    gidx = pl.program_id(1)
    L = SEQ_BLOCK

    @pl.when(gidx == 0)
    def _():
        c_sc[...] = cinit_ref[...]
        m_sc[...] = minit_ref[...]

    glane = lax.broadcasted_iota(jnp.int32, (1, LANES), 1)
    off = GA_OFF(1 if reverse else 0)
    is_i = (glane >= off) & (glane < off + A_HEADS)
    ri = lax.broadcasted_iota(jnp.int32, (L, L), 0)
    ci = lax.broadcasted_iota(jnp.int32, (L, L), 1)
    past = (ci >= ri) if reverse else (ci <= ri)
    last = 0 if reverse else L - 1
    ones = jnp.ones((L, HEAD_PAD), F32)
    twice = lambda x: jnp.concatenate([x, x], axis=1)

    def chunk(ii, carry):
        i = lay.group - 1 - ii if reverse else ii
        rows = range(MLSTM_BATCH)
        qf = [lay.load(q_ref, i, (bb,)) for bb in rows]
        q = [qf[bb].astype(BF16) for bb in rows]
        k = [lay.load(k_ref, i, (bb,)).astype(BF16) for bb in rows]
        v = [lay.load(v_ref, i, (bb,)) for bb in rows]
        g = [lay.load(g_ref, i, (bb,)) + gb_ref[0] for bb in rows]
        lf = [_split_hi_lo(_log_sigmoid(g[bb])) for bb in rows]
        cs = [jnp.dot(cmask_ref[...], jnp.concatenate(lf[bb], axis=1), preferred_element_type=F32) for bb in rows]
        bc = [cs[bb][:, 0:LANES] + cs[bb][:, LANES:2 * LANES] for bb in rows]
        x = [jnp.concatenate(_split_hi_lo(jnp.where(is_i, g[bb], bc[bb])), axis=1) for bb in rows]
        spread_all = jnp.dot(jnp.concatenate(x, axis=0), spread_ref[...], preferred_element_type=F32)
        spread = [spread_all[bb * L:(bb + 1) * L] for bb in rows]
        c_rows = [lax.dot_general(rowsel_ref[...], x[bb], (((1,), (1,)), ((), ())), preferred_element_type=F32)
                  for bb in rows]

        units = [(bb, h) for bb in rows for h in range(A_HEADS)]
        each = lambda f: {u: f(*u) for u in units}
        sl = lambda h: slice(h * HEAD_PAD, (h + 1) * HEAD_PAD)
        b_rep = each(lambda bb, h: spread[bb][:, sl(h)])
        c_rep = each(lambda bb, h: spread[bb][:, sl(A_HEADS + h)])
        c_row = each(lambda bb, h: c_rows[bb][h:h + 1, :])
        m_prev = each(lambda bb, h: m_sc[bb, h:h + 1, :])
        c2 = each(lambda bb, h: c_sc[bb, h])
        v2 = each(lambda bb, h: jnp.concatenate([v[bb][:, sl(h)], ones], axis=1))
        u_rep = each(lambda bb, h: jnp.maximum(
            m_prev[bb, h], jnp.max(jnp.where(past, c_row[bb, h], -jnp.inf), axis=1, keepdims=True)))
        qk = each(lambda bb, h: lax.dot_general(q[bb][:, sl(h)], k[bb][:, sl(h)], (((1,), (1,)), ((), ())),
                                                preferred_element_type=F32))
        s = each(lambda bb, h: (qk[bb, h] * jnp.exp(jnp.where(past, c_row[bb, h] - u_rep[bb, h], -jnp.inf))
                                ).astype(BF16))
        inter = each(lambda bb, h: jnp.exp(m_prev[bb, h] - u_rep[bb, h]))
        num2 = each(lambda bb, h: jnp.dot(
            jnp.concatenate([s[bb, h], (inter[bb, h] * qf[bb][:, sl(h)]).astype(BF16)], axis=1),
            jnp.concatenate([v2[bb, h], c2[bb, h]], axis=0).astype(BF16), preferred_element_type=F32))
        inv = each(lambda bb, h: 1.0 / jnp.maximum(jnp.abs(num2[bb, h][:, HEAD_PAD:]),
                                                   jnp.exp(-(b_rep[bb, h] + u_rep[bb, h]))))
        for bb in rows:
            out = jnp.concatenate([num2[bb, h][:, :HEAD_PAD] * inv[bb, h] for h in range(A_HEADS)], axis=1)
            if other_ref is not None:
                out = out + lay.load(other_ref, i, (bb,))
            lay.store(h_ref, i, out, (bb,))

        b_last = each(lambda bb, h: b_rep[bb, h][last:last + 1, :])
        w_log = each(lambda bb, h: b_last[bb, h] + c_rep[bb, h])
        m_new = each(lambda bb, h: jnp.maximum(b_last[bb, h] + m_prev[bb, h],
                                               jnp.max(w_log[bb, h], axis=0, keepdims=True)))
        wv2 = each(lambda bb, h: (twice(jnp.exp(w_log[bb, h] - m_new[bb, h])) * v2[bb, h]).astype(BF16))
        for bb, h in units:
            upd = lax.dot_general(k[bb][:, sl(h)], wv2[bb, h], (((0,), (0,)), ((), ())), preferred_element_type=F32)
            decay = jnp.exp(b_last[bb, h] + m_prev[bb, h] - m_new[bb, h])
            c_sc[bb, h] = twice(decay) * c2[bb, h] + upd
            m_sc[bb, h:h + 1, :] = m_new[bb, h]
        return carry

    lax.fori_loop(0, lay.group, chunk, 0)

    @pl.when(gidx == lay.ngrp - 1)
    def _():
        cfin_ref[...] = c_sc[...]
        mfin_ref[...] = m_sc[...]


def _mlstm_gate_matrices(off):
    lane = jnp.arange(LANES) - off
    blocks = []
    for h in range(A_HEADS):
        blocks.append(jnp.broadcast_to((lane == A_HEADS + h).astype(F32)[:, None], (LANES, LANES)))
    for h in range(A_HEADS):
        col = (lane == h).astype(F32) - (lane == A_HEADS + h).astype(F32)
        blocks.append(jnp.broadcast_to(col[:, None], (LANES, LANES)))
    half = jnp.concatenate(blocks, axis=1)
    spread = jnp.concatenate([half, half], axis=0).astype(BF16)
    sel = jnp.stack([(lane == h).astype(F32) - (lane == A_HEADS + h).astype(F32) for h in range(A_HEADS)]
                    + [jnp.zeros((LANES,), F32)] * (2 * SUBLANES - A_HEADS))
    rowsel = jnp.concatenate([sel, sel], axis=1).astype(BF16)
    return spread, rowsel


def _mlstm_call(qconv, kconv, st, gate_b, cinit, minit, lay, reverse, other=None):
    bsz = qconv.shape[0]
    di = 1 if reverse else 0
    nb = MLSTM_BATCH
    tile = lambda c: lay.spec(c, lambda b, g: (b,), reverse, batch=nb)
    full = lambda shp: pl.BlockSpec(shp, lambda b, g: (0,) * len(shp))
    c_spec = pl.BlockSpec((nb, A_HEADS, HEAD_PAD, 2 * HEAD_PAD), lambda b, g: (b, 0, 0, 0))
    m_spec = pl.BlockSpec((nb, SUBLANES, LANES), lambda b, g: (b, 0, 0))
    qa, ka, va, ga = lay.view(qconv), lay.view(kconv), lay.view(st["va"]), lay.view(st["gt"])
    cmask, _ = _scan_masks(reverse)
    spread, rowsel = _mlstm_gate_matrices(GA_OFF(di))
    h, cfin, mfin = pl.pallas_call(
        functools.partial(_mlstm_kernel, lay, reverse),
        grid=(bsz // nb, lay.ngrp),
        in_specs=[tile(A_PAD), tile(A_PAD), tile(A_PAD), tile(LANES),
                  pl.BlockSpec((1, 1, LANES), lambda b, g: (di, 0, 0)),
                  full(cmask.shape), full(spread.shape), full(rowsel.shape),
                  c_spec, m_spec] + ([tile(A_PAD)] if reverse else []),
        out_specs=[tile(A_PAD), c_spec, m_spec],
        out_shape=[jax.ShapeDtypeStruct(qa.shape, F32),
                   jax.ShapeDtypeStruct(cinit.shape, F32),
                   jax.ShapeDtypeStruct(minit.shape, F32)],
        scratch_shapes=[pltpu.VMEM((nb, A_HEADS, HEAD_PAD, 2 * HEAD_PAD), F32),
                        pltpu.VMEM((nb, SUBLANES, LANES), F32)],
        compiler_params=_cparams(2),
        name="mlstm_scan_bwd" if reverse else "mlstm_scan_fwd",
    )(qa, ka, va, ga, gate_b, cmask, spread, rowsel, cinit, minit, *([lay.view(other)] if reverse else []))
    return lay.unview(h), cfin, mfin


def _gla_kernel(lay, reverse, q_ref, k_ref, v_ref, gk_ref, w2_ref, bgk_ref, cmask_ref, lmask_ref, sinit_ref,
                *rest):
    other_ref = rest[0] if reverse else None
    o_ref, sfin_ref, s_sc, sbd_sc = rest[1:] if reverse else rest
    gidx = pl.program_id(1)
    L = SEQ_BLOCK

    @pl.when(gidx == 0)
    def _():
        s_sc[...] = sinit_ref[...]
        sbd_sc[...] = jnp.zeros_like(sbd_sc)
        for bb in range(GLA_BATCH):
            for h in range(C_HEADS):
                sbd_sc[bb, h * CK_PAD:(h + 1) * CK_PAD, h * HEAD_PAD:(h + 1) * HEAD_PAD] = \
                    sinit_ref[bb, h].astype(BF16)

    glane = lax.broadcasted_iota(jnp.int32, (1, LANES), 1)
    gk_lo = GK_OFF(1 if reverse else 0) + 2 * C_GATE_RANK
    row = lax.broadcasted_iota(jnp.int32, (L, 1), 0)
    hlane = glane // CK_PAD
    last = 0 if reverse else L - 1
    ones = jnp.ones((L, HEAD_PAD), BF16)

    def rows_from(x, idx_of_row_block, rows_per_block):
        n = L // rows_per_block
        return jnp.concatenate([jnp.broadcast_to(x[idx_of_row_block(i):idx_of_row_block(i) + 1],
                                                 (rows_per_block, x.shape[1])) for i in range(n)], axis=0)

    def chunk(ii, carry):
        i = lay.group - 1 - ii if reverse else ii
        rows = range(GLA_BATCH)
        heads = range(C_HEADS)
        g = [lay.load(gk_ref, i, (bb,)) for bb in rows]
        g_hi = [g[bb].astype(BF16).astype(F32) for bb in rows]
        g_split = [jnp.where(glane < gk_lo, g_hi[bb], g[bb] - g_hi[bb]).astype(BF16) for bb in rows]
        gk_all = jnp.dot(jnp.concatenate(g_split, axis=0), w2_ref[0], preferred_element_type=F32)
        la = [_log_sigmoid(gk_all[bb * L:(bb + 1) * L] + bgk_ref[0]) * (1.0 / C_GATE_NORM) for bb in rows]
        la_split = [_split_hi_lo(la[bb]) for bb in rows]
        cs = [jnp.dot(cmask_ref[...], jnp.concatenate(la_split[bb], axis=1), preferred_element_type=F32)
              for bb in rows]
        bs = [cs[bb][:, 0:CK_W] + cs[bb][:, CK_W:2 * CK_W] for bb in rows]
        qs = [lay.load(q_ref, i, (bb,)) * (C_HEAD_DK ** -0.5) for bb in rows]
        kk = [lay.load(k_ref, i, (bb,)) for bb in rows]
        vb = [lay.load(v_ref, i, (bb,)).astype(BF16) for bb in rows]
        att = [[None] * C_HEADS for _ in rows]

        def add_level(bb, q_t, k_mul, level, k_rows=None):
            k_all = kk[bb] if k_mul is None else kk[bb] * k_mul
            if k_rows is not None:
                k_all = jnp.where(k_rows, k_all, 0.0)
            k_b = k_all.astype(BF16)
            for p in range(C_HEADS // 2):
                tile = slice(p * LANES, (p + 1) * LANES)
                q_pair = q_t[:, tile]
                q_stack = jnp.concatenate([jnp.where(hlane == 0, q_pair, 0.0), jnp.where(hlane == 1, q_pair, 0.0)],
                                          axis=0).astype(BF16)
                a2 = lax.dot_general(q_stack, k_b[:, tile], (((1,), (1,)), ((), ())), preferred_element_type=F32)
                for j in range(2):
                    h = 2 * p + j
                    a = a2[j * L:(j + 1) * L]
                    if level is not None:
                        a = a * lmask_ref[level]
                    att[bb][h] = a if att[bb][h] is None else att[bb][h] + a

        def level_factors(bb, m):
            bnd = m if reverse else m - 1
            if m >= SUBLANES:
                zero = jnp.zeros((m, CK_W), F32)
                q_parts, k_parts = [], []
                for base in range(0, L, 2 * m):
                    pref = bs[bb][base + bnd:base + bnd + 1]
                    lo, hi = slice(base, base + m), slice(base + m, base + 2 * m)
                    q_half, k_half = (lo, hi) if reverse else (hi, lo)
                    q_piece = qs[bb][q_half] * jnp.exp(bs[bb][q_half] - pref)
                    k_piece = jnp.exp(pref - bs[bb][k_half])
                    q_parts += [q_piece, zero] if reverse else [zero, q_piece]
                    k_parts += [zero, k_piece] if reverse else [k_piece, zero]
                return jnp.concatenate(q_parts, axis=0), jnp.concatenate(k_parts, axis=0)
            pos = row % (2 * m)
            is_q = (pos < m) if reverse else (pos >= m)
            if m == 1:
                return jnp.where(is_q, qs[bb] * jnp.exp(la[bb]), 0.0), None
            if 2 * m >= SUBLANES:
                pref = rows_from(bs[bb], lambda r: r * 2 * m + bnd, 2 * m)
            else:
                p0 = rows_from(bs[bb], lambda r: r * SUBLANES + bnd, SUBLANES)
                p1 = rows_from(bs[bb], lambda r: r * SUBLANES + 2 * m + bnd, SUBLANES)
                pref = jnp.where(row % SUBLANES < 2 * m, p0, p1)
            return (qs[bb] * jnp.exp(jnp.where(is_q, bs[bb] - pref, -jnp.inf)),
                    jnp.exp(jnp.where(is_q, -jnp.inf, pref - bs[bb])))

        level = 0
        m = L // 2
        while m >= 1:
            key_rows = (row % 2 == (1 if reverse else 0)) if m == 1 else None
            for bb in rows:
                q_t, k_mul = level_factors(bb, m)
                add_level(bb, q_t, k_mul, None if m == L // 2 else level, key_rows)
            level += 1
            m //= 2
        for bb in rows:
            add_level(bb, qs[bb], None, level)

        o_inter = [jnp.dot((qs[bb] * jnp.exp(bs[bb])).astype(BF16), sbd_sc[bb], preferred_element_type=F32)
                   for bb in rows]
        for bb in rows:
            o_heads = []
            for h in heads:
                cols = slice(h * HEAD_PAD, (h + 1) * HEAD_PAD)
                o_heads.append(o_inter[bb][:, cols] + jnp.dot(att[bb][h].astype(BF16), vb[bb][:, cols],
                                                              preferred_element_type=F32))
            out = jnp.concatenate(o_heads, axis=1)
            if other_ref is not None:
                out = out + lay.load(other_ref, i, (bb,))
            lay.store(o_ref, i, out, (bb,))

        ones2 = jnp.concatenate([ones, ones], axis=0)
        for bb in rows:
            btot = bs[bb][last:last + 1]
            ke_t = (kk[bb] * jnp.exp(btot - bs[bb])).T.astype(BF16)
            la_cat = jnp.concatenate(la_split[bb], axis=0)
            dec_col = jnp.exp(lax.dot_general(la_cat, ones2, (((0,), (0,)), ((), ())),
                                              preferred_element_type=F32))
            for h in heads:
                krows = slice(h * CK_PAD, (h + 1) * CK_PAD)
                cols = slice(h * HEAD_PAD, (h + 1) * HEAD_PAD)
                upd = jnp.dot(ke_t[krows], vb[bb][:, cols], preferred_element_type=F32)
                s_new = dec_col[krows] * s_sc[bb, h] + upd
                s_sc[bb, h] = s_new
                sbd_sc[bb, krows, cols] = s_new.astype(BF16)
        return carry

    lax.fori_loop(0, lay.group, chunk, 0)

    @pl.when(gidx == lay.ngrp - 1)
    def _():
        sfin_ref[...] = s_sc[...]


def _scan_masks(reverse):
    t = jnp.arange(SEQ_BLOCK)[:, None]
    u = jnp.arange(SEQ_BLOCK)[None, :]
    cmask = ((u >= t) if reverse else (u <= t)).astype(BF16)
    sizes = []
    m = SEQ_BLOCK
    while m >= 1:
        sizes.append(m)
        m //= 2
    lmask = jnp.stack([(t // sz) == (u // sz) for sz in sizes]).astype(F32)
    return cmask, lmask


def _gla_call(st, w2s, b_gk, sinit, lay, reverse, other=None):
    bsz = st["qc"].shape[0]
    di = 1 if reverse else 0
    nb = GLA_BATCH
    tile = lambda c: lay.spec(c, lambda b, g: (b,), reverse, batch=nb)
    full = lambda shp: pl.BlockSpec(shp, lambda b, g: (0,) * len(shp))
    state_spec = pl.BlockSpec((nb, C_HEADS, CK_PAD, HEAD_PAD), lambda b, g: (b, 0, 0, 0))
    qc, kc, vc, gk = (lay.view(st[n]) for n in ("qc", "kc", "vc", "gt"))
    cmask, lmask = _scan_masks(reverse)
    o, sfin = pl.pallas_call(
        functools.partial(_gla_kernel, lay, reverse),
        grid=(bsz // nb, lay.ngrp),
        in_specs=[tile(CK_W), tile(CK_W), tile(CV_W), tile(LANES),
                  pl.BlockSpec((1, LANES, CK_W), lambda b, g: (di, 0, 0)),
                  pl.BlockSpec((1, 1, CK_W), lambda b, g: (di, 0, 0)),
                  full(cmask.shape), full(lmask.shape), state_spec] + ([tile(CV_W)] if reverse else []),
        out_specs=[tile(CV_W), state_spec],
        out_shape=[jax.ShapeDtypeStruct(vc.shape, F32),
                   jax.ShapeDtypeStruct(sinit.shape, F32)],
        scratch_shapes=[pltpu.VMEM((nb, C_HEADS, CK_PAD, HEAD_PAD), F32),
                        pltpu.VMEM((nb, CK_W, CV_W), BF16)],
        compiler_params=_cparams(2),
        name="gla_scan_bwd" if reverse else "gla_scan_fwd",
    )(qc, kc, vc, gk, w2s, b_gk, cmask, lmask, sinit, *([lay.view(other)] if reverse else []))
    return lay.unview(o), sfin


def _mix_ffn_kernel(final_norm, x_ref, mod_ref, ha_ref, hc_ref, oa_ref, gc_ref, s_ref, na_ref, nc_ref, wmix_ref,
                    g2_ref, wg_ref, wu_ref, wd_ref, fg_ref, o_ref):
    def head_norm(xh, gain):
        return xh * _rms_scale(xh, A_HEAD_DIM) * gain

    acc = None
    for j in range(B_WIDTH // LANES):
        r0 = A_PAD + j * LANES
        part = jnp.dot(s_ref[0, j].astype(BF16), wmix_ref[r0:r0 + LANES, :], preferred_element_type=F32)
        acc = part if acc is None else acc + part
    for h in range(A_HEADS):
        sl = slice(h * HEAD_PAD, (h + 1) * HEAD_PAD)
        a = head_norm(ha_ref[0, h], na_ref[:, sl])
        c = head_norm(hc_ref[0, h], nc_ref[:, sl])
        ah = (a * jax.nn.sigmoid(oa_ref[0, h].astype(F32))).astype(BF16)
        ch = (c * _silu(gc_ref[0, h].astype(F32))).astype(BF16)
        acc = acc + jnp.dot(ah, wmix_ref[h * HEAD_PAD:(h + 1) * HEAD_PAD, :], preferred_element_type=F32)
        c0 = A_PAD + B_WIDTH + h * HEAD_PAD
        acc = acc + jnp.dot(ch, wmix_ref[c0:c0 + HEAD_PAD, :], preferred_element_type=F32)
    x1 = x_ref[0] + mod_ref[0, 2:3, :] * acc

    y = x1 * _rms_scale(x1, x1.shape[-1]) * g2_ref[...]
    hmod = (y * (1.0 + mod_ref[0, 4:5, :]) + mod_ref[0, 3:4, :]).astype(BF16)
    gate = jnp.dot(hmod, wg_ref[...], preferred_element_type=F32)
    up = jnp.dot(hmod, wu_ref[...], preferred_element_type=F32)
    act = (_silu(gate) * up).astype(BF16)
    x2 = x1 + mod_ref[0, 5:6, :] * jnp.dot(act, wd_ref[...], preferred_element_type=F32)
    if final_norm:
        x2 = x2 * _rms_scale(x2, x2.shape[-1]) * fg_ref[...]
    o_ref[0] = x2


def _mix_ffn_call(x, mod, ha, hc, oa, gc, s, na, nc, w_mix, g2, w_ffn_in, w_ffn_out, final_g, final_norm, tm):
    bsz, t, d = x.shape
    d_ff = w_ffn_out.shape[0]
    mod_map = (lambda b, i: (b, 0, 0)) if mod.shape[0] > 1 else (lambda b, i: (0, 0, 0))
    row = lambda c: pl.BlockSpec((1, tm, c), lambda b, i: (b, i, 0))
    chunked = lambda c: pl.BlockSpec((1, c // LANES, tm, LANES), lambda b, i: (b, 0, i, 0))
    full = lambda shp: pl.BlockSpec(shp, lambda b, i: (0,) * len(shp))
    half = lambda j: pl.BlockSpec((d, d_ff), lambda b, i: (0, j), pipeline_mode=pl.Buffered(1))
    return pl.pallas_call(
        functools.partial(_mix_ffn_kernel, final_norm),
        grid=(bsz, t // tm),
        in_specs=[row(d), pl.BlockSpec((1, 6, d), mod_map), chunked(A_PAD), chunked(CV_W),
                  chunked(A_PAD), chunked(CV_W), chunked(B_WIDTH),
                  full((1, A_PAD)), full((1, CV_W)), _resident(w_mix.shape, 2),
                  full((1, d)), half(0), half(1), _resident(w_ffn_out.shape, 2), full((1, d))],
        out_specs=row(d),
        out_shape=jax.ShapeDtypeStruct(x.shape, F32),
        compiler_params=_cparams(2),
        name="mix_ffn",
    )(x, mod, ha, hc, oa, gc, s, na, nc, w_mix, g2, w_ffn_in, w_ffn_in, w_ffn_out, final_g)


def _pad_heads(w, n_heads, dim, pad, axis=-1):
    w = jnp.moveaxis(w, axis, -1)
    lead = w.shape[:-1]
    w = w.reshape(lead + (n_heads, dim))
    w = jnp.pad(w, [(0, 0)] * len(lead) + [(0, 0), (0, pad - dim)])
    return jnp.moveaxis(w.reshape(lead + (n_heads * pad,)), -1, axis)


def _pad_to(w, width):
    return jnp.pad(w, [(0, 0)] * (w.ndim - 1) + [(0, width - w.shape[-1])])


def _prep_w_in(w_in):
    a_w = A_HEADS * A_HEAD_DIM
    ck = C_HEADS * C_HEAD_DK
    cv = C_HEADS * C_HEAD_DV
    sizes = (a_w, a_w, a_w, a_w, N_DIR * 2 * A_HEADS, B_WIDTH, B_WIDTH, ck, ck, cv, cv, N_DIR * C_GATE_RANK)
    pts = [sum(sizes[:i + 1]) for i in range(len(sizes) - 1)]
    qa, ka, va, oa, ga, ub, vb, qc, kc, vc, gc, gkc = jnp.split(w_in, pts, axis=-1)
    pa = lambda w: _pad_heads(w, A_HEADS, A_HEAD_DIM, HEAD_PAD)
    pk = lambda w: _pad_heads(w, C_HEADS, C_HEAD_DK, CK_PAD)
    pv = lambda w: _pad_heads(w, C_HEADS, C_HEAD_DV, HEAD_PAD)
    gates = _pad_to(jnp.concatenate([ga, jnp.tile(gkc[:, :C_GATE_RANK], (1, GK_COPIES)),
                                     jnp.tile(gkc[:, C_GATE_RANK:], (1, GK_COPIES))], axis=1), LANES)
    cols = [pa(qa), pa(ka), pa(va), pa(oa), gates, ub, vb, pk(qc), pk(kc), pv(vc), pv(gc)]
    return jnp.concatenate(cols, axis=-1).astype(BF16)


def _prep_w_gk2(w_gk2):
    w = _pad_heads(w_gk2, C_HEADS, C_HEAD_DK, CK_PAD)
    hi = w.astype(BF16)
    lo = (w - hi.astype(F32)).astype(BF16)
    stacked = jnp.concatenate([hi, lo, hi], axis=1)
    return jnp.stack([jnp.pad(stacked[d], ((GK_OFF(d), LANES - GK_OFF(d) - GK_COPIES * C_GATE_RANK), (0, 0)))
                      for d in range(N_DIR)])


def _prep_w_out(w_out):
    a_w = A_HEADS * A_HEAD_DIM
    wa, wb, wc = w_out[:a_w], w_out[a_w:a_w + B_WIDTH], w_out[a_w + B_WIDTH:]
    wa = _pad_heads(wa, A_HEADS, A_HEAD_DIM, HEAD_PAD, axis=0)
    wc = _pad_heads(wc, C_HEADS, C_HEAD_DV, HEAD_PAD, axis=0)
    return jnp.concatenate([wa, wb, wc], axis=0).astype(BF16)


def _mixer_scans(st, lw, a_state, c_state, lay):
    qconv, kconv, s = _seq_prep_call(st, lw["conv"], lw["sgu_g"], lw["sgu_w"], lw["sgu_bias"], lay)
    ha, hc, a_fin, c_fin = None, None, [], []
    for di, reverse in enumerate((False, True)):
        ha, cfin, mfin = _mlstm_call(qconv, kconv, st, lw["gate_b"], a_state[di][0], a_state[di][1], lay, reverse, ha)
        a_fin.append((cfin, mfin))
        hc, sfin = _gla_call(st, lw["w_gk2"], lw["b_gk"], c_state[di], lay, reverse, hc)
        c_fin.append(sfin)
    return ha, hc, s, a_fin, c_fin


def kernel(x, c, ctx, c_ctx, norm1_g, norm2_g, w_ada, b_ada, w_in, mlstm_conv, mlstm_gate_b,
           mlstm_norm_g, gla_w_gk2, gla_b_gk, gla_norm_g, sgu_norm_g, sgu_w, sgu_b, w_out,
           w_ffn_in, w_ffn_out, final_g):
    bsz, seq, d = x.shape
    ctx_len = ctx.shape[1]
    depth = w_in.shape[0]
    n_cond = 2 * SUBLANES
    cond = jnp.zeros((n_cond, d), F32).at[:bsz].set(c).at[bsz].set(c_ctx)
    mods = _ada_call(cond, w_ada, b_ada).reshape(depth, n_cond, 6, d)

    fg = final_g.reshape(1, d)
    x_lat, x_ctx = x, ctx
    for l in range(depth):
        need_ctx = l < depth - 1
        mod_lat = mods[l, :bsz]
        mod_ctx = mods[l, bsz:bsz + 1]
        gb = mlstm_gate_b[l].reshape(N_DIR, 1, 2 * A_HEADS)
        lw = {
            "conv": jnp.stack([_pad_heads(mlstm_conv[l][:, :A_HEADS * A_HEAD_DIM], A_HEADS, A_HEAD_DIM, HEAD_PAD),
                               _pad_heads(mlstm_conv[l][:, A_HEADS * A_HEAD_DIM:], A_HEADS, A_HEAD_DIM, HEAD_PAD)]),
            "gate_b": jnp.stack([jnp.pad(gb[d], ((0, 0), (GA_OFF(d), LANES - GA_OFF(d) - 2 * A_HEADS)))
                                 for d in range(N_DIR)]),
            "w_gk2": _prep_w_gk2(gla_w_gk2[l]),
            "b_gk": _pad_heads(gla_b_gk[l], C_HEADS, C_HEAD_DK, CK_PAD).reshape(N_DIR, 1, CK_W),
            "sgu_g": sgu_norm_g[l].reshape(1, B_WIDTH),
            "sgu_w": sgu_w[l].astype(BF16),
            "sgu_bias": jnp.repeat(sgu_b[l].T, B_GROUP_DIM, axis=1),
        }
        w_in_l = _prep_w_in(w_in[l])
        w_out_l = _prep_w_out(w_out[l])
        na = _pad_heads(mlstm_norm_g[l], A_HEADS, A_HEAD_DIM, HEAD_PAD).reshape(1, A_PAD)
        nc = _pad_heads(jnp.tile(gla_norm_g[l], C_HEADS), C_HEADS, C_HEAD_DV, HEAD_PAD).reshape(1, CV_W)
        g1 = norm1_g[l].reshape(1, d)
        g2 = norm2_g[l].reshape(1, d)
        w_ffn_in_l = w_ffn_in[l].astype(BF16)
        w_ffn_out_l = w_ffn_out[l].astype(BF16)

        st_ctx = _proj_in_call(x_ctx, mod_ctx, g1, w_in_l, tm=_row_tile(ctx_len, PROJ_ROWS))
        st_lat = _proj_in_call(x_lat, mod_lat, g1, w_in_l, tm=_row_tile(seq, PROJ_ROWS))

        a0 = (jnp.zeros((bsz, A_HEADS, HEAD_PAD, 2 * HEAD_PAD), F32), jnp.zeros((bsz, SUBLANES, LANES), F32))
        c0 = jnp.zeros((bsz, C_HEADS, CK_PAD, HEAD_PAD), F32)
        lay_ctx = _SeqLayout(ctx_len, False)
        lay_lat = _SeqLayout(seq, l % 2 == 1)
        ha_c, hc_c, s_c, a_state, c_state = _mixer_scans(st_ctx, lw, (a0, a0), (c0, c0), lay_ctx)
        ha_l, hc_l, s_l, _, _ = _mixer_scans(st_lat, lw, a_state, c_state, lay_lat)

        x_lat = _mix_ffn_call(x_lat, mod_lat, ha_l, hc_l, st_lat["oa"], st_lat["gc"], s_l, na, nc, w_out_l,
                              g2, w_ffn_in_l, w_ffn_out_l, fg, not need_ctx, tm=_row_tile(seq, TAIL_ROWS))
        if need_ctx:
            x_ctx = _mix_ffn_call(x_ctx, mod_ctx, ha_c, hc_c, st_ctx["oa"], st_ctx["gc"], s_c, na, nc, w_out_l,
                                  g2, w_ffn_in_l, w_ffn_out_l, fg, False, tm=_row_tile(ctx_len, TAIL_ROWS))
    return x_lat
```

```python
import functools
import math

import jax
import jax.numpy as jnp
from jax import lax
from jax.experimental import pallas as pl
from jax.experimental.pallas import tpu as pltpu

F32 = jnp.float32
BF16 = jnp.bfloat16
HIGHEST = lax.Precision.HIGHEST

LANES = 128
SUBLANES = 8
VMEM_LIMIT_BYTES = 56 * 1024 * 1024

GRID_W = 64
RMS_EPS = 1e-6
A_HEADS = 4
A_HEAD_DIM = 96
B_GROUPS = 4
B_GROUP_DIM = 64
B_WIDTH = B_GROUPS * B_GROUP_DIM
C_HEADS = 4
C_HEAD_DK = 48
C_HEAD_DV = 96
C_GATE_RANK = 16
C_GATE_NORM = 16.0
N_DIR = 2

HEAD_PAD = LANES
A_PAD = A_HEADS * HEAD_PAD
CK_PAD = 64
CK_W = C_HEADS * CK_PAD
CV_W = C_HEADS * HEAD_PAD

SEQ_BLOCK = 128
SEQ_GROUP = SUBLANES
MLSTM_BATCH = 2
GLA_BATCH = 2
GK_COPIES = 3
GA_OFF = lambda d: 2 * A_HEADS * d
GK_OFF = lambda d: N_DIR * 2 * A_HEADS + GK_COPIES * C_GATE_RANK * d

PROJ_ROWS = 512
TAIL_ROWS = 512
ADA_COLS = 1536

_IN_LAYOUT = (("qa", A_PAD), ("ka", A_PAD), ("va", A_PAD), ("oa", A_PAD), ("gt", LANES),
              ("ub", B_WIDTH), ("vb", B_WIDTH), ("qc", CK_W), ("kc", CK_W), ("vc", CV_W), ("gc", CV_W))
N_IN_PAD = sum(w for _, w in _IN_LAYOUT)
_BF16_STREAMS = ("oa", "gc")


def _cparams(n_axes):
    return pltpu.CompilerParams(dimension_semantics=("arbitrary",) * n_axes,
                                vmem_limit_bytes=VMEM_LIMIT_BYTES)


def _resident(shape, n_grid):
    zeros = (0,) * len(shape)
    maps = {2: lambda a, b: zeros, 3: lambda a, b, c: zeros}
    return pl.BlockSpec(shape, maps[n_grid], pipeline_mode=pl.Buffered(1))


def _row_tile(t_len, want):
    return min(want, t_len)


def _log_sigmoid(x):
    return jnp.minimum(x, 0.0) - jnp.log(1.0 + jnp.exp(-jnp.abs(x)))


def _silu(x):
    return x * jax.nn.sigmoid(x)


def _gelu_tanh(x):
    c = math.sqrt(2.0 / math.pi)
    return 0.5 * x * (1.0 + jnp.tanh(c * (x + 0.044715 * (x * x * x))))


def _rms_scale(x, n):
    return lax.rsqrt(jnp.sum(x * x, axis=-1, keepdims=True) * (1.0 / n) + RMS_EPS)


def _split_hi_lo(x):
    hi = x.astype(BF16)
    lo = (x - hi.astype(F32)).astype(BF16)
    return hi, lo


def _ada_kernel(s_ref, w_ref, b_ref, o_ref):
    s = _silu(s_ref[...])
    o_ref[0] = jnp.dot(s, w_ref[0], precision=HIGHEST, preferred_element_type=F32) + b_ref[0]


def _ada_call(cond, w_ada, b_ada):
    depth, d, n6 = w_ada.shape
    nb = cond.shape[0]
    tn = ADA_COLS
    return pl.pallas_call(
        _ada_kernel,
        grid=(depth, n6 // tn),
        in_specs=[pl.BlockSpec((nb, d), lambda l, j: (0, 0)),
                  pl.BlockSpec((1, d, tn), lambda l, j: (l, 0, j)),
                  pl.BlockSpec((1, 1, tn), lambda l, j: (l, 0, j))],
        out_specs=pl.BlockSpec((1, nb, tn), lambda l, j: (l, 0, j)),
        out_shape=jax.ShapeDtypeStruct((depth, nb, n6), F32),
        compiler_params=_cparams(2),
        name="ada_mod",
    )(cond, w_ada, b_ada.reshape(depth, 1, n6))


def _proj_in_kernel(x_ref, mod_ref, g_ref, w_ref, *out_refs):
    x = x_ref[0]
    y = x * _rms_scale(x, x.shape[-1]) * g_ref[...]
    h = (y * (1.0 + mod_ref[0, 1:2, :]) + mod_ref[0, 0:1, :]).astype(BF16)
    off = 0
    for (_, width), o_ref in zip(_IN_LAYOUT, out_refs):
        res = jnp.dot(h, w_ref[:, off:off + width], preferred_element_type=F32)
        for j in range(width // LANES):
            o_ref[0, j] = res[:, j * LANES:(j + 1) * LANES].astype(o_ref.dtype)
        off += width


def _proj_in_call(x, mod, g, w, tm):
    bsz, t, d = x.shape
    mod_b = mod.shape[0]
    mod_map = (lambda b, i: (b, 0, 0)) if mod_b > 1 else (lambda b, i: (0, 0, 0))
    out_shapes, out_specs = [], []
    for name, wd in _IN_LAYOUT:
        nch = wd // LANES
        out_shapes.append(jax.ShapeDtypeStruct((bsz, nch, t, LANES), BF16 if name in _BF16_STREAMS else F32))
        out_specs.append(pl.BlockSpec((1, nch, tm, LANES), lambda b, i: (b, 0, i, 0)))
    outs = pl.pallas_call(
        _proj_in_kernel,
        grid=(bsz, t // tm),
        in_specs=[pl.BlockSpec((1, tm, d), lambda b, i: (b, i, 0)),
                  pl.BlockSpec((1, 6, d), mod_map),
                  pl.BlockSpec((1, d), lambda b, i: (0, 0)),
                  _resident((d, N_IN_PAD), 2)],
        out_specs=out_specs,
        out_shape=out_shapes,
        compiler_params=_cparams(2),
        name="proj_in",
    )(x, mod, g, w)
    return dict(zip([n for n, _ in _IN_LAYOUT], outs))


class _SeqLayout:
    def __init__(self, t_len, column_major):
        self.cm = column_major
        self.t_len = t_len
        self.nblk = t_len // SEQ_BLOCK
        self.group = min(SEQ_GROUP, self.nblk)
        self.ngrp = self.nblk // self.group
        if column_major:
            assert t_len == SEQ_BLOCK * GRID_W and self.group == SEQ_GROUP

    def view(self, a):
        if not self.cm:
            return a
        return a.reshape(a.shape[:-2] + (SEQ_BLOCK, GRID_W, LANES))

    def unview(self, a):
        if not self.cm:
            return a
        return a.reshape(a.shape[:-3] + (self.t_len, LANES))

    def spec(self, c, lead_map, reverse=False, batch=1):
        def grp(g):
            return self.ngrp - 1 - g if reverse else g

        n_lead = len(lead_map(0, 0))
        lead_blk = (1,) * (n_lead - 1) + (batch,)
        nch = c // LANES
        if self.cm:
            return pl.BlockSpec(lead_blk + (nch, SEQ_BLOCK, self.group, LANES),
                                lambda b, g: lead_map(b, g) + (0, 0, grp(g), 0))
        return pl.BlockSpec(lead_blk + (nch, self.group * SEQ_BLOCK, LANES),
                            lambda b, g: lead_map(b, g) + (0, grp(g), 0))

    def halo_specs(self, c):
        nch = c // LANES
        if self.cm:
            shp = (1, nch, SUBLANES, self.group, LANES)
            prev = pl.BlockSpec(shp, lambda b, g: (b, 0, SEQ_BLOCK // SUBLANES - 1, jnp.maximum(g - 1, 0), 0))
            nxt = pl.BlockSpec(shp, lambda b, g: (b, 0, 0, jnp.minimum(g + 1, self.ngrp - 1), 0))
        else:
            per = self.group * SEQ_BLOCK // SUBLANES
            last = self.t_len // SUBLANES - 1
            shp = (1, nch, SUBLANES, LANES)
            prev = pl.BlockSpec(shp, lambda b, g: (b, 0, jnp.maximum(g * per - 1, 0), 0))
            nxt = pl.BlockSpec(shp, lambda b, g: (b, 0, jnp.minimum((g + 1) * per, last), 0))
        return prev, nxt

    @staticmethod
    def _cat(pieces):
        return pieces[0] if len(pieces) == 1 else jnp.concatenate(pieces, axis=1)

    def halo_prev_row(self, ref):
        r = SUBLANES - 1
        if self.cm:
            return self._cat([ref[0, j, r:r + 1, self.group - 1, :] for j in range(ref.shape[1])])
        return self._cat([ref[0, j, r:r + 1, :] for j in range(ref.shape[1])])

    def halo_next_row(self, ref):
        if self.cm:
            return self._cat([ref[0, j, 0:1, 0, :] for j in range(ref.shape[1])])
        return self._cat([ref[0, j, 0:1, :] for j in range(ref.shape[1])])

    def _piece(self, ref, lead, j, i):
        if self.cm:
            flat = ref.at[lead + (j,)].reshape(self.group * SEQ_BLOCK, LANES)
            return flat, (pl.ds(i, SEQ_BLOCK, stride=self.group), slice(None))
        start = i * SEQ_BLOCK
        if not isinstance(i, int):
            start = pl.multiple_of(start, SEQ_BLOCK)
        return ref, lead + (j, pl.ds(start, SEQ_BLOCK), slice(None))

    def load(self, ref, i, lead=(0,)):
        pieces = []
        for j in range(ref.shape[len(lead)]):
            r, idx = self._piece(ref, lead, j, i)
            pieces.append(r[idx])
        return self._cat(pieces)

    def store(self, ref, i, val, lead=(0,)):
        for j in range(ref.shape[len(lead)]):
            r, idx = self._piece(ref, lead, j, i)
            r[idx] = val[:, j * LANES:(j + 1) * LANES]

    def row(self, ref, i, r):
        if self.cm:
            return self._cat([ref[0, j, r:r + 1, i, :] for j in range(ref.shape[1])])
        t = i * SEQ_BLOCK + r
        return self._cat([ref[0, j, t:t + 1, :] for j in range(ref.shape[1])])


def _seq_prep_kernel(lay, q_ref, qp_ref, qn_ref, k_ref, kp_ref, kn_ref, u_ref, v_ref, cw_ref, ng_ref, w_ref,
                     bias_ref, qo_ref, ko_ref, s_ref):
    g = pl.program_id(1)
    L = SEQ_BLOCK
    row = lax.broadcasted_iota(jnp.int32, (L, 1), 0)
    has_prev = (g > 0).astype(F32)
    has_next = (g < lay.ngrp - 1).astype(F32)
    grp = lax.broadcasted_iota(jnp.int32, (1, B_WIDTH), 1) // B_GROUP_DIM

    for i in range(lay.group):
        for x_ref, xp_ref, xn_ref, o_ref, w, scale in ((q_ref, qp_ref, qn_ref, qo_ref, cw_ref[0], A_HEAD_DIM ** -0.5),
                                                       (k_ref, kp_ref, kn_ref, ko_ref, cw_ref[1], None)):
            x = lay.load(x_ref, i)
            prev_row = lay.row(x_ref, i - 1, L - 1) if i > 0 else lay.halo_prev_row(xp_ref) * has_prev
            next_row = lay.row(x_ref, i + 1, 0) if i < lay.group - 1 else lay.halo_next_row(xn_ref) * has_next
            x_dn = jnp.where(row == 0, prev_row, pltpu.roll(x, 1, 0))
            x_up = jnp.where(row == L - 1, next_row, pltpu.roll(x, L - 1, 0))
            y = _silu(w[0:1, :] * x_dn + w[1:2, :] * x + w[2:3, :] * x_up)
            lay.store(o_ref, i, y if scale is None else y * scale)

        u = _gelu_tanh(lay.load(u_ref, i))
        v = _gelu_tanh(lay.load(v_ref, i))
        v = v * _rms_scale(v, B_WIDTH) * ng_ref[...]
        mixed = bias_ref[...]
        for gi in range(B_GROUPS):
            vg = jnp.where(grp == gi, v, 0.0).astype(BF16)
            mixed = mixed + jnp.dot(w_ref[gi], vg, preferred_element_type=F32)
        lay.store(s_ref, i, u * mixed)


def _seq_prep_call(st, conv_w, norm_g, w_s, bias, lay):
    bsz = st["qa"].shape[0]
    lead = lambda b, g: (b,)
    tile = lambda c: lay.spec(c, lead)
    hp, hn = lay.halo_specs(A_PAD)
    full = lambda shp: pl.BlockSpec(shp, lambda b, g: (0,) * len(shp))
    qa, ka, ub, vb = (lay.view(st[n]) for n in ("qa", "ka", "ub", "vb"))
    qo, ko, s = pl.pallas_call(
        functools.partial(_seq_prep_kernel, lay),
        grid=(bsz, lay.ngrp),
        in_specs=[tile(A_PAD), hp, hn, tile(A_PAD), hp, hn, tile(B_WIDTH), tile(B_WIDTH),
                  full((2, 3, A_PAD)), full((1, B_WIDTH)), full((B_GROUPS, SEQ_BLOCK, SEQ_BLOCK)),
                  full((SEQ_BLOCK, B_WIDTH))],
        out_specs=[tile(A_PAD), tile(A_PAD), tile(B_WIDTH)],
        out_shape=[jax.ShapeDtypeStruct(qa.shape, F32), jax.ShapeDtypeStruct(ka.shape, F32),
                   jax.ShapeDtypeStruct(ub.shape, F32)],
        compiler_params=_cparams(2),
        name="seq_prep",
    )(qa, qa, qa, ka, ka, ka, ub, vb, conv_w, norm_g, w_s, bias)
    return lay.unview(qo), lay.unview(ko), lay.unview(s)


def _mlstm_kernel(lay, reverse, q_ref, k_ref, v_ref, g_ref, gb_ref, cmask_ref, spread_ref, rowsel_ref,
                  cinit_ref, minit_ref, *rest):
    other_ref = rest[0] if reverse else None
    h_ref, cfin_ref, mfin_ref, c_sc, m_sc = rest[1:] if reverse else rest
    gidx = pl.program_id(1)
    L = SEQ_BLOCK

    @pl.when(gidx == 0)
    def _():
        c_sc[...] = cinit_ref[...]
        m_sc[...] = minit_ref[...]

    glane = lax.broadcasted_iota(jnp.int32, (1, LANES), 1)
    off = GA_OFF(1 if reverse else 0)
    is_i = (glane >= off) & (glane < off + A_HEADS)
    ri = lax.broadcasted_iota(jnp.int32, (L, L), 0)
    ci = lax.broadcasted_iota(jnp.int32, (L, L), 1)
    past = (ci >= ri) if reverse else (ci <= ri)
    last = 0 if reverse else L - 1
    ones = jnp.ones((L, HEAD_PAD), F32)
    twice = lambda x: jnp.concatenate([x, x], axis=1)

    def chunk(ii, carry):
        i = lay.group - 1 - ii if reverse else ii
        rows = range(MLSTM_BATCH)
        qf = [lay.load(q_ref, i, (bb,)) for bb in rows]
        q = [qf[bb].astype(BF16) for bb in rows]
        k = [lay.load(k_ref, i, (bb,)).astype(BF16) for bb in rows]
        v = [lay.load(v_ref, i, (bb,)) for bb in rows]
        g = [lay.load(g_ref, i, (bb,)) + gb_ref[0] for bb in rows]
        lf = [_split_hi_lo(_log_sigmoid(g[bb])) for bb in rows]
        cs = [jnp.dot(cmask_ref[...], jnp.concatenate(lf[bb], axis=1), preferred_element_type=F32) for bb in rows]
        bc = [cs[bb][:, 0:LANES] + cs[bb][:, LANES:2 * LANES] for bb in rows]
        x = [jnp.concatenate(_split_hi_lo(jnp.where(is_i, g[bb], bc[bb])), axis=1) for bb in rows]
        spread_all = jnp.dot(jnp.concatenate(x, axis=0), spread_ref[...], preferred_element_type=F32)
        spread = [spread_all[bb * L:(bb + 1) * L] for bb in rows]
        c_rows = [lax.dot_general(rowsel_ref[...], x[bb], (((1,), (1,)), ((), ())), preferred_element_type=F32)
                  for bb in rows]

        units = [(bb, h) for bb in rows for h in range(A_HEADS)]
        each = lambda f: {u: f(*u) for u in units}
        sl = lambda h: slice(h * HEAD_PAD, (h + 1) * HEAD_PAD)
        b_rep = each(lambda bb, h: spread[bb][:, sl(h)])
        c_rep = each(lambda bb, h: spread[bb][:, sl(A_HEADS + h)])
        c_row = each(lambda bb, h: c_rows[bb][h:h + 1, :])
        m_prev = each(lambda bb, h: m_sc[bb, h:h + 1, :])
        c2 = each(lambda bb, h: c_sc[bb, h])
        v2 = each(lambda bb, h: jnp.concatenate([v[bb][:, sl(h)], ones], axis=1))
        u_rep = each(lambda bb, h: jnp.maximum(
            m_prev[bb, h], jnp.max(jnp.where(past, c_row[bb, h], -jnp.inf), axis=1, keepdims=True)))
        qk = each(lambda bb, h: lax.dot_general(q[bb][:, sl(h)], k[bb][:, sl(h)], (((1,), (1,)), ((), ())),
                                                preferred_element_type=F32))
        s = each(lambda bb, h: (qk[bb, h] * jnp.exp(jnp.where(past, c_row[bb, h] - u_rep[bb, h], -jnp.inf))
                                ).astype(BF16))
        inter = each(lambda bb, h: jnp.exp(m_prev[bb, h] - u_rep[bb, h]))
        num2 = each(lambda bb, h: jnp.dot(
            jnp.concatenate([s[bb, h], (inter[bb, h] * qf[bb][:, sl(h)]).astype(BF16)], axis=1),
            jnp.concatenate([v2[bb, h], c2[bb, h]], axis=0).astype(BF16), preferred_element_type=F32))
        inv = each(lambda bb, h: 1.0 / jnp.maximum(jnp.abs(num2[bb, h][:, HEAD_PAD:]),
                                                   jnp.exp(-(b_rep[bb, h] + u_rep[bb, h]))))
        for bb in rows:
            out = jnp.concatenate([num2[bb, h][:, :HEAD_PAD] * inv[bb, h] for h in range(A_HEADS)], axis=1)
            if other_ref is not None:
                out = out + lay.load(other_ref, i, (bb,))
            lay.store(h_ref, i, out, (bb,))

        b_last = each(lambda bb, h: b_rep[bb, h][last:last + 1, :])
        w_log = each(lambda bb, h: b_last[bb, h] + c_rep[bb, h])
        m_new = each(lambda bb, h: jnp.maximum(b_last[bb, h] + m_prev[bb, h],
                                               jnp.max(w_log[bb, h], axis=0, keepdims=True)))
        wv2 = each(lambda bb, h: (twice(jnp.exp(w_log[bb, h] - m_new[bb, h])) * v2[bb, h]).astype(BF16))
        for bb, h in units:
            upd = lax.dot_general(k[bb][:, sl(h)], wv2[bb, h], (((0,), (0,)), ((), ())), preferred_element_type=F32)
            decay = jnp.exp(b_last[bb, h] + m_prev[bb, h] - m_new[bb, h])
            c_sc[bb, h] = twice(decay) * c2[bb, h] + upd
            m_sc[bb, h:h + 1, :] = m_new[bb, h]
        return carry

    lax.fori_loop(0, lay.group, chunk, 0)

    @pl.when(gidx == lay.ngrp - 1)
    def _():
        cfin_ref[...] = c_sc[...]
        mfin_ref[...] = m_sc[...]


def _mlstm_gate_matrices(off):
    lane = jnp.arange(LANES) - off
    blocks = []
    for h in range(A_HEADS):
        blocks.append(jnp.broadcast_to((lane == A_HEADS + h).astype(F32)[:, None], (LANES, LANES)))
    for h in range(A_HEADS):
        col = (lane == h).astype(F32) - (lane == A_HEADS + h).astype(F32)
        blocks.append(jnp.broadcast_to(col[:, None], (LANES, LANES)))
    half = jnp.concatenate(blocks, axis=1)
    spread = jnp.concatenate([half, half], axis=0).astype(BF16)
    sel = jnp.stack([(lane == h).astype(F32) - (lane == A_HEADS + h).astype(F32) for h in range(A_HEADS)]
                    + [jnp.zeros((LANES,), F32)] * (2 * SUBLANES - A_HEADS))
    rowsel = jnp.concatenate([sel, sel], axis=1).astype(BF16)
    return spread, rowsel


def _mlstm_call(qconv, kconv, st, gate_b, cinit, minit, lay, reverse, other=None):
    bsz = qconv.shape[0]
    di = 1 if reverse else 0
    nb = MLSTM_BATCH
    tile = lambda c: lay.spec(c, lambda b, g: (b,), reverse, batch=nb)
    full = lambda shp: pl.BlockSpec(shp, lambda b, g: (0,) * len(shp))
    c_spec = pl.BlockSpec((nb, A_HEADS, HEAD_PAD, 2 * HEAD_PAD), lambda b, g: (b, 0, 0, 0))
    m_spec = pl.BlockSpec((nb, SUBLANES, LANES), lambda b, g: (b, 0, 0))
    qa, ka, va, ga = lay.view(qconv), lay.view(kconv), lay.view(st["va"]), lay.view(st["gt"])
    cmask, _ = _scan_masks(reverse)
    spread, rowsel = _mlstm_gate_matrices(GA_OFF(di))
    h, cfin, mfin = pl.pallas_call(
        functools.partial(_mlstm_kernel, lay, reverse),
        grid=(bsz // nb, lay.ngrp),
        in_specs=[tile(A_PAD), tile(A_PAD), tile(A_PAD), tile(LANES),
                  pl.BlockSpec((1, 1, LANES), lambda b, g: (di, 0, 0)),
                  full(cmask.shape), full(spread.shape), full(rowsel.shape),
                  c_spec, m_spec] + ([tile(A_PAD)] if reverse else []),
        out_specs=[tile(A_PAD), c_spec, m_spec],
        out_shape=[jax.ShapeDtypeStruct(qa.shape, F32),
                   jax.ShapeDtypeStruct(cinit.shape, F32),
                   jax.ShapeDtypeStruct(minit.shape, F32)],
        scratch_shapes=[pltpu.VMEM((nb, A_HEADS, HEAD_PAD, 2 * HEAD_PAD), F32),
                        pltpu.VMEM((nb, SUBLANES, LANES), F32)],
        compiler_params=_cparams(2),
        name="mlstm_scan_bwd" if reverse else "mlstm_scan_fwd",
    )(qa, ka, va, ga, gate_b, cmask, spread, rowsel, cinit, minit, *([lay.view(other)] if reverse else []))
    return lay.unview(h), cfin, mfin


def _gla_kernel(lay, reverse, q_ref, k_ref, v_ref, gk_ref, w2_ref, bgk_ref, cmask_ref, lmask_ref, sinit_ref,
                *rest):
    other_ref = rest[0] if reverse else None
    o_ref, sfin_ref, s_sc, sbd_sc = rest[1:] if reverse else rest
    gidx = pl.program_id(1)
    L = SEQ_BLOCK

    @pl.when(gidx == 0)
    def _():
        s_sc[...] = sinit_ref[...]
        sbd_sc[...] = jnp.zeros_like(sbd_sc)
        for bb in range(GLA_BATCH):
            for h in range(C_HEADS):
                sbd_sc[bb, h * CK_PAD:(h + 1) * CK_PAD, h * HEAD_PAD:(h + 1) * HEAD_PAD] = \
                    sinit_ref[bb, h].astype(BF16)

    glane = lax.broadcasted_iota(jnp.int32, (1, LANES), 1)
    gk_lo = GK_OFF(1 if reverse else 0) + 2 * C_GATE_RANK
    row = lax.broadcasted_iota(jnp.int32, (L, 1), 0)
    hlane = glane // CK_PAD
    last = 0 if reverse else L - 1
    ones = jnp.ones((L, HEAD_PAD), BF16)

    def rows_from(x, idx_of_row_block, rows_per_block):
        n = L // rows_per_block
        return jnp.concatenate([jnp.broadcast_to(x[idx_of_row_block(i):idx_of_row_block(i) + 1],
                                                 (rows_per_block, x.shape[1])) for i in range(n)], axis=0)

    def chunk(ii, carry):
        i = lay.group - 1 - ii if reverse else ii
        rows = range(GLA_BATCH)
        heads = range(C_HEADS)
        g = [lay.load(gk_ref, i, (bb,)) for bb in rows]
        g_hi = [g[bb].astype(BF16).astype(F32) for bb in rows]
        g_split = [jnp.where(glane < gk_lo, g_hi[bb], g[bb] - g_hi[bb]).astype(BF16) for bb in rows]
        gk_all = jnp.dot(jnp.concatenate(g_split, axis=0), w2_ref[0], preferred_element_type=F32)
        la = [_log_sigmoid(gk_all[bb * L:(bb + 1) * L] + bgk_ref[0]) * (1.0 / C_GATE_NORM) for bb in rows]
        la_split = [_split_hi_lo(la[bb]) for bb in rows]
        cs = [jnp.dot(cmask_ref[...], jnp.concatenate(la_split[bb], axis=1), preferred_element_type=F32)
              for bb in rows]
        bs = [cs[bb][:, 0:CK_W] + cs[bb][:, CK_W:2 * CK_W] for bb in rows]
        qs = [lay.load(q_ref, i, (bb,)) * (C_HEAD_DK ** -0.5) for bb in rows]
        kk = [lay.load(k_ref, i, (bb,)) for bb in rows]
        vb = [lay.load(v_ref, i, (bb,)).astype(BF16) for bb in rows]
        att = [[None] * C_HEADS for _ in rows]

        def add_level(bb, q_t, k_mul, level, k_rows=None):
            k_all = kk[bb] if k_mul is None else kk[bb] * k_mul
            if k_rows is not None:
                k_all = jnp.where(k_rows, k_all, 0.0)
            k_b = k_all.astype(BF16)
            for p in range(C_HEADS // 2):
                tile = slice(p * LANES, (p + 1) * LANES)
                q_pair = q_t[:, tile]
                q_stack = jnp.concatenate([jnp.where(hlane == 0, q_pair, 0.0), jnp.where(hlane == 1, q_pair, 0.0)],
                                          axis=0).astype(BF16)
                a2 = lax.dot_general(q_stack, k_b[:, tile], (((1,), (1,)), ((), ())), preferred_element_type=F32)
                for j in range(2):
                    h = 2 * p + j
                    a = a2[j * L:(j + 1) * L]
                    if level is not None:
                        a = a * lmask_ref[level]
                    att[bb][h] = a if att[bb][h] is None else att[bb][h] + a

        def level_factors(bb, m):
            bnd = m if reverse else m - 1
            if m >= SUBLANES:
                zero = jnp.zeros((m, CK_W), F32)
                q_parts, k_parts = [], []
                for base in range(0, L, 2 * m):
                    pref = bs[bb][base + bnd:base + bnd + 1]
                    lo, hi = slice(base, base + m), slice(base + m, base + 2 * m)
                    q_half, k_half = (lo, hi) if reverse else (hi, lo)
                    q_piece = qs[bb][q_half] * jnp.exp(bs[bb][q_half] - pref)
                    k_piece = jnp.exp(pref - bs[bb][k_half])
                    q_parts += [q_piece, zero] if reverse else [zero, q_piece]
                    k_parts += [zero, k_piece] if reverse else [k_piece, zero]
                return jnp.concatenate(q_parts, axis=0), jnp.concatenate(k_parts, axis=0)
            pos = row % (2 * m)
            is_q = (pos < m) if reverse else (pos >= m)
            if m == 1:
                return jnp.where(is_q, qs[bb] * jnp.exp(la[bb]), 0.0), None
            if 2 * m >= SUBLANES:
                pref = rows_from(bs[bb], lambda r: r * 2 * m + bnd, 2 * m)
            else:
                p0 = rows_from(bs[bb], lambda r: r * SUBLANES + bnd, SUBLANES)
                p1 = rows_from(bs[bb], lambda r: r * SUBLANES + 2 * m + bnd, SUBLANES)
                pref = jnp.where(row % SUBLANES < 2 * m, p0, p1)
            return (qs[bb] * jnp.exp(jnp.where(is_q, bs[bb] - pref, -jnp.inf)),
                    jnp.exp(jnp.where(is_q, -jnp.inf, pref - bs[bb])))

        level = 0
        m = L // 2
        while m >= 1:
            key_rows = (row % 2 == (1 if reverse else 0)) if m == 1 else None
            for bb in rows:
                q_t, k_mul = level_factors(bb, m)
                add_level(bb, q_t, k_mul, None if m == L // 2 else level, key_rows)
            level += 1
            m //= 2
        for bb in rows:
            add_level(bb, qs[bb], None, level)

        o_inter = [jnp.dot((qs[bb] * jnp.exp(bs[bb])).astype(BF16), sbd_sc[bb], preferred_element_type=F32)
                   for bb in rows]
        for bb in rows:
            o_heads = []
            for h in heads:
                cols = slice(h * HEAD_PAD, (h + 1) * HEAD_PAD)
                o_heads.append(o_inter[bb][:, cols] + jnp.dot(att[bb][h].astype(BF16), vb[bb][:, cols],
                                                              preferred_element_type=F32))
            out = jnp.concatenate(o_heads, axis=1)
            if other_ref is not None:
                out = out + lay.load(other_ref, i, (bb,))
            lay.store(o_ref, i, out, (bb,))

        ones2 = jnp.concatenate([ones, ones], axis=0)
        for bb in rows:
            btot = bs[bb][last:last + 1]
            ke_t = (kk[bb] * jnp.exp(btot - bs[bb])).T.astype(BF16)
            la_cat = jnp.concatenate(la_split[bb], axis=0)
            dec_col = jnp.exp(lax.dot_general(la_cat, ones2, (((0,), (0,)), ((), ())),
                                              preferred_element_type=F32))
            for h in heads:
                krows = slice(h * CK_PAD, (h + 1) * CK_PAD)
                cols = slice(h * HEAD_PAD, (h + 1) * HEAD_PAD)
                upd = jnp.dot(ke_t[krows], vb[bb][:, cols], preferred_element_type=F32)
                s_new = dec_col[krows] * s_sc[bb, h] + upd
                s_sc[bb, h] = s_new
                sbd_sc[bb, krows, cols] = s_new.astype(BF16)
        return carry

    lax.fori_loop(0, lay.group, chunk, 0)

    @pl.when(gidx == lay.ngrp - 1)
    def _():
        sfin_ref[...] = s_sc[...]


def _scan_masks(reverse):
    t = jnp.arange(SEQ_BLOCK)[:, None]
    u = jnp.arange(SEQ_BLOCK)[None, :]
    cmask = ((u >= t) if reverse else (u <= t)).astype(BF16)
    sizes = []
    m = SEQ_BLOCK
    while m >= 1:
        sizes.append(m)
        m //= 2
    lmask = jnp.stack([(t // sz) == (u // sz) for sz in sizes]).astype(F32)
    return cmask, lmask


def _gla_call(st, w2s, b_gk, sinit, lay, reverse, other=None):
    bsz = st["qc"].shape[0]
    di = 1 if reverse else 0
    nb = GLA_BATCH
    tile = lambda c: lay.spec(c, lambda b, g: (b,), reverse, batch=nb)
    full = lambda shp: pl.BlockSpec(shp, lambda b, g: (0,) * len(shp))
    state_spec = pl.BlockSpec((nb, C_HEADS, CK_PAD, HEAD_PAD), lambda b, g: (b, 0, 0, 0))
    qc, kc, vc, gk = (lay.view(st[n]) for n in ("qc", "kc", "vc", "gt"))
    cmask, lmask = _scan_masks(reverse)
    o, sfin = pl.pallas_call(
        functools.partial(_gla_kernel, lay, reverse),
        grid=(bsz // nb, lay.ngrp),
        in_specs=[tile(CK_W), tile(CK_W), tile(CV_W), tile(LANES),
                  pl.BlockSpec((1, LANES, CK_W), lambda b, g: (di, 0, 0)),
                  pl.BlockSpec((1, 1, CK_W), lambda b, g: (di, 0, 0)),
                  full(cmask.shape), full(lmask.shape), state_spec] + ([tile(CV_W)] if reverse else []),
        out_specs=[tile(CV_W), state_spec],
        out_shape=[jax.ShapeDtypeStruct(vc.shape, F32),
                   jax.ShapeDtypeStruct(sinit.shape, F32)],
        scratch_shapes=[pltpu.VMEM((nb, C_HEADS, CK_PAD, HEAD_PAD), F32),
                        pltpu.VMEM((nb, CK_W, CV_W), BF16)],
        compiler_params=_cparams(2),
        name="gla_scan_bwd" if reverse else "gla_scan_fwd",
    )(qc, kc, vc, gk, w2s, b_gk, cmask, lmask, sinit, *([lay.view(other)] if reverse else []))
    return lay.unview(o), sfin


def _mix_ffn_kernel(final_norm, x_ref, mod_ref, ha_ref, hc_ref, oa_ref, gc_ref, s_ref, na_ref, nc_ref, wmix_ref,
                    g2_ref, wg_ref, wu_ref, wd_ref, fg_ref, o_ref):
    def head_norm(xh, gain):
        return xh * _rms_scale(xh, A_HEAD_DIM) * gain

    a_parts, c_parts = [], []
    for h in range(A_HEADS):
        sl = slice(h * HEAD_PAD, (h + 1) * HEAD_PAD)
        a = head_norm(ha_ref[0, h], na_ref[:, sl])
        c = head_norm(hc_ref[0, h], nc_ref[:, sl])
        a_parts.append((a * jax.nn.sigmoid(oa_ref[0, h].astype(F32))).astype(BF16))
        c_parts.append((c * _silu(gc_ref[0, h].astype(F32))).astype(BF16))
    s_parts = [s_ref[0, j].astype(BF16) for j in range(B_WIDTH // LANES)]
    mixed = jnp.concatenate(a_parts + s_parts + c_parts, axis=1)
    acc = jnp.dot(mixed, wmix_ref[...], preferred_element_type=F32)
    x1 = x_ref[0] + mod_ref[0, 2:3, :] * acc

    y = x1 * _rms_scale(x1, x1.shape[-1]) * g2_ref[...]
    hmod = (y * (1.0 + mod_ref[0, 4:5, :]) + mod_ref[0, 3:4, :]).astype(BF16)
    gate = jnp.dot(hmod, wg_ref[...], preferred_element_type=F32)
    up = jnp.dot(hmod, wu_ref[...], preferred_element_type=F32)
    act = (_silu(gate) * up).astype(BF16)
    x2 = x1 + mod_ref[0, 5:6, :] * jnp.dot(act, wd_ref[...], preferred_element_type=F32)
    if final_norm:
        x2 = x2 * _rms_scale(x2, x2.shape[-1]) * fg_ref[...]
    o_ref[0] = x2


def _mix_ffn_call(x, mod, ha, hc, oa, gc, s, na, nc, w_mix, g2, w_ffn_in, w_ffn_out, final_g, final_norm, tm):
    bsz, t, d = x.shape
    d_ff = w_ffn_out.shape[0]
    mod_map = (lambda b, i: (b, 0, 0)) if mod.shape[0] > 1 else (lambda b, i: (0, 0, 0))
    row = lambda c: pl.BlockSpec((1, tm, c), lambda b, i: (b, i, 0))
    chunked = lambda c: pl.BlockSpec((1, c // LANES, tm, LANES), lambda b, i: (b, 0, i, 0))
    full = lambda shp: pl.BlockSpec(shp, lambda b, i: (0,) * len(shp))
    half = lambda j: pl.BlockSpec((d, d_ff), lambda b, i: (0, j), pipeline_mode=pl.Buffered(1))
    return pl.pallas_call(
        functools.partial(_mix_ffn_kernel, final_norm),
        grid=(bsz, t // tm),
        in_specs=[row(d), pl.BlockSpec((1, 6, d), mod_map), chunked(A_PAD), chunked(CV_W),
                  chunked(A_PAD), chunked(CV_W), chunked(B_WIDTH),
                  full((1, A_PAD)), full((1, CV_W)), _resident(w_mix.shape, 2),
                  full((1, d)), half(0), half(1), _resident(w_ffn_out.shape, 2), full((1, d))],
        out_specs=row(d),
        out_shape=jax.ShapeDtypeStruct(x.shape, F32),
        compiler_params=_cparams(2),
        name="mix_ffn",
    )(x, mod, ha, hc, oa, gc, s, na, nc, w_mix, g2, w_ffn_in, w_ffn_in, w_ffn_out, final_g)


def _pad_heads(w, n_heads, dim, pad, axis=-1):
    w = jnp.moveaxis(w, axis, -1)
    lead = w.shape[:-1]
    w = w.reshape(lead + (n_heads, dim))
    w = jnp.pad(w, [(0, 0)] * len(lead) + [(0, 0), (0, pad - dim)])
    return jnp.moveaxis(w.reshape(lead + (n_heads * pad,)), -1, axis)


def _pad_to(w, width):
    return jnp.pad(w, [(0, 0)] * (w.ndim - 1) + [(0, width - w.shape[-1])])


def _prep_w_in(w_in):
    a_w = A_HEADS * A_HEAD_DIM
    ck = C_HEADS * C_HEAD_DK
    cv = C_HEADS * C_HEAD_DV
    sizes = (a_w, a_w, a_w, a_w, N_DIR * 2 * A_HEADS, B_WIDTH, B_WIDTH, ck, ck, cv, cv, N_DIR * C_GATE_RANK)
    pts = [sum(sizes[:i + 1]) for i in range(len(sizes) - 1)]
    qa, ka, va, oa, ga, ub, vb, qc, kc, vc, gc, gkc = jnp.split(w_in, pts, axis=-1)
    pa = lambda w: _pad_heads(w, A_HEADS, A_HEAD_DIM, HEAD_PAD)
    pk = lambda w: _pad_heads(w, C_HEADS, C_HEAD_DK, CK_PAD)
    pv = lambda w: _pad_heads(w, C_HEADS, C_HEAD_DV, HEAD_PAD)
    gates = _pad_to(jnp.concatenate([ga, jnp.tile(gkc[:, :C_GATE_RANK], (1, GK_COPIES)),
                                     jnp.tile(gkc[:, C_GATE_RANK:], (1, GK_COPIES))], axis=1), LANES)
    cols = [pa(qa), pa(ka), pa(va), pa(oa), gates, ub, vb, pk(qc), pk(kc), pv(vc), pv(gc)]
    return jnp.concatenate(cols, axis=-1).astype(BF16)


def _prep_w_gk2(w_gk2):
    w = _pad_heads(w_gk2, C_HEADS, C_HEAD_DK, CK_PAD)
    hi = w.astype(BF16)
    lo = (w - hi.astype(F32)).astype(BF16)
    stacked = jnp.concatenate([hi, lo, hi], axis=1)
    return jnp.stack([jnp.pad(stacked[d], ((GK_OFF(d), LANES - GK_OFF(d) - GK_COPIES * C_GATE_RANK), (0, 0)))
                      for d in range(N_DIR)])


def _prep_w_out(w_out):
    a_w = A_HEADS * A_HEAD_DIM
    wa, wb, wc = w_out[:a_w], w_out[a_w:a_w + B_WIDTH], w_out[a_w + B_WIDTH:]
    wa = _pad_heads(wa, A_HEADS, A_HEAD_DIM, HEAD_PAD, axis=0)
    wc = _pad_heads(wc, C_HEADS, C_HEAD_DV, HEAD_PAD, axis=0)
    return jnp.concatenate([wa, wb, wc], axis=0).astype(BF16)


def _mixer_scans(st, lw, a_state, c_state, lay):
    qconv, kconv, s = _seq_prep_call(st, lw["conv"], lw["sgu_g"], lw["sgu_w"], lw["sgu_bias"], lay)
    ha, hc, a_fin, c_fin = None, None, [], []
    for di, reverse in enumerate((False, True)):
        ha, cfin, mfin = _mlstm_call(qconv, kconv, st, lw["gate_b"], a_state[di][0], a_state[di][1], lay, reverse, ha)
        a_fin.append((cfin, mfin))
        hc, sfin = _gla_call(st, lw["w_gk2"], lw["b_gk"], c_state[di], lay, reverse, hc)
        c_fin.append(sfin)
    return ha, hc, s, a_fin, c_fin


def kernel(x, c, ctx, c_ctx, norm1_g, norm2_g, w_ada, b_ada, w_in, mlstm_conv, mlstm_gate_b,
           mlstm_norm_g, gla_w_gk2, gla_b_gk, gla_norm_g, sgu_norm_g, sgu_w, sgu_b, w_out,
           w_ffn_in, w_ffn_out, final_g):
    bsz, seq, d = x.shape
    ctx_len = ctx.shape[1]
    depth = w_in.shape[0]
    n_cond = 2 * SUBLANES
    cond = jnp.zeros((n_cond, d), F32).at[:bsz].set(c).at[bsz].set(c_ctx)
    mods = _ada_call(cond, w_ada, b_ada).reshape(depth, n_cond, 6, d)

    fg = final_g.reshape(1, d)
    x_lat, x_ctx = x, ctx
    for l in range(depth):
        need_ctx = l < depth - 1
        mod_lat = mods[l, :bsz]
        mod_ctx = mods[l, bsz:bsz + 1]
        gb = mlstm_gate_b[l].reshape(N_DIR, 1, 2 * A_HEADS)
        lw = {
            "conv": jnp.stack([_pad_heads(mlstm_conv[l][:, :A_HEADS * A_HEAD_DIM], A_HEADS, A_HEAD_DIM, HEAD_PAD),
                               _pad_heads(mlstm_conv[l][:, A_HEADS * A_HEAD_DIM:], A_HEADS, A_HEAD_DIM, HEAD_PAD)]),
            "gate_b": jnp.stack([jnp.pad(gb[d], ((0, 0), (GA_OFF(d), LANES - GA_OFF(d) - 2 * A_HEADS)))
                                 for d in range(N_DIR)]),
            "w_gk2": _prep_w_gk2(gla_w_gk2[l]),
            "b_gk": _pad_heads(gla_b_gk[l], C_HEADS, C_HEAD_DK, CK_PAD).reshape(N_DIR, 1, CK_W),
            "sgu_g": sgu_norm_g[l].reshape(1, B_WIDTH),
            "sgu_w": sgu_w[l].astype(BF16),
            "sgu_bias": jnp.repeat(sgu_b[l].T, B_GROUP_DIM, axis=1),
        }
        w_in_l = _prep_w_in(w_in[l])
        w_out_l = _prep_w_out(w_out[l])
        na = _pad_heads(mlstm_norm_g[l], A_HEADS, A_HEAD_DIM, HEAD_PAD).reshape(1, A_PAD)
        nc = _pad_heads(jnp.tile(gla_norm_g[l], C_HEADS), C_HEADS, C_HEAD_DV, HEAD_PAD).reshape(1, CV_W)
        g1 = norm1_g[l].reshape(1, d)
        g2 = norm2_g[l].reshape(1, d)
        w_ffn_in_l = w_ffn_in[l].astype(BF16)
        w_ffn_out_l = w_ffn_out[l].astype(BF16)

        st_ctx = _proj_in_call(x_ctx, mod_ctx, g1, w_in_l, tm=_row_tile(ctx_len, PROJ_ROWS))
        st_lat = _proj_in_call(x_lat, mod_lat, g1, w_in_l, tm=_row_tile(seq, PROJ_ROWS))

        a0 = (jnp.zeros((bsz, A_HEADS, HEAD_PAD, 2 * HEAD_PAD), F32), jnp.zeros((bsz, SUBLANES, LANES), F32))
        c0 = jnp.zeros((bsz, C_HEADS, CK_PAD, HEAD_PAD), F32)
        lay_ctx = _SeqLayout(ctx_len, False)
        lay_lat = _SeqLayout(seq, l % 2 == 1)
        ha_c, hc_c, s_c, a_state, c_state = _mixer_scans(st_ctx, lw, (a0, a0), (c0, c0), lay_ctx)
        ha_l, hc_l, s_l, _, _ = _mixer_scans(st_lat, lw, a_state, c_state, lay_lat)

        x_lat = _mix_ffn_call(x_lat, mod_lat, ha_l, hc_l, st_lat["oa"], st_lat["gc"], s_l, na, nc, w_out_l,
                              g2, w_ffn_in_l, w_ffn_out_l, fg, not need_ctx, tm=_row_tile(seq, TAIL_ROWS))
        if need_ctx:
            x_ctx = _mix_ffn_call(x_ctx, mod_ctx, ha_c, hc_c, st_ctx["oa"], st_ctx["gc"], s_c, na, nc, w_out_l,
                                  g2, w_ffn_in_l, w_ffn_out_l, fg, False, tm=_row_tile(ctx_len, TAIL_ROWS))
    return x_lat
```
